```python
import math
import jax, jax.numpy as jnp
from jax import lax
import numpy as np

D_MODEL = 2048
BATCH = 4
SEQ = 4096
DEPTH = 2

N_MEM = 256
EPS = 1e-6
GLA_HEADS = 4
GLA_DV = D_MODEL // 2 // GLA_HEADS
GLA_DK = GLA_DV // 2
GLA_RANK = 16
GLA_TAU = 16.0
GLA_CHUNK = 64
DSA_HEADS = 8
DSA_DH = D_MODEL // 2 // DSA_HEADS
IDX_HEADS = 16
IDX_DH = 64
TOPK_MAX = 256
DSA_QBLOCK = 64
REL_BUCKETS = 32
REL_MAX_DIST = 128
XA_HEADS = 4
XA_DH = D_MODEL // XA_HEADS
D_FF = -(-8 * D_MODEL // (3 * 256)) * 256
IN_SPLITS = [
    GLA_HEADS * GLA_DK,
    GLA_HEADS * GLA_DK,
    GLA_HEADS * GLA_DV,
    GLA_RANK,
    GLA_HEADS * GLA_DV,
    DSA_HEADS * DSA_DH,
    DSA_HEADS * DSA_DH,
    DSA_HEADS * DSA_DH,
    IDX_HEADS * IDX_DH,
    IDX_DH,
    IDX_HEADS,
]
D_IN = sum(IN_SPLITS)

kernel_name = "hymba_gla_dsa_sandwich_block"


def rmsnorm(x, g):
    xf = x.astype(jnp.float32)
    y = xf * lax.rsqrt(jnp.mean(xf * xf, axis=-1, keepdims=True) + EPS) * g.astype(jnp.float32)
    return y.astype(x.dtype)


def gla_mixer(q, k, v, z, r, w_gate_up, b_gate, g_head):
    f32 = jnp.float32
    B, L, _ = q.shape
    C = GLA_CHUNK
    N = L // C
    logg = jax.nn.log_sigmoid(z.astype(f32) @ w_gate_up.astype(f32) + b_gate.astype(f32)) / GLA_TAU

    def heads(t, d):
        return t.astype(f32).reshape(B, N, C, GLA_HEADS, d).transpose(0, 1, 3, 2, 4)

    qc = heads(q, GLA_DK) * GLA_DK ** -0.5
    kc = heads(k, GLA_DK)
    vc = heads(v, GLA_DV)
    bcum = jnp.cumsum(heads(logg, GLA_DK), axis=3)
    b_last = bcum[:, :, :, -1:, :]
    q_dec = qc * jnp.exp(bcum)
    k_intra = kc * jnp.exp(-bcum)
    k_state = kc * jnp.exp(b_last - bcum)
    causal = jnp.tril(jnp.ones((C, C), dtype=bool))
    a = jnp.where(causal, jnp.einsum('bnhid,bnhjd->bnhij', q_dec, k_intra), 0.0)
    o_intra = jnp.einsum('bnhij,bnhjv->bnhiv', a, vc)
    upd = jnp.einsum('bnhjd,bnhjv->bnhdv', k_state, vc)
    decay = jnp.exp(b_last[:, :, :, 0, :])

    def step(s, inp):
        u, d = inp
        return d[..., None] * s + u, s

    s0 = jnp.zeros((B, GLA_HEADS, GLA_DK, GLA_DV), f32)
    _, s_prev = lax.scan(step, s0, (upd.transpose(1, 0, 2, 3, 4), decay.transpose(1, 0, 2, 3)))
    s_prev = s_prev.transpose(1, 0, 2, 3, 4)
    o = o_intra + jnp.einsum('bnhid,bnhdv->bnhiv', q_dec, s_prev)
    o = o * lax.rsqrt(jnp.mean(o * o, axis=-1, keepdims=True) + EPS) * g_head.astype(f32)
    o = o.transpose(0, 1, 3, 2, 4).reshape(B, L, GLA_HEADS * GLA_DV)
    return (o * jax.nn.silu(r.astype(f32))).astype(q.dtype)


def rel_bucket(dist):
    n = jnp.maximum(dist, 0)
    max_exact = REL_BUCKETS // 2
    nf = jnp.maximum(n, 1).astype(jnp.float32)
    large = max_exact + (jnp.log(nf / max_exact) / math.log(REL_MAX_DIST / max_exact)
                         * (REL_BUCKETS - max_exact)).astype(jnp.int32)
    large = jnp.minimum(large, REL_BUCKETS - 1)
    return jnp.where(n < max_exact, n, large)


def dsa_mixer(q, k, v, qi, ki, wi, rel_table):
    f32 = jnp.float32
    B, L, _ = q.shape
    topk = min(TOPK_MAX, L // 4)
    TQ = DSA_QBLOCK
    NB = L // TQ
    qh = q.reshape(B, L, DSA_HEADS, DSA_DH)
    kh = k.reshape(B, L, DSA_HEADS, DSA_DH)
    vh = v.reshape(B, L, DSA_HEADS, DSA_DH)
    qih = qi.reshape(B, L, IDX_HEADS, IDX_DH)
    kif = ki.astype(f32)
    w_sc = wi.astype(f32) * (IDX_HEADS ** -0.5 * IDX_DH ** -0.5)
    key_pos = jnp.arange(L)

    def blocks(t):
        return t.reshape(B, NB, TQ, *t.shape[2:]).swapaxes(0, 1)

    def one_block(inp):
        start, q_b, qi_b, w_b = inp
        q_pos = start + jnp.arange(TQ)
        rel = jax.nn.relu(jnp.einsum('bthd,bsd->bths', qi_b.astype(f32), kif))
        score = jnp.einsum('bths,bth->bts', rel, w_b)
        visible = key_pos[None, :] <= q_pos[:, None]
        score = jnp.where(visible, score, -jnp.inf)
        _, idx = lax.top_k(score, topk)
        valid = idx <= q_pos[None, :, None]
        kg = jax.vmap(lambda kb, ib: kb[ib])(kh, idx)
        vg = jax.vmap(lambda vb, ib: vb[ib])(vh, idx)
        bias = rel_table[rel_bucket(q_pos[None, :, None] - idx)]
        logits = (jnp.einsum('bthd,btkhd->bthk', q_b, kg).astype(f32) * DSA_DH ** -0.5
                  + bias.astype(f32).transpose(0, 1, 3, 2))
        logits = jnp.where(valid[:, :, None, :], logits, -1e30)
        p = jax.nn.softmax(logits, axis=-1)
        return jnp.einsum('bthk,btkhd->bthd', p.astype(vg.dtype), vg)

    starts = jnp.arange(NB, dtype=jnp.int32) * TQ
    out = lax.map(one_block, (starts, blocks(qh), blocks(qih), blocks(w_sc)))
    return out.swapaxes(0, 1).reshape(B, L, DSA_HEADS * DSA_DH)


def cross_attn(h, mem_n, wq, wkv, wo):
    B, L, _ = h.shape
    M = mem_n.shape[1]
    q = (h @ wq).reshape(B, L, XA_HEADS, XA_DH)
    k, v = jnp.split(mem_n @ wkv, 2, axis=-1)
    k = k.reshape(B, M, XA_HEADS, XA_DH)
    v = v.reshape(B, M, XA_HEADS, XA_DH)
    logits = jnp.einsum('bthd,bmhd->bhtm', q, k).astype(jnp.float32) * XA_DH ** -0.5
    p = jax.nn.softmax(logits, axis=-1)
    o = jnp.einsum('bhtm,bmhd->bthd', p.astype(v.dtype), v).reshape(B, L, D_MODEL)
    return o @ wo


def swiglu(h, w_in, w_out):
    g, u = jnp.split(h @ w_in, 2, axis=-1)
    return (jax.nn.silu(g) * u) @ w_out


def setup_inputs(seed: int = 0) -> dict:
    key = jax.random.key(seed)
    ks = jax.random.split(key, 16)
    f32 = jnp.float32
    nrm = lambda k, shape, s: jax.random.normal(k, shape, f32) * s
    return {
        "x": nrm(ks[0], (BATCH, SEQ, D_MODEL), 1.0),
        "mem": nrm(ks[1], (BATCH, N_MEM, D_MODEL), 1.0),
        "norm_gains": 1.0 + nrm(ks[2], (DEPTH, 7, D_MODEL), 0.02),
        "w_in": nrm(ks[3], (DEPTH, D_MODEL, D_IN), D_MODEL ** -0.5),
        "w_gate_up": nrm(ks[4], (DEPTH, GLA_RANK, GLA_HEADS * GLA_DK), GLA_RANK ** -0.5),
        "b_gate": nrm(ks[5], (DEPTH, GLA_HEADS * GLA_DK), 0.01),
        "gla_head_gain": 1.0 + nrm(ks[6], (DEPTH, GLA_DV), 0.02),
        "rel_table": nrm(ks[7], (REL_BUCKETS, DSA_HEADS), 0.1),
        "w_out": nrm(ks[8], (DEPTH, D_MODEL, D_MODEL), D_MODEL ** -0.5),
        "w_xq": nrm(ks[9], (DEPTH, D_MODEL, D_MODEL), D_MODEL ** -0.5),
        "w_xkv": nrm(ks[10], (DEPTH, D_MODEL, 2 * D_MODEL), D_MODEL ** -0.5),
        "w_xo": nrm(ks[11], (DEPTH, D_MODEL, D_MODEL), D_MODEL ** -0.5),
        "w_ffn_in": nrm(ks[12], (DEPTH, D_MODEL, 2 * D_FF), D_MODEL ** -0.5),
        "w_ffn_out": nrm(ks[13], (DEPTH, D_FF, D_MODEL), D_FF ** -0.5),
    }


def reference(x, mem, norm_gains, w_in, w_gate_up, b_gate, gla_head_gain, rel_table,
              w_out, w_xq, w_xkv, w_xo, w_ffn_in, w_ffn_out):
    offsets = np.cumsum(IN_SPLITS)[:-1].tolist()
    for l in range(DEPTH):
        g = norm_gains[l]
        h = rmsnorm(x, g[0])
        proj = h @ w_in[l]
        gq, gk, gv, gz, gr, dq, dk, dv, iq, ik, iw = jnp.split(proj, offsets, axis=-1)
        o_gla = gla_mixer(gq, gk, gv, gz, gr, w_gate_up[l], b_gate[l], gla_head_gain[l])
        o_dsa = dsa_mixer(dq, dk, dv, iq, ik, iw, rel_table)
        mix = jnp.concatenate([o_gla, o_dsa.astype(o_gla.dtype)], axis=-1) @ w_out[l]
        x = x + rmsnorm(mix, g[1])
        h = rmsnorm(x, g[2])
        mem_n = rmsnorm(mem, g[6])
        x = x + rmsnorm(cross_attn(h, mem_n, w_xq[l], w_xkv[l], w_xo[l]), g[3])
        h = rmsnorm(x, g[4])
        x = x + rmsnorm(swiglu(h, w_ffn_in[l], w_ffn_out[l]), g[5])
    return x
```

```python
import functools
import math

import jax
import jax.numpy as jnp
from jax import lax
from jax.experimental import pallas as pl
from jax.experimental.pallas import tpu as pltpu

F32 = jnp.float32
BF16 = jnp.bfloat16

EPS = 1e-6
GLA_HEADS = 4
GLA_RANK = 16
GLA_TAU = 16.0
GLA_CHUNK = 64
DSA_HEADS = 8
IDX_HEADS = 16
IDX_DH = 64
TOPK_MAX = 256
REL_BUCKETS = 32
REL_MAX_DIST = 128
XA_HEADS = 4

LANES = 128
VMEM_LIMIT_BYTES = 56 * 1024 * 1024

NEG_BIG = -1e30
INT_MIN = -(2 ** 31)

NT_DIMS = (((1,), (1,)), ((), ()))
TN_DIMS = (((0,), (0,)), ((), ()))


def _params(*sem):
    return pltpu.CompilerParams(dimension_semantics=sem, vmem_limit_bytes=VMEM_LIMIT_BYTES)


def _resident(shape, index_map):
    return pl.BlockSpec(shape, index_map, pipeline_mode=pl.Buffered(1))


def _rms(x, g):
    return x * lax.rsqrt(jnp.mean(x * x, axis=-1, keepdims=True) + EPS) * g


def _rms_to(x_ref, g_ref, h_ref, chunk=128):
    g = g_ref[...]

    def body(c, _):
        rows = pl.ds(pl.multiple_of(c * chunk, chunk), chunk)
        h_ref[rows, :] = _rms(x_ref[rows, :], g).astype(h_ref.dtype)
        return 0

    lax.fori_loop(0, x_ref.shape[0] // chunk, body, 0)


def _norm_matmul_kernel(x_ref, g_ref, w_ref, o_ref, h_ref):
    @pl.when(pl.program_id(1) == 0)
    def _():
        _rms_to(x_ref, g_ref, h_ref)

    o_ref[...] = jnp.dot(h_ref[...], w_ref[...], preferred_element_type=F32).astype(o_ref.dtype)


def norm_matmul(x, g, w, out_dtype, bm, bn):
    m, k = x.shape
    n = w.shape[1]
    assert m % bm == 0 and n % bn == 0
    return pl.pallas_call(
        _norm_matmul_kernel,
        grid=(m // bm, n // bn),
        in_specs=[
            pl.BlockSpec((bm, k), lambda i, j: (i, 0)),
            _resident((1, k), lambda i, j: (0, 0)),
            pl.BlockSpec((k, bn), lambda i, j: (0, j)),
        ],
        out_specs=pl.BlockSpec((bm, bn), lambda i, j: (i, j)),
        out_shape=jax.ShapeDtypeStruct((m, n), out_dtype),
        scratch_shapes=[pltpu.VMEM((bm, k), BF16)],
        compiler_params=_params("parallel", "arbitrary"),
        name="norm_matmul",
    )(x, g.reshape(1, k), w)


def _ffn_in_kernel(x_ref, g_ref, wg_ref, wu_ref, o_ref, h_ref):
    @pl.when(pl.program_id(1) == 0)
    def _():
        _rms_to(x_ref, g_ref, h_ref)

    h = h_ref[...]
    a = jnp.dot(h, wg_ref[...], preferred_element_type=F32)
    b = jnp.dot(h, wu_ref[...], preferred_element_type=F32)
    o_ref[...] = (a * (1.0 / (1.0 + jnp.exp(-a))) * b).astype(o_ref.dtype)


def ffn_in(x, g, w, d_ff, bm, bn):
    m, k = x.shape
    assert m % bm == 0 and d_ff % bn == 0
    nj = d_ff // bn
    return pl.pallas_call(
        _ffn_in_kernel,
        grid=(m // bm, nj),
        in_specs=[
            pl.BlockSpec((bm, k), lambda i, j: (i, 0)),
            _resident((1, k), lambda i, j: (0, 0)),
            pl.BlockSpec((k, bn), lambda i, j: (0, j)),
            pl.BlockSpec((k, bn), lambda i, j: (0, j + nj)),
        ],
        out_specs=pl.BlockSpec((bm, bn), lambda i, j: (i, j)),
        out_shape=jax.ShapeDtypeStruct((m, d_ff), BF16),
        scratch_shapes=[pltpu.VMEM((bm, k), BF16)],
        compiler_params=_params("parallel", "arbitrary"),
        name="ffn_in",
    )(x, g.reshape(1, k), w, w)


def _mm_norm_res_kernel(*refs, n_lhs, nk):
    lhs = refs[:n_lhs]
    w_ref, g_ref, res_ref, o_ref = refs[n_lhs:n_lhs + 4]
    if nk == 1:
        y = None
        off = 0
        for a in lhs:
            wd = a.shape[1]
            part = jnp.dot(a[...], w_ref[off:off + wd, :], preferred_element_type=F32)
            y = part if y is None else y + part
            off += wd
        o_ref[...] = res_ref[...] + _rms(y, g_ref[...])
    else:
        acc_ref = refs[n_lhs + 4]
        k = pl.program_id(1)
        part = jnp.dot(lhs[0][...], w_ref[...], preferred_element_type=F32)

        @pl.when(k == 0)
        def _():
            acc_ref[...] = part

        @pl.when(k > 0)
        def _():
            acc_ref[...] += part

        @pl.when(k == nk - 1)
        def _():
            o_ref[...] = res_ref[...] + _rms(acc_ref[...], g_ref[...])


def mm_norm_res(lhs_list, w, g, res, bm, nk=1):
    m, n = res.shape
    kk = w.shape[0]
    assert m % bm == 0 and kk % nk == 0
    bk = kk // nk
    n_lhs = len(lhs_list)
    if nk == 1:
        lhs_specs = [pl.BlockSpec((bm, a.shape[1]), lambda i, k: (i, 0)) for a in lhs_list]
        w_spec = _resident((kk, n), lambda i, k: (0, 0))
        scratch = []
    else:
        assert n_lhs == 1
        lhs_specs = [pl.BlockSpec((bm, bk), lambda i, k: (i, k))]
        w_spec = pl.BlockSpec((bk, n), lambda i, k: (k, 0))
        scratch = [pltpu.VMEM((bm, n), F32)]
    return pl.pallas_call(
        functools.partial(_mm_norm_res_kernel, n_lhs=n_lhs, nk=nk),
        grid=(m // bm, nk),
        in_specs=lhs_specs + [
            w_spec,
            _resident((1, n), lambda i, k: (0, 0)),
            pl.BlockSpec((bm, n), lambda i, k: (i, 0)),
        ],
        out_specs=pl.BlockSpec((bm, n), lambda i, k: (i, 0)),
        out_shape=jax.ShapeDtypeStruct((m, n), F32),
        scratch_shapes=scratch,
        compiler_params=_params("parallel", "arbitrary"),
        name="mm_norm_res",
    )(*lhs_list, w, g.reshape(1, n), res)


def _gla_kernel(q_ref, k_ref, v_ref, z_ref, r_ref, w2_ref, bg_ref, gh_ref, tri_ref,
                o_ref, st_ref, *, dk):
    tl = q_ref.shape[0]

    @pl.when(pl.program_id(2) == 0)
    def _():
        st_ref[...] = jnp.zeros_like(st_ref)

    pre = jnp.dot(z_ref[...].astype(BF16), w2_ref[...], preferred_element_type=F32) + bg_ref[...]
    logg = (jnp.minimum(pre, 0.0) - jnp.log1p(jnp.exp(-jnp.abs(pre)))) * (1.0 / GLA_TAU)
    hi = logg.astype(BF16)
    lo = (logg - hi.astype(F32)).astype(BF16)
    hl = jnp.concatenate([hi, lo], axis=1)
    incl = tri_ref[0]
    cs = jnp.dot(incl, hl, preferred_element_type=F32)
    sf = jnp.dot(tri_ref[1], hl, preferred_element_type=F32)
    bcum = cs[:, :dk] + cs[:, dk:]
    suf = sf[:, :dk] + sf[:, dk:]

    qf = q_ref[...].astype(F32) * dk ** -0.5
    kf = k_ref[...].astype(F32)
    q_dec = (qf * jnp.exp(bcum)).astype(BF16)
    k_intra = (kf * jnp.exp(-bcum)).astype(BF16)
    k_state = (kf * jnp.exp(suf)).astype(BF16)
    v = v_ref[...]

    a = lax.dot_general(q_dec, k_intra, NT_DIMS, preferred_element_type=F32)
    a = jnp.where(incl > 0, a, 0.0).astype(BF16)
    o_intra = jnp.dot(a, v, preferred_element_type=F32)

    st = st_ref[...]
    outs = []
    for c in range(tl // GLA_CHUNK):
        rows = slice(c * GLA_CHUNK, (c + 1) * GLA_CHUNK)
        o_inter = lax.dot_general(q_dec[rows], st.astype(BF16), NT_DIMS, preferred_element_type=F32)
        outs.append(o_intra[rows] + o_inter)
        last = (c + 1) * GLA_CHUNK - 1
        decay = jnp.exp(bcum[last:last + 1, :])
        upd = lax.dot_general(v[rows], k_state[rows], TN_DIMS, preferred_element_type=F32)
        st = decay * st + upd
    st_ref[...] = st
    o = jnp.concatenate(outs, axis=0)
    o = _rms(o, gh_ref[...])
    r = r_ref[...]
    o_ref[...] = (o * (r * (1.0 / (1.0 + jnp.exp(-r))))).astype(o_ref.dtype)


def gla(pa, pb, w2p, bgate, ghead, tri, tl):
    b, l, _ = pa.shape
    dk = LANES
    dv = 2 * dk
    hq = GLA_HEADS
    vb = 2 * GLA_HEADS * dk // dv
    zb = GLA_HEADS * dv // LANES
    return pl.pallas_call(
        functools.partial(_gla_kernel, dk=dk),
        grid=(b, GLA_HEADS, l // tl),
        in_specs=[
            pl.BlockSpec((None, tl, dk), lambda bi, h, t: (bi, t, h)),
            pl.BlockSpec((None, tl, dk), lambda bi, h, t: (bi, t, hq + h)),
            pl.BlockSpec((None, tl, dv), lambda bi, h, t: (bi, t, vb + h)),
            pl.BlockSpec((None, tl, LANES), lambda bi, h, t: (bi, t, zb)),
            pl.BlockSpec((None, tl, dv), lambda bi, h, t: (bi, t, h)),
            pl.BlockSpec((LANES, dk), lambda bi, h, t: (0, h)),
            pl.BlockSpec((1, dk), lambda bi, h, t: (0, h)),
            _resident((1, dv), lambda bi, h, t: (0, 0)),
            _resident((2, tl, tl), lambda bi, h, t: (0, 0, 0)),
        ],
        out_specs=pl.BlockSpec((None, tl, dv), lambda bi, h, t: (bi, t, h)),
        out_shape=jax.ShapeDtypeStruct((b, l, GLA_HEADS * dv), BF16),
        scratch_shapes=[pltpu.VMEM((dv, dk), F32)],
        compiler_params=_params("parallel", "parallel", "arbitrary"),
        name="gla",
    )(pa, pa, pa, pb, pb, w2p, bgate, ghead, tri)


def _dsa_kernel(q_ref, iq_ref, iw_ref, k_ref, v_ref, ik_ref, corr_ref, o_ref,
                key_ref, madd_ref, wb_ref, t_ref, m_ref, l_ref, acc_ref,
                *, topk, rc):
    tq = q_ref.shape[0]
    tk = tq
    dh = LANES
    nlt = tk // LANES
    i = pl.program_id(1)
    nkb = i + 1
    scale = dh ** -0.5

    def lanes(x):
        return jnp.concatenate([x] * nlt, axis=1)

    w = iw_ref[...] * (IDX_HEADS ** -0.5 * IDX_DH ** -0.5)
    for h in range(IDX_HEADS):
        wb_ref[h] = jnp.broadcast_to(w[:, h:h + 1], (tq, LANES))
    row = lax.broadcasted_iota(jnp.int32, (tq, tk), 0) + i * tq
    col0 = lax.broadcasted_iota(jnp.int32, (tq, tk), 1)

    def score_block(kb, _):
        ik = ik_ref[pl.ds(pl.multiple_of(kb * tk, tk), tk), :]
        ik_even, ik_odd = ik[:, :LANES], ik[:, LANES:]
        acc = jnp.zeros((tq, tk), F32)
        for p in range(IDX_HEADS // 2):
            qp = iq_ref[:, p * LANES:(p + 1) * LANES]
            for par, ikx in ((0, ik_even), (1, ik_odd)):
                d = lax.dot_general(qp, ikx, NT_DIMS, preferred_element_type=F32)
                acc = acc + jnp.maximum(d, 0.0) * lanes(wb_ref[2 * p + par])
        sc = jnp.where(col0 + kb * tk <= row, acc, -jnp.inf)
        bits = lax.bitcast_convert_type(sc, jnp.int32)
        key_ref[kb] = bits ^ ((bits >> 31) & 0x7FFFFFFF)
        return 0

    lax.fori_loop(0, nkb, score_block, 0)

    def search_rows(c, _):
        rows = pl.ds(pl.multiple_of(c * rc, rc), rc)
        t_ref[rows, :] = jnp.full((rc, LANES), INT_MIN, jnp.int32)

        def search_bit(s, _):
            t_old = t_ref[rows, :]
            cand = t_old ^ lax.shift_left(jnp.int32(1), 31 - s)

            def count(kb, cnt):
                kt = key_ref[kb, rows, :]
                for lt in range(nlt):
                    cnt = cnt + jnp.where(kt[:, lt * LANES:(lt + 1) * LANES] >= cand, 1.0, 0.0)
                return cnt

            cnt = lax.fori_loop(0, nkb, count, jnp.zeros((rc, LANES), F32))
            total = jnp.sum(cnt, axis=-1, keepdims=True)
            t_ref[rows, :] = jnp.where(total >= topk, cand, t_old)
            return 0

        lax.fori_loop(0, 32, search_bit, 0)
        return 0

    lax.fori_loop(0, tq // rc, search_rows, 0)

    thr = lanes(t_ref[...])

    def mask_block(kb, _):
        vis = col0 + kb * tk <= row
        madd_ref[kb] = jnp.where(vis, jnp.where(key_ref[kb] >= thr, 0.0, NEG_BIG), NEG_BIG)
        return 0

    lax.fori_loop(0, nkb, mask_block, 0)

    for h in range(DSA_HEADS):
        hs = slice(h * dh, (h + 1) * dh)
        q_h = q_ref[:, hs]
        m_ref[...] = jnp.full((tq, LANES), NEG_BIG, F32)
        l_ref[...] = jnp.zeros((tq, LANES), F32)
        acc_ref[...] = jnp.zeros((tq, dh), F32)

        def attend(kb, corr):
            keys = pl.ds(pl.multiple_of(kb * tk, tk), tk)
            s = lax.dot_general(q_h, k_ref[keys, hs], NT_DIMS, preferred_element_type=F32) * scale
            if corr is not None:
                s = s + corr
            s = s + madd_ref[kb]
            m_prev = m_ref[...]
            m_next = jnp.maximum(m_prev, jnp.max(s, axis=-1, keepdims=True))
            alpha = jnp.exp(m_prev - m_next)
            p = jnp.exp(s - lanes(m_next))
            l_ref[...] = alpha * l_ref[...] + jnp.sum(p, axis=-1, keepdims=True)
            acc_ref[...] = alpha * acc_ref[...] + jnp.dot(
                p.astype(BF16), v_ref[keys, hs], preferred_element_type=F32)
            m_ref[...] = m_next

        def far(kb, _):
            attend(kb, None)
            return 0

        lax.fori_loop(0, i - 1, far, 0)

        @pl.when(i >= 1)
        def _():
            attend(i - 1, corr_ref[h, 1])

        attend(i, corr_ref[h, 0])
        o_ref[:, hs] = (acc_ref[...] / l_ref[...]).astype(o_ref.dtype)


def dsa(pa, pb, corr, tq):
    b, l, _ = pa.shape
    dh = LANES
    hd = DSA_HEADS * dh
    qb = (2 * GLA_HEADS * LANES + GLA_HEADS * 2 * LANES) // hd
    ikb = (qb + 4) * hd // (2 * LANES)
    iwb = GLA_HEADS * 2 * LANES // LANES + 1
    topk = min(TOPK_MAX, l // 4)
    assert l % tq == 0 and tq >= topk and tq >= REL_MAX_DIST
    nkb = l // tq
    return pl.pallas_call(
        functools.partial(_dsa_kernel, topk=topk, rc=64),
        grid=(b, l // tq),
        in_specs=[
            pl.BlockSpec((None, tq, hd), lambda bi, i: (bi, i, qb)),
            pl.BlockSpec((None, tq, IDX_HEADS * IDX_DH), lambda bi, i: (bi, i, qb + 3)),
            pl.BlockSpec((None, tq, LANES), lambda bi, i: (bi, i, iwb)),
            _resident((None, l, hd), lambda bi, i: (bi, 0, qb + 1)),
            _resident((None, l, hd), lambda bi, i: (bi, 0, qb + 2)),
            _resident((None, l, 2 * LANES), lambda bi, i: (bi, 0, ikb)),
            _resident((DSA_HEADS, 2, tq, tq), lambda bi, i: (0, 0, 0, 0)),
        ],
        out_specs=pl.BlockSpec((None, tq, hd), lambda bi, i: (bi, i, 0)),
        out_shape=jax.ShapeDtypeStruct((b, l, hd), BF16),
        scratch_shapes=[
            pltpu.VMEM((nkb, tq, tq), jnp.int32),
            pltpu.VMEM((nkb, tq, tq), F32),
            pltpu.VMEM((IDX_HEADS, tq, LANES), F32),
            pltpu.VMEM((tq, LANES), jnp.int32),
            pltpu.VMEM((tq, LANES), F32),
            pltpu.VMEM((tq, LANES), F32),
            pltpu.VMEM((tq, dh), F32),
        ],
        compiler_params=_params("parallel", "arbitrary"),
        name="dsa",
    )(pa, pa, pb, pa, pa, pa, corr)


def _xattn_kernel(q_ref, kv_ref, o_ref):
    d = q_ref.shape[1]
    dh = d // XA_HEADS
    scale = dh ** -0.5
    for h in range(XA_HEADS):
        hs = slice(h * dh, (h + 1) * dh)
        s = lax.dot_general(q_ref[:, hs], kv_ref[:, hs], NT_DIMS, preferred_element_type=F32) * scale
        e = jnp.exp(s - jnp.max(s, axis=-1, keepdims=True))
        o = jnp.dot(e.astype(BF16), kv_ref[:, d + h * dh:d + (h + 1) * dh], preferred_element_type=F32)
        o_ref[:, hs] = (o / jnp.sum(e, axis=-1, keepdims=True)).astype(o_ref.dtype)


def xattn(q, kv, tq):
    b, l, d = q.shape
    nm = kv.shape[1]
    return pl.pallas_call(
        _xattn_kernel,
        grid=(b, l // tq),
        in_specs=[
            pl.BlockSpec((None, tq, d), lambda bi, i: (bi, i, 0)),
            pl.BlockSpec((None, nm, 2 * d), lambda bi, i: (bi, 0, 0)),
        ],
        out_specs=pl.BlockSpec((None, tq, d), lambda bi, i: (bi, i, 0)),
        out_shape=jax.ShapeDtypeStruct((b, l, d), BF16),
        compiler_params=_params("parallel", "parallel"),
        name="xattn",
    )(q, kv)


def _rel_bucket(dist):
    n = jnp.maximum(dist, 0)
    max_exact = REL_BUCKETS // 2
    nf = jnp.maximum(n, 1).astype(F32)
    large = max_exact + (jnp.log(nf / max_exact) / math.log(REL_MAX_DIST / max_exact)
                         * (REL_BUCKETS - max_exact)).astype(jnp.int32)
    large = jnp.minimum(large, REL_BUCKETS - 1)
    return jnp.where(n < max_exact, n, large)


def _bias_correction_tiles(rel_table, tq):
    t = jnp.arange(tq)
    d0 = t[:, None] - t[None, :]
    far = rel_table[_rel_bucket(jnp.int32(2 * tq))]
    tiles = jnp.stack([rel_table[_rel_bucket(d0)], rel_table[_rel_bucket(d0 + tq)]], axis=0)
    return jnp.transpose(tiles - far, (3, 0, 1, 2)).astype(F32)


def _chunk_sum_matrices(tl):
    r = jnp.arange(tl)
    same = (r[:, None] // GLA_CHUNK) == (r[None, :] // GLA_CHUNK)
    incl = same & (r[None, :] <= r[:, None])
    after = same & (r[None, :] > r[:, None])
    return jnp.stack([incl, after]).astype(BF16)


def _split_in_proj(w):
    d = w.shape[0]
    half = d // 2
    sizes = [GLA_HEADS * LANES, GLA_HEADS * LANES, half, GLA_RANK, half,
             half, half, half, IDX_HEADS * IDX_DH, IDX_DH, IDX_HEADS]
    offs = [0]
    for s in sizes:
        offs.append(offs[-1] + s)
    assert offs[-1] == w.shape[1]
    seg = lambda n: w[:, offs[n]:offs[n + 1]]
    zero = lambda n: jnp.zeros((d, n), w.dtype)
    ik = seg(9)
    wa = jnp.concatenate([seg(0), seg(1), seg(2), seg(5), seg(6), seg(7), seg(8),
                          ik, zero(2 * IDX_DH), ik], axis=1)
    wb = jnp.concatenate([seg(4), seg(3), zero(LANES - GLA_RANK),
                          seg(10), zero(LANES - IDX_HEADS)], axis=1)
    return wa.astype(BF16), wb.astype(BF16)


def kernel(x, mem, norm_gains, w_in, w_gate_up, b_gate, gla_head_gain, rel_table,
           w_out, w_xq, w_xkv, w_xo, w_ffn_in, w_ffn_out):
    b, l, d = x.shape
    nm = mem.shape[1]
    depth = w_in.shape[0]
    d_ff = w_ffn_out.shape[1]
    m = b * l
    assert d // 2 // GLA_HEADS == 2 * LANES and d // 2 // DSA_HEADS == LANES

    tq_dsa = 256
    tl_gla = 512
    corr = _bias_correction_tiles(rel_table, tq_dsa)
    tri = _chunk_sum_matrices(tl_gla)
    w2p = jnp.pad(w_gate_up, ((0, 0), (0, LANES - GLA_RANK), (0, 0))).astype(BF16)

    xf = x.reshape(m, d)
    memf = mem.reshape(b * nm, d)
    for li in range(depth):
        g = norm_gains[li]
        wa, wb = _split_in_proj(w_in[li])
        pa = norm_matmul(xf, g[0], wa, BF16, bm=512, bn=1280).reshape(b, l, -1)
        pb = norm_matmul(xf, g[0], wb, F32, bm=512, bn=1280).reshape(b, l, -1)
        o_gla = gla(pa, pb, w2p[li], b_gate[li].reshape(1, -1),
                    gla_head_gain[li].reshape(1, -1), tri, tl_gla)
        o_dsa = dsa(pa, pb, corr, tq_dsa)
        xf = mm_norm_res([o_gla.reshape(m, -1), o_dsa.reshape(m, -1)],
                         w_out[li].astype(BF16), g[1], xf, bm=512)
        q = norm_matmul(xf, g[2], w_xq[li].astype(BF16), BF16, bm=512, bn=1024)
        kv = norm_matmul(memf, g[6], w_xkv[li].astype(BF16), BF16, bm=512, bn=1024)
        o_x = xattn(q.reshape(b, l, d), kv.reshape(b, nm, 2 * d), tq=512)
        xf = mm_norm_res([o_x.reshape(m, d)], w_xo[li].astype(BF16), g[3], xf, bm=512)
        act = ffn_in(xf, g[4], w_ffn_in[li].astype(BF16), d_ff, bm=512, bn=512)
        xf = mm_norm_res([act], w_ffn_out[li].astype(BF16), g[5], xf, bm=512, nk=4)
    return xf.reshape(b, l, d)
```

```python
import functools
import math

import jax
import jax.numpy as jnp
from jax import lax
from jax.experimental import pallas as pl
from jax.experimental.pallas import tpu as pltpu

F32 = jnp.float32
BF16 = jnp.bfloat16

EPS = 1e-6
GLA_HEADS = 4
GLA_RANK = 16
GLA_TAU = 16.0
GLA_CHUNK = 64
DSA_HEADS = 8
IDX_HEADS = 16
IDX_DH = 64
TOPK_MAX = 256
REL_BUCKETS = 32
REL_MAX_DIST = 128
XA_HEADS = 4

LANES = 128
VMEM_LIMIT_BYTES = 56 * 1024 * 1024

NEG_BIG = -1e30
INT_MIN = -(2 ** 31)

NT_DIMS = (((1,), (1,)), ((), ()))
TN_DIMS = (((0,), (0,)), ((), ()))


def _params(*sem):
    return pltpu.CompilerParams(dimension_semantics=sem, vmem_limit_bytes=VMEM_LIMIT_BYTES)


def _resident(shape, index_map):
    return pl.BlockSpec(shape, index_map, pipeline_mode=pl.Buffered(1))


def _rms(x, g):
    return x * lax.rsqrt(jnp.mean(x * x, axis=-1, keepdims=True) + EPS) * g


def _rms_to(x_ref, g_ref, h_ref, chunk=128):
    g = g_ref[...]

    def body(c, _):
        rows = pl.ds(pl.multiple_of(c * chunk, chunk), chunk)
        h_ref[rows, :] = _rms(x_ref[rows, :], g).astype(h_ref.dtype)
        return 0

    lax.fori_loop(0, x_ref.shape[0] // chunk, body, 0)


def _norm_matmul_kernel(x_ref, g_ref, w_ref, o_ref, h_ref):
    @pl.when(pl.program_id(1) == 0)
    def _():
        _rms_to(x_ref, g_ref, h_ref)

    o_ref[...] = jnp.dot(h_ref[...], w_ref[...], preferred_element_type=F32).astype(o_ref.dtype)


def norm_matmul(x, g, w, out_dtype, bm, bn):
    m, k = x.shape
    n = w.shape[1]
    assert m % bm == 0 and n % bn == 0
    return pl.pallas_call(
        _norm_matmul_kernel,
        grid=(m // bm, n // bn),
        in_specs=[
            pl.BlockSpec((bm, k), lambda i, j: (i, 0)),
            _resident((1, k), lambda i, j: (0, 0)),
            pl.BlockSpec((k, bn), lambda i, j: (0, j)),
        ],
        out_specs=pl.BlockSpec((bm, bn), lambda i, j: (i, j)),
        out_shape=jax.ShapeDtypeStruct((m, n), out_dtype),
        scratch_shapes=[pltpu.VMEM((bm, k), BF16)],
        compiler_params=_params("parallel", "arbitrary"),
        name="norm_matmul",
    )(x, g.reshape(1, k), w)


def _ffn_in_kernel(x_ref, g_ref, wg_ref, wu_ref, o_ref, h_ref):
    @pl.when(pl.program_id(1) == 0)
    def _():
        _rms_to(x_ref, g_ref, h_ref)

    h = h_ref[...]
    a = jnp.dot(h, wg_ref[...], preferred_element_type=F32)
    b = jnp.dot(h, wu_ref[...], preferred_element_type=F32)
    o_ref[...] = (a * (1.0 / (1.0 + jnp.exp(-a))) * b).astype(o_ref.dtype)


def ffn_in(x, g, w, d_ff, bm, bn):
    m, k = x.shape
    assert m % bm == 0 and d_ff % bn == 0
    nj = d_ff // bn
    return pl.pallas_call(
        _ffn_in_kernel,
        grid=(m // bm, nj),
        in_specs=[
            pl.BlockSpec((bm, k), lambda i, j: (i, 0)),
            _resident((1, k), lambda i, j: (0, 0)),
            pl.BlockSpec((k, bn), lambda i, j: (0, j)),
            pl.BlockSpec((k, bn), lambda i, j: (0, j + nj)),
        ],
        out_specs=pl.BlockSpec((bm, bn), lambda i, j: (i, j)),
        out_shape=jax.ShapeDtypeStruct((m, d_ff), BF16),
        scratch_shapes=[pltpu.VMEM((bm, k), BF16)],
        compiler_params=_params("parallel", "arbitrary"),
        name="ffn_in",
    )(x, g.reshape(1, k), w, w)


def _mm_norm_res_kernel(*refs, n_lhs, nk):
    lhs = refs[:n_lhs]
    w_ref, g_ref, res_ref, o_ref = refs[n_lhs:n_lhs + 4]
    if nk == 1:
        y = None
        off = 0
        for a in lhs:
            wd = a.shape[1]
            part = jnp.dot(a[...], w_ref[off:off + wd, :], preferred_element_type=F32)
            y = part if y is None else y + part
            off += wd
        o_ref[...] = res_ref[...] + _rms(y, g_ref[...])
    else:
        acc_ref = refs[n_lhs + 4]
        k = pl.program_id(1)
        part = jnp.dot(lhs[0][...], w_ref[...], preferred_element_type=F32)

        @pl.when(k == 0)
        def _():
            acc_ref[...] = part

        @pl.when(k > 0)
        def _():
            acc_ref[...] += part

        @pl.when(k == nk - 1)
        def _():
            o_ref[...] = res_ref[...] + _rms(acc_ref[...], g_ref[...])


def mm_norm_res(lhs_list, w, g, res, bm, nk=1):
    m, n = res.shape
    kk = w.shape[0]
    assert m % bm == 0 and kk % nk == 0
    bk = kk // nk
    n_lhs = len(lhs_list)
    if nk == 1:
        lhs_specs = [pl.BlockSpec((bm, a.shape[1]), lambda i, k: (i, 0)) for a in lhs_list]
        w_spec = _resident((kk, n), lambda i, k: (0, 0))
        scratch = []
    else:
        assert n_lhs == 1
        lhs_specs = [pl.BlockSpec((bm, bk), lambda i, k: (i, k))]
        w_spec = pl.BlockSpec((bk, n), lambda i, k: (k, 0))
        scratch = [pltpu.VMEM((bm, n), F32)]
    return pl.pallas_call(
        functools.partial(_mm_norm_res_kernel, n_lhs=n_lhs, nk=nk),
        grid=(m // bm, nk),
        in_specs=lhs_specs + [
            w_spec,
            _resident((1, n), lambda i, k: (0, 0)),
            pl.BlockSpec((bm, n), lambda i, k: (i, 0)),
        ],
        out_specs=pl.BlockSpec((bm, n), lambda i, k: (i, 0)),
        out_shape=jax.ShapeDtypeStruct((m, n), F32),
        scratch_shapes=scratch,
        compiler_params=_params("parallel", "arbitrary"),
        name="mm_norm_res",
    )(*lhs_list, w, g.reshape(1, n), res)


def _gla_kernel(q_ref, k_ref, v_ref, z_ref, r_ref, w2_ref, bg_ref, gh_ref, tri_ref,
                o_ref, st_ref, *, dk):
    tl = q_ref.shape[0]

    @pl.when(pl.program_id(2) == 0)
    def _():
        st_ref[...] = jnp.zeros_like(st_ref)

    pre = jnp.dot(z_ref[...].astype(BF16), w2_ref[...], preferred_element_type=F32) + bg_ref[...]
    logg = (jnp.minimum(pre, 0.0) - jnp.log1p(jnp.exp(-jnp.abs(pre)))) * (1.0 / GLA_TAU)
    hi = logg.astype(BF16)
    lo = (logg - hi.astype(F32)).astype(BF16)
    hl = jnp.concatenate([hi, lo], axis=1)
    incl = tri_ref[0]
    cs = jnp.dot(incl, hl, preferred_element_type=F32)
    sf = jnp.dot(tri_ref[1], hl, preferred_element_type=F32)
    bcum = cs[:, :dk] + cs[:, dk:]
    suf = sf[:, :dk] + sf[:, dk:]

    qf = q_ref[...].astype(F32) * dk ** -0.5
    kf = k_ref[...].astype(F32)
    q_dec = (qf * jnp.exp(bcum)).astype(BF16)
    k_intra = (kf * jnp.exp(-bcum)).astype(BF16)
    k_state = (kf * jnp.exp(suf)).astype(BF16)
    v = v_ref[...]

    a = lax.dot_general(q_dec, k_intra, NT_DIMS, preferred_element_type=F32)
    a = jnp.where(incl > 0, a, 0.0).astype(BF16)
    o_intra = jnp.dot(a, v, preferred_element_type=F32)

    st = st_ref[...]
    outs = []
    for c in range(tl // GLA_CHUNK):
        rows = slice(c * GLA_CHUNK, (c + 1) * GLA_CHUNK)
        o_inter = lax.dot_general(q_dec[rows], st.astype(BF16), NT_DIMS, preferred_element_type=F32)
        outs.append(o_intra[rows] + o_inter)
        last = (c + 1) * GLA_CHUNK - 1
        decay = jnp.exp(bcum[last:last + 1, :])
        upd = lax.dot_general(v[rows], k_state[rows], TN_DIMS, preferred_element_type=F32)
        st = decay * st + upd
    st_ref[...] = st
    o = jnp.concatenate(outs, axis=0)
    o = _rms(o, gh_ref[...])
    r = r_ref[...]
    o_ref[...] = (o * (r * (1.0 / (1.0 + jnp.exp(-r))))).astype(o_ref.dtype)


def gla(pa, pb, w2p, bgate, ghead, tri, tl):
    b, l, _ = pa.shape
    dk = LANES
    dv = 2 * dk
    hq = GLA_HEADS
    vb = 2 * GLA_HEADS * dk // dv
    zb = GLA_HEADS * dv // LANES
    return pl.pallas_call(
        functools.partial(_gla_kernel, dk=dk),
        grid=(b, GLA_HEADS, l // tl),
        in_specs=[
            pl.BlockSpec((None, tl, dk), lambda bi, h, t: (bi, t, h)),
            pl.BlockSpec((None, tl, dk), lambda bi, h, t: (bi, t, hq + h)),
            pl.BlockSpec((None, tl, dv), lambda bi, h, t: (bi, t, vb + h)),
            pl.BlockSpec((None, tl, LANES), lambda bi, h, t: (bi, t, zb)),
            pl.BlockSpec((None, tl, dv), lambda bi, h, t: (bi, t, h)),
            pl.BlockSpec((LANES, dk), lambda bi, h, t: (0, h)),
            pl.BlockSpec((1, dk), lambda bi, h, t: (0, h)),
            _resident((1, dv), lambda bi, h, t: (0, 0)),
            _resident((2, tl, tl), lambda bi, h, t: (0, 0, 0)),
        ],
        out_specs=pl.BlockSpec((None, tl, dv), lambda bi, h, t: (bi, t, h)),
        out_shape=jax.ShapeDtypeStruct((b, l, GLA_HEADS * dv), BF16),
        scratch_shapes=[pltpu.VMEM((dv, dk), F32)],
        compiler_params=_params("parallel", "parallel", "arbitrary"),
        name="gla",
    )(pa, pa, pa, pb, pb, w2p, bgate, ghead, tri)


def _dsa_kernel(q_ref, iq_ref, iw_ref, k_ref, v_ref, ik_ref, corr_ref, o_ref,
                key_ref, madd_ref, wt_ref, t_ref, m_ref, l_ref, acc_ref, *, topk):
    tq = q_ref.shape[0]
    tk = tq
    dh = LANES
    nlt = tk // LANES
    sub = 8
    i = pl.program_id(1)
    nkb = i + 1
    scale = dh ** -0.5

    def lanes(x):
        return jnp.concatenate([x] * nlt, axis=1)

    wt_ref[...] = (iw_ref[...] * (IDX_HEADS ** -0.5 * IDX_DH ** -0.5)).T
    qpos = lax.broadcasted_iota(jnp.int32, (tk, tq), 1) + i * tq
    kpos0 = lax.broadcasted_iota(jnp.int32, (tk, tq), 0)

    def score_block(kb, _):
        ik = ik_ref[pl.ds(pl.multiple_of(kb * tk, tk), tk), :]
        ik_even, ik_odd = ik[:, :LANES], ik[:, LANES:]
        acc = jnp.zeros((tk, tq), F32)
        for p in range(IDX_HEADS // 2):
            qp = iq_ref[:, p * LANES:(p + 1) * LANES]
            for par, ikx in ((0, ik_even), (1, ik_odd)):
                h = 2 * p + par
                d = lax.dot_general(ikx, qp, NT_DIMS, preferred_element_type=F32)
                acc = acc + jnp.maximum(d, 0.0) * wt_ref[h:h + 1, :]
        sc = jnp.where(kpos0 + kb * tk <= qpos, acc, -jnp.inf)
        bits = lax.bitcast_convert_type(sc, jnp.int32)
        key_ref[kb] = bits ^ ((bits >> 31) & 0x7FFFFFFF)
        return 0

    lax.fori_loop(0, nkb, score_block, 0)

    t_ref[...] = jnp.full((sub, tq), INT_MIN, jnp.int32)
    n_acc = 4

    def search_bit(s, _):
        t_old = t_ref[...]
        cand = t_old ^ lax.shift_left(jnp.int32(1), 31 - s)

        def count(kb, cnts):
            cnts = list(cnts)
            for r in range(tk // sub):
                hit = key_ref[kb, r * sub:(r + 1) * sub, :] >= cand
                cnts[r % n_acc] = cnts[r % n_acc] + jnp.where(hit, 1, 0)
            return tuple(cnts)

        zero = jnp.zeros((sub, tq), jnp.int32)
        cnts = lax.fori_loop(0, nkb, count, (zero,) * n_acc)
        cnt = (cnts[0] + cnts[1]) + (cnts[2] + cnts[3])
        total = jnp.sum(cnt, axis=0, keepdims=True)
        t_ref[...] = jnp.where(total >= topk, cand, t_old)
        return 0

    lax.fori_loop(0, 32, search_bit, 0)

    thr = t_ref[0:1, :]

    def mask_block(kb, _):
        vis = kpos0 + kb * tk <= qpos
        sel = jnp.where(vis, jnp.where(key_ref[kb] >= thr, 0.0, NEG_BIG), NEG_BIG)
        madd_ref[kb] = sel.T
        return 0

    lax.fori_loop(0, nkb, mask_block, 0)

    m_ref[...] = jnp.full(m_ref.shape, NEG_BIG, F32)
    l_ref[...] = jnp.zeros(l_ref.shape, F32)
    acc_ref[...] = jnp.zeros(acc_ref.shape, F32)

    def attend(kb, near):
        keys = pl.ds(pl.multiple_of(kb * tk, tk), tk)
        madd = madd_ref[kb]
        for h in range(DSA_HEADS):
            hs = slice(h * dh, (h + 1) * dh)
            s = lax.dot_general(q_ref[:, hs], k_ref[keys, hs], NT_DIMS,
                                preferred_element_type=F32) * scale
            if near is not None:
                s = s + corr_ref[h, near]
            s = s + madd
            m_prev = m_ref[h]
            m_next = jnp.maximum(m_prev, jnp.max(s, axis=-1, keepdims=True))
            alpha = jnp.exp(m_prev - m_next)
            p = jnp.exp(s - lanes(m_next))
            l_ref[h] = alpha * l_ref[h] + jnp.sum(p, axis=-1, keepdims=True)
            acc_ref[h] = alpha * acc_ref[h] + jnp.dot(
                p.astype(BF16), v_ref[keys, hs], preferred_element_type=F32)
            m_ref[h] = m_next

    def far(kb, _):
        attend(kb, None)
        return 0

    lax.fori_loop(0, i - 1, far, 0)

    @pl.when(i >= 1)
    def _():
        attend(i - 1, 1)

    attend(i, 0)
    for h in range(DSA_HEADS):
        o_ref[:, h * dh:(h + 1) * dh] = (acc_ref[h] / l_ref[h]).astype(o_ref.dtype)


def dsa(pa, pb, corr, tq):
    b, l, _ = pa.shape
    dh = LANES
    hd = DSA_HEADS * dh
    qb = (2 * GLA_HEADS * LANES + GLA_HEADS * 2 * LANES) // hd
    ikb = (qb + 4) * hd // (2 * LANES)
    iwb = GLA_HEADS * 2 * LANES // LANES + 1
    topk = min(TOPK_MAX, l // 4)
    assert l % tq == 0 and tq >= topk and tq >= REL_MAX_DIST
    nkb = l // tq
    return pl.pallas_call(
        functools.partial(_dsa_kernel, topk=topk),
        grid=(b, l // tq),
        in_specs=[
            pl.BlockSpec((None, tq, hd), lambda bi, i: (bi, i, qb)),
            pl.BlockSpec((None, tq, IDX_HEADS * IDX_DH), lambda bi, i: (bi, i, qb + 3)),
            pl.BlockSpec((None, tq, LANES), lambda bi, i: (bi, i, iwb)),
            _resident((None, l, hd), lambda bi, i: (bi, 0, qb + 1)),
            _resident((None, l, hd), lambda bi, i: (bi, 0, qb + 2)),
            _resident((None, l, 2 * LANES), lambda bi, i: (bi, 0, ikb)),
            _resident((DSA_HEADS, 2, tq, tq), lambda bi, i: (0, 0, 0, 0)),
        ],
        out_specs=pl.BlockSpec((None, tq, hd), lambda bi, i: (bi, i, 0)),
        out_shape=jax.ShapeDtypeStruct((b, l, hd), BF16),
        scratch_shapes=[
            pltpu.VMEM((nkb, tq, tq), jnp.int32),
            pltpu.VMEM((nkb, tq, tq), F32),
            pltpu.VMEM((LANES, tq), F32),
            pltpu.VMEM((8, tq), jnp.int32),
            pltpu.VMEM((DSA_HEADS, tq, LANES), F32),
            pltpu.VMEM((DSA_HEADS, tq, LANES), F32),
            pltpu.VMEM((DSA_HEADS, tq, dh), F32),
        ],
        compiler_params=_params("parallel", "arbitrary"),
        name="dsa",
    )(pa, pa, pb, pa, pa, pa, corr)


def _xattn_kernel(q_ref, kv_ref, o_ref):
    d = q_ref.shape[1]
    dh = d // XA_HEADS
    scale = dh ** -0.5
    for h in range(XA_HEADS):
        hs = slice(h * dh, (h + 1) * dh)
        s = lax.dot_general(q_ref[:, hs], kv_ref[:, hs], NT_DIMS, preferred_element_type=F32) * scale
        e = jnp.exp(s - jnp.max(s, axis=-1, keepdims=True))
        o = jnp.dot(e.astype(BF16), kv_ref[:, d + h * dh:d + (h + 1) * dh], preferred_element_type=F32)
        o_ref[:, hs] = (o / jnp.sum(e, axis=-1, keepdims=True)).astype(o_ref.dtype)


def xattn(q, kv, tq):
    b, l, d = q.shape
    nm = kv.shape[1]
    return pl.pallas_call(
        _xattn_kernel,
        grid=(b, l // tq),
        in_specs=[
            pl.BlockSpec((None, tq, d), lambda bi, i: (bi, i, 0)),
            pl.BlockSpec((None, nm, 2 * d), lambda bi, i: (bi, 0, 0)),
        ],
        out_specs=pl.BlockSpec((None, tq, d), lambda bi, i: (bi, i, 0)),
        out_shape=jax.ShapeDtypeStruct((b, l, d), BF16),
        compiler_params=_params("parallel", "parallel"),
        name="xattn",
    )(q, kv)


def _rel_bucket(dist):
    n = jnp.maximum(dist, 0)
    max_exact = REL_BUCKETS // 2
    nf = jnp.maximum(n, 1).astype(F32)
    large = max_exact + (jnp.log(nf / max_exact) / math.log(REL_MAX_DIST / max_exact)
                         * (REL_BUCKETS - max_exact)).astype(jnp.int32)
    large = jnp.minimum(large, REL_BUCKETS - 1)
    return jnp.where(n < max_exact, n, large)


def _bias_correction_tiles(rel_table, tq):
    t = jnp.arange(tq)
    d0 = t[:, None] - t[None, :]
    buckets = jnp.stack([_rel_bucket(d0), _rel_bucket(d0 + tq)], axis=0)
    onehot = (buckets[..., None] == jnp.arange(REL_BUCKETS)).astype(F32)
    far = rel_table[_rel_bucket(jnp.int32(2 * tq))]
    return jnp.einsum("ntsk,kh->hnts", onehot, rel_table - far, precision=lax.Precision.HIGHEST)


def _chunk_sum_matrices(tl):
    r = jnp.arange(tl)
    same = (r[:, None] // GLA_CHUNK) == (r[None, :] // GLA_CHUNK)
    incl = same & (r[None, :] <= r[:, None])
    after = same & (r[None, :] > r[:, None])
    return jnp.stack([incl, after]).astype(BF16)


def _split_in_proj(w):
    d = w.shape[0]
    half = d // 2
    sizes = [GLA_HEADS * LANES, GLA_HEADS * LANES, half, GLA_RANK, half,
             half, half, half, IDX_HEADS * IDX_DH, IDX_DH, IDX_HEADS]
    offs = [0]
    for s in sizes:
        offs.append(offs[-1] + s)
    assert offs[-1] == w.shape[1]
    seg = lambda n: w[:, offs[n]:offs[n + 1]]
    zero = lambda n: jnp.zeros((d, n), w.dtype)
    ik = seg(9)
    wa = jnp.concatenate([seg(0), seg(1), seg(2), seg(5), seg(6), seg(7), seg(8),
                          ik, zero(2 * IDX_DH), ik], axis=1)
    wb = jnp.concatenate([seg(4), seg(3), zero(LANES - GLA_RANK),
                          seg(10), zero(LANES - IDX_HEADS)], axis=1)
    return wa.astype(BF16), wb.astype(BF16)


def kernel(x, mem, norm_gains, w_in, w_gate_up, b_gate, gla_head_gain, rel_table,
           w_out, w_xq, w_xkv, w_xo, w_ffn_in, w_ffn_out):
    b, l, d = x.shape
    nm = mem.shape[1]
    depth = w_in.shape[0]
    d_ff = w_ffn_out.shape[1]
    m = b * l
    assert d // 2 // GLA_HEADS == 2 * LANES and d // 2 // DSA_HEADS == LANES

    tq_dsa = 256
    tl_gla = 512
    corr = _bias_correction_tiles(rel_table, tq_dsa)
    tri = _chunk_sum_matrices(tl_gla)
    w2p = jnp.pad(w_gate_up, ((0, 0), (0, LANES - GLA_RANK), (0, 0))).astype(BF16)

    xf = x.reshape(m, d)
    memf = mem.reshape(b * nm, d)
    for li in range(depth):
        g = norm_gains[li]
        wa, wb = _split_in_proj(w_in[li])
        pa = norm_matmul(xf, g[0], wa, BF16, bm=512, bn=1280).reshape(b, l, -1)
        pb = norm_matmul(xf, g[0], wb, F32, bm=512, bn=1280).reshape(b, l, -1)
        o_gla = gla(pa, pb, w2p[li], b_gate[li].reshape(1, -1),
                    gla_head_gain[li].reshape(1, -1), tri, tl_gla)
        o_dsa = dsa(pa, pb, corr, tq_dsa)
        xf = mm_norm_res([o_gla.reshape(m, -1), o_dsa.reshape(m, -1)],
                         w_out[li].astype(BF16), g[1], xf, bm=512)
        q = norm_matmul(xf, g[2], w_xq[li].astype(BF16), BF16, bm=512, bn=1024)
        kv = norm_matmul(memf, g[6], w_xkv[li].astype(BF16), BF16, bm=512, bn=1024)
        o_x = xattn(q.reshape(b, l, d), kv.reshape(b, nm, 2 * d), tq=512)
        xf = mm_norm_res([o_x.reshape(m, d)], w_xo[li].astype(BF16), g[3], xf, bm=512)
        act = ffn_in(xf, g[4], w_ffn_in[li].astype(BF16), d_ff, bm=512, bn=512)
        xf = mm_norm_res([act], w_ffn_out[li].astype(BF16), g[5], xf, bm=512, nk=4)
    return xf.reshape(b, l, d)
```

```python
import functools
import math

import jax
import jax.numpy as jnp
from jax import lax
from jax.experimental import pallas as pl
from jax.experimental.pallas import tpu as pltpu

F32 = jnp.float32
BF16 = jnp.bfloat16

EPS = 1e-6
GLA_HEADS = 4
GLA_RANK = 16
GLA_TAU = 16.0
GLA_CHUNK = 64
DSA_HEADS = 8
IDX_HEADS = 16
IDX_DH = 64
TOPK_MAX = 256
REL_BUCKETS = 32
REL_MAX_DIST = 128
XA_HEADS = 4

LANES = 128
VMEM_LIMIT_BYTES = 56 * 1024 * 1024

NEG_BIG = -1e30
INT_MIN = -(2 ** 31)

NT_DIMS = (((1,), (1,)), ((), ()))
TN_DIMS = (((0,), (0,)), ((), ()))


def _params(*sem):
    return pltpu.CompilerParams(dimension_semantics=sem, vmem_limit_bytes=VMEM_LIMIT_BYTES)


def _resident(shape, index_map):
    return pl.BlockSpec(shape, index_map, pipeline_mode=pl.Buffered(1))


def _rms(x, g):
    return x * lax.rsqrt(jnp.mean(x * x, axis=-1, keepdims=True) + EPS) * g


def _rms_to(x_ref, g_ref, h_ref, chunk=128):
    g = g_ref[...]

    def body(c, _):
        rows = pl.ds(pl.multiple_of(c * chunk, chunk), chunk)
        h_ref[rows, :] = _rms(x_ref[rows, :], g).astype(h_ref.dtype)
        return 0

    lax.fori_loop(0, x_ref.shape[0] // chunk, body, 0)


def _norm_matmul_kernel(x_ref, g_ref, w_ref, o_ref, h_ref):
    @pl.when(pl.program_id(1) == 0)
    def _():
        _rms_to(x_ref, g_ref, h_ref)

    o_ref[...] = jnp.dot(h_ref[...], w_ref[...], preferred_element_type=F32).astype(o_ref.dtype)


def norm_matmul(x, g, w, out_dtype, bm, bn):
    m, k = x.shape
    n = w.shape[1]
    assert m % bm == 0 and n % bn == 0
    return pl.pallas_call(
        _norm_matmul_kernel,
        grid=(m // bm, n // bn),
        in_specs=[
            pl.BlockSpec((bm, k), lambda i, j: (i, 0)),
            _resident((1, k), lambda i, j: (0, 0)),
            pl.BlockSpec((k, bn), lambda i, j: (0, j)),
        ],
        out_specs=pl.BlockSpec((bm, bn), lambda i, j: (i, j)),
        out_shape=jax.ShapeDtypeStruct((m, n), out_dtype),
        scratch_shapes=[pltpu.VMEM((bm, k), BF16)],
        compiler_params=_params("parallel", "arbitrary"),
        name="norm_matmul",
    )(x, g.reshape(1, k), w)


def _ffn_in_kernel(x_ref, g_ref, wg_ref, wu_ref, o_ref, h_ref):
    @pl.when(pl.program_id(1) == 0)
    def _():
        _rms_to(x_ref, g_ref, h_ref)

    h = h_ref[...]
    a = jnp.dot(h, wg_ref[...], preferred_element_type=F32)
    b = jnp.dot(h, wu_ref[...], preferred_element_type=F32)
    o_ref[...] = (a * (1.0 / (1.0 + jnp.exp(-a))) * b).astype(o_ref.dtype)


def ffn_in(x, g, w, d_ff, bm, bn):
    m, k = x.shape
    assert m % bm == 0 and d_ff % bn == 0
    nj = d_ff // bn
    return pl.pallas_call(
        _ffn_in_kernel,
        grid=(m // bm, nj),
        in_specs=[
            pl.BlockSpec((bm, k), lambda i, j: (i, 0)),
            _resident((1, k), lambda i, j: (0, 0)),
            pl.BlockSpec((k, bn), lambda i, j: (0, j)),
            pl.BlockSpec((k, bn), lambda i, j: (0, j + nj)),
        ],
        out_specs=pl.BlockSpec((bm, bn), lambda i, j: (i, j)),
        out_shape=jax.ShapeDtypeStruct((m, d_ff), BF16),
        scratch_shapes=[pltpu.VMEM((bm, k), BF16)],
        compiler_params=_params("parallel", "arbitrary"),
        name="ffn_in",
    )(x, g.reshape(1, k), w, w)


def _mm_norm_res_kernel(*refs, n_lhs, nk):
    lhs = refs[:n_lhs]
    w_ref, g_ref, res_ref, o_ref = refs[n_lhs:n_lhs + 4]
    if nk == 1:
        y = None
        off = 0
        for a in lhs:
            wd = a.shape[1]
            part = jnp.dot(a[...], w_ref[off:off + wd, :], preferred_element_type=F32)
            y = part if y is None else y + part
            off += wd
        o_ref[...] = res_ref[...] + _rms(y, g_ref[...])
    else:
        acc_ref = refs[n_lhs + 4]
        k = pl.program_id(1)
        part = jnp.dot(lhs[0][...], w_ref[...], preferred_element_type=F32)

        @pl.when(k == 0)
        def _():
            acc_ref[...] = part

        @pl.when(k > 0)
        def _():
            acc_ref[...] += part

        @pl.when(k == nk - 1)
        def _():
            o_ref[...] = res_ref[...] + _rms(acc_ref[...], g_ref[...])


def mm_norm_res(lhs_list, w, g, res, bm, nk=1):
    m, n = res.shape
    kk = w.shape[0]
    assert m % bm == 0 and kk % nk == 0
    bk = kk // nk
    n_lhs = len(lhs_list)
    if nk == 1:
        lhs_specs = [pl.BlockSpec((bm, a.shape[1]), lambda i, k: (i, 0)) for a in lhs_list]
        w_spec = _resident((kk, n), lambda i, k: (0, 0))
        scratch = []
    else:
        assert n_lhs == 1
        lhs_specs = [pl.BlockSpec((bm, bk), lambda i, k: (i, k))]
        w_spec = pl.BlockSpec((bk, n), lambda i, k: (k, 0))
        scratch = [pltpu.VMEM((bm, n), F32)]
    return pl.pallas_call(
        functools.partial(_mm_norm_res_kernel, n_lhs=n_lhs, nk=nk),
        grid=(m // bm, nk),
        in_specs=lhs_specs + [
            w_spec,
            _resident((1, n), lambda i, k: (0, 0)),
            pl.BlockSpec((bm, n), lambda i, k: (i, 0)),
        ],
        out_specs=pl.BlockSpec((bm, n), lambda i, k: (i, 0)),
        out_shape=jax.ShapeDtypeStruct((m, n), F32),
        scratch_shapes=scratch,
        compiler_params=_params("parallel", "arbitrary"),
        name="mm_norm_res",
    )(*lhs_list, w, g.reshape(1, n), res)


def _gla_kernel(q_ref, k_ref, v_ref, z_ref, r_ref, w2_ref, bg_ref, gh_ref, tri_ref,
                o_ref, st_ref, *, dk):
    tl = q_ref.shape[0]

    @pl.when(pl.program_id(2) == 0)
    def _():
        st_ref[...] = jnp.zeros_like(st_ref)

    pre = jnp.dot(z_ref[...].astype(BF16), w2_ref[...], preferred_element_type=F32) + bg_ref[...]
    logg = (jnp.minimum(pre, 0.0) - jnp.log1p(jnp.exp(-jnp.abs(pre)))) * (1.0 / GLA_TAU)
    hi = logg.astype(BF16)
    lo = (logg - hi.astype(F32)).astype(BF16)
    hl = jnp.concatenate([hi, lo], axis=1)
    incl = tri_ref[0]
    cs = jnp.dot(incl, hl, preferred_element_type=F32)
    sf = jnp.dot(tri_ref[1], hl, preferred_element_type=F32)
    bcum = cs[:, :dk] + cs[:, dk:]
    suf = sf[:, :dk] + sf[:, dk:]

    qf = q_ref[...].astype(F32) * dk ** -0.5
    kf = k_ref[...].astype(F32)
    q_dec = (qf * jnp.exp(bcum)).astype(BF16)
    k_intra = (kf * jnp.exp(-bcum)).astype(BF16)
    k_state = (kf * jnp.exp(suf)).astype(BF16)
    v = v_ref[...]

    a = lax.dot_general(q_dec, k_intra, NT_DIMS, preferred_element_type=F32)
    a = jnp.where(incl > 0, a, 0.0).astype(BF16)
    o_intra = jnp.dot(a, v, preferred_element_type=F32)

    st = st_ref[...]
    outs = []
    for c in range(tl // GLA_CHUNK):
        rows = slice(c * GLA_CHUNK, (c + 1) * GLA_CHUNK)
        o_inter = lax.dot_general(q_dec[rows], st.astype(BF16), NT_DIMS, preferred_element_type=F32)
        outs.append(o_intra[rows] + o_inter)
        last = (c + 1) * GLA_CHUNK - 1
        decay = jnp.exp(bcum[last:last + 1, :])
        upd = lax.dot_general(v[rows], k_state[rows], TN_DIMS, preferred_element_type=F32)
        st = decay * st + upd
    st_ref[...] = st
    o = jnp.concatenate(outs, axis=0)
    o = _rms(o, gh_ref[...])
    r = r_ref[...]
    o_ref[...] = (o * (r * (1.0 / (1.0 + jnp.exp(-r))))).astype(o_ref.dtype)


def gla(pa, pb, w2p, bgate, ghead, tri, tl):
    b, l, _ = pa.shape
    dk = LANES
    dv = 2 * dk
    hq = GLA_HEADS
    vb = 2 * GLA_HEADS * dk // dv
    zb = GLA_HEADS * dv // LANES
    return pl.pallas_call(
        functools.partial(_gla_kernel, dk=dk),
        grid=(b, GLA_HEADS, l // tl),
        in_specs=[
            pl.BlockSpec((None, tl, dk), lambda bi, h, t: (bi, t, h)),
            pl.BlockSpec((None, tl, dk), lambda bi, h, t: (bi, t, hq + h)),
            pl.BlockSpec((None, tl, dv), lambda bi, h, t: (bi, t, vb + h)),
            pl.BlockSpec((None, tl, LANES), lambda bi, h, t: (bi, t, zb)),
            pl.BlockSpec((None, tl, dv), lambda bi, h, t: (bi, t, h)),
            pl.BlockSpec((LANES, dk), lambda bi, h, t: (0, h)),
            pl.BlockSpec((1, dk), lambda bi, h, t: (0, h)),
            _resident((1, dv), lambda bi, h, t: (0, 0)),
            _resident((2, tl, tl), lambda bi, h, t: (0, 0, 0)),
        ],
        out_specs=pl.BlockSpec((None, tl, dv), lambda bi, h, t: (bi, t, h)),
        out_shape=jax.ShapeDtypeStruct((b, l, GLA_HEADS * dv), BF16),
        scratch_shapes=[pltpu.VMEM((dv, dk), F32)],
        compiler_params=_params("parallel", "parallel", "arbitrary"),
        name="gla",
    )(pa, pa, pa, pb, pb, w2p, bgate, ghead, tri)


def _dsa_kernel(q_ref, iq_ref, iw_ref, k_ref, v_ref, ik_ref, corr_ref, o_ref,
                key_ref, madd_ref, wt_ref, t_ref, m_ref, l_ref, acc_ref, vt_ref, s_ref,
                *, topk):
    tq = q_ref.shape[0]
    tk = tq
    dh = LANES
    sub = 8
    i = pl.program_id(1)
    nkb = i + 1
    scale = dh ** -0.5

    wt_ref[...] = (iw_ref[...] * (IDX_HEADS ** -0.5 * IDX_DH ** -0.5)).T
    qpos = lax.broadcasted_iota(jnp.int32, (tk, tq), 1) + i * tq
    kpos0 = lax.broadcasted_iota(jnp.int32, (tk, tq), 0)

    def score_block(kb, _):
        ik = ik_ref[pl.ds(pl.multiple_of(kb * tk, tk), tk), :]
        ik_even, ik_odd = ik[:, :LANES], ik[:, LANES:]
        acc = jnp.zeros((tk, tq), F32)
        for p in range(IDX_HEADS // 2):
            qp = iq_ref[:, p * LANES:(p + 1) * LANES]
            for par, ikx in ((0, ik_even), (1, ik_odd)):
                h = 2 * p + par
                d = lax.dot_general(ikx, qp, NT_DIMS, preferred_element_type=F32)
                acc = acc + jnp.maximum(d, 0.0) * wt_ref[h:h + 1, :]
        sc = jnp.where(kpos0 + kb * tk <= qpos, acc, -jnp.inf)
        bits = lax.bitcast_convert_type(sc, jnp.int32)
        key_ref[kb] = bits ^ ((bits >> 31) & 0x7FFFFFFF)
        return 0

    lax.fori_loop(0, nkb, score_block, 0)

    t_ref[...] = jnp.full((sub, tq), INT_MIN, jnp.int32)
    n_acc = 4

    def search_bit(s, _):
        t_old = t_ref[...]
        cand = t_old ^ lax.shift_left(jnp.int32(1), 31 - s)

        def count(kb, cnts):
            cnts = list(cnts)
            for r in range(tk // sub):
                hit = key_ref[kb, r * sub:(r + 1) * sub, :] >= cand
                cnts[r % n_acc] = cnts[r % n_acc] + jnp.where(hit, 1, 0)
            return tuple(cnts)

        zero = jnp.zeros((sub, tq), jnp.int32)
        cnts = lax.fori_loop(0, nkb, count, (zero,) * n_acc)
        cnt = (cnts[0] + cnts[1]) + (cnts[2] + cnts[3])
        total = jnp.sum(cnt, axis=0, keepdims=True)
        t_ref[...] = jnp.where(total >= topk, cand, t_old)
        return 0

    lax.fori_loop(0, 32, search_bit, 0)

    thr = t_ref[0:1, :]

    def mask_block(kb, _):
        vis = kpos0 + kb * tk <= qpos
        madd_ref[kb] = jnp.where(vis, jnp.where(key_ref[kb] >= thr, 0.0, NEG_BIG), NEG_BIG)
        return 0

    lax.fori_loop(0, nkb, mask_block, 0)

    @pl.when(i == 0)
    def _():
        def transpose_values(kb, _):
            vt_ref[kb] = v_ref[pl.ds(pl.multiple_of(kb * tk, tk), tk), :].T
            return 0

        lax.fori_loop(0, vt_ref.shape[0], transpose_values, 0)

    m_ref[...] = jnp.full(m_ref.shape, NEG_BIG, F32)
    l_ref[...] = jnp.zeros(l_ref.shape, F32)
    acc_ref[...] = jnp.zeros(acc_ref.shape, F32)

    def attend(kb, near):
        keys = pl.ds(pl.multiple_of(kb * tk, tk), tk)
        madd = madd_ref[kb]
        for h in range(DSA_HEADS):
            hs = slice(h * dh, (h + 1) * dh)
            s_ref[h] = lax.dot_general(k_ref[keys, hs], q_ref[:, hs], NT_DIMS,
                                       preferred_element_type=F32)
        for h in range(DSA_HEADS):
            hs = slice(h * dh, (h + 1) * dh)
            s = s_ref[h] * scale
            if near is not None:
                s = s + corr_ref[h, near]
            s = s + madd
            m_prev = m_ref[h, 0:1, :]
            m_next = jnp.maximum(m_prev, jnp.max(s, axis=0, keepdims=True))
            alpha = jnp.exp(m_prev - m_next)
            p = jnp.exp(s - m_next)
            l_next = alpha * l_ref[h, 0:1, :] + jnp.sum(p, axis=0, keepdims=True)
            pv = jnp.dot(vt_ref[kb, hs, :], p.astype(BF16),
                         preferred_element_type=F32)
            acc_ref[h] = alpha * acc_ref[h] + pv
            m_ref[h] = jnp.broadcast_to(m_next, (sub, tq))
            l_ref[h] = jnp.broadcast_to(l_next, (sub, tq))

    def far(kb, _):
        attend(kb, None)
        return 0

    lax.fori_loop(0, i - 1, far, 0)

    @pl.when(i >= 1)
    def _():
        attend(i - 1, 1)

    attend(i, 0)
    for h in range(DSA_HEADS):
        o_h = acc_ref[h] / l_ref[h, 0:1, :]
        o_ref[:, h * dh:(h + 1) * dh] = o_h.T.astype(o_ref.dtype)


def dsa(pa, pb, corr, tq):
    b, l, _ = pa.shape
    dh = LANES
    hd = DSA_HEADS * dh
    qb = (2 * GLA_HEADS * LANES + GLA_HEADS * 2 * LANES) // hd
    ikb = (qb + 4) * hd // (2 * LANES)
    iwb = GLA_HEADS * 2 * LANES // LANES + 1
    topk = min(TOPK_MAX, l // 4)
    assert l % tq == 0 and tq >= topk and tq >= REL_MAX_DIST
    nkb = l // tq
    return pl.pallas_call(
        functools.partial(_dsa_kernel, topk=topk),
        grid=(b, l // tq),
        in_specs=[
            pl.BlockSpec((None, tq, hd), lambda bi, i: (bi, i, qb)),
            pl.BlockSpec((None, tq, IDX_HEADS * IDX_DH), lambda bi, i: (bi, i, qb + 3)),
            pl.BlockSpec((None, tq, LANES), lambda bi, i: (bi, i, iwb)),
            _resident((None, l, hd), lambda bi, i: (bi, 0, qb + 1)),
            _resident((None, l, hd), lambda bi, i: (bi, 0, qb + 2)),
            _resident((None, l, 2 * LANES), lambda bi, i: (bi, 0, ikb)),
            _resident((DSA_HEADS, 2, tq, tq), lambda bi, i: (0, 0, 0, 0)),
        ],
        out_specs=pl.BlockSpec((None, tq, hd), lambda bi, i: (bi, i, 0)),
        out_shape=jax.ShapeDtypeStruct((b, l, hd), BF16),
        scratch_shapes=[
            pltpu.VMEM((nkb, tq, tq), jnp.int32),
            pltpu.VMEM((nkb, tq, tq), F32),
            pltpu.VMEM((LANES, tq), F32),
            pltpu.VMEM((8, tq), jnp.int32),
            pltpu.VMEM((DSA_HEADS, 8, tq), F32),
            pltpu.VMEM((DSA_HEADS, 8, tq), F32),
            pltpu.VMEM((DSA_HEADS, dh, tq), F32),
            pltpu.VMEM((nkb, hd, tq), BF16),
            pltpu.VMEM((DSA_HEADS, tq, tq), F32),
        ],
        compiler_params=_params("parallel", "arbitrary"),
        name="dsa",
    )(pa, pa, pb, pa, pa, pa, corr)


def _xattn_kernel(q_ref, kv_ref, o_ref):
    d = q_ref.shape[1]
    dh = d // XA_HEADS
    scale = dh ** -0.5
    for h in range(XA_HEADS):
        hs = slice(h * dh, (h + 1) * dh)
        s = lax.dot_general(q_ref[:, hs], kv_ref[:, hs], NT_DIMS, preferred_element_type=F32) * scale
        e = jnp.exp(s - jnp.max(s, axis=-1, keepdims=True))
        o = jnp.dot(e.astype(BF16), kv_ref[:, d + h * dh:d + (h + 1) * dh], preferred_element_type=F32)
        o_ref[:, hs] = (o / jnp.sum(e, axis=-1, keepdims=True)).astype(o_ref.dtype)


def xattn(q, kv, tq):
    b, l, d = q.shape
    nm = kv.shape[1]
    return pl.pallas_call(
        _xattn_kernel,
        grid=(b, l // tq),
        in_specs=[
            pl.BlockSpec((None, tq, d), lambda bi, i: (bi, i, 0)),
            pl.BlockSpec((None, nm, 2 * d), lambda bi, i: (bi, 0, 0)),
        ],
        out_specs=pl.BlockSpec((None, tq, d), lambda bi, i: (bi, i, 0)),
        out_shape=jax.ShapeDtypeStruct((b, l, d), BF16),
        compiler_params=_params("parallel", "parallel"),
        name="xattn",
    )(q, kv)


def _rel_bucket(dist):
    n = jnp.maximum(dist, 0)
    max_exact = REL_BUCKETS // 2
    nf = jnp.maximum(n, 1).astype(F32)
    large = max_exact + (jnp.log(nf / max_exact) / math.log(REL_MAX_DIST / max_exact)
                         * (REL_BUCKETS - max_exact)).astype(jnp.int32)
    large = jnp.minimum(large, REL_BUCKETS - 1)
    return jnp.where(n < max_exact, n, large)


def _bias_correction_tiles(rel_table, tq):
    t = jnp.arange(tq)
    d0 = t[:, None] - t[None, :]
    buckets = jnp.stack([_rel_bucket(d0), _rel_bucket(d0 + tq)], axis=0)
    onehot = (buckets[..., None] == jnp.arange(REL_BUCKETS)).astype(F32)
    far = rel_table[_rel_bucket(jnp.int32(2 * tq))]
    return jnp.einsum("ntsk,kh->hnst", onehot, rel_table - far, precision=lax.Precision.HIGHEST)


def _chunk_sum_matrices(tl):
    r = jnp.arange(tl)
    same = (r[:, None] // GLA_CHUNK) == (r[None, :] // GLA_CHUNK)
    incl = same & (r[None, :] <= r[:, None])
    after = same & (r[None, :] > r[:, None])
    return jnp.stack([incl, after]).astype(BF16)


def _split_in_proj(w):
    d = w.shape[0]
    half = d // 2
    sizes = [GLA_HEADS * LANES, GLA_HEADS * LANES, half, GLA_RANK, half,
             half, half, half, IDX_HEADS * IDX_DH, IDX_DH, IDX_HEADS]
    offs = [0]
    for s in sizes:
        offs.append(offs[-1] + s)
    assert offs[-1] == w.shape[1]
    seg = lambda n: w[:, offs[n]:offs[n + 1]]
    zero = lambda n: jnp.zeros((d, n), w.dtype)
    ik = seg(9)
    wa = jnp.concatenate([seg(0), seg(1), seg(2), seg(5), seg(6), seg(7), seg(8),
                          ik, zero(2 * IDX_DH), ik], axis=1)
    wb = jnp.concatenate([seg(4), seg(3), zero(LANES - GLA_RANK),
                          seg(10), zero(LANES - IDX_HEADS)], axis=1)
    return wa.astype(BF16), wb.astype(BF16)


def kernel(x, mem, norm_gains, w_in, w_gate_up, b_gate, gla_head_gain, rel_table,
           w_out, w_xq, w_xkv, w_xo, w_ffn_in, w_ffn_out):
    b, l, d = x.shape
    nm = mem.shape[1]
    depth = w_in.shape[0]
    d_ff = w_ffn_out.shape[1]
    m = b * l
    assert d // 2 // GLA_HEADS == 2 * LANES and d // 2 // DSA_HEADS == LANES

    tq_dsa = 256
    tl_gla = 512
    corr = _bias_correction_tiles(rel_table, tq_dsa)
    tri = _chunk_sum_matrices(tl_gla)
    w2p = jnp.pad(w_gate_up, ((0, 0), (0, LANES - GLA_RANK), (0, 0))).astype(BF16)

    xf = x.reshape(m, d)
    memf = mem.reshape(b * nm, d)
    for li in range(depth):
        g = norm_gains[li]
        wa, wb = _split_in_proj(w_in[li])
        pa = norm_matmul(xf, g[0], wa, BF16, bm=512, bn=1280).reshape(b, l, -1)
        pb = norm_matmul(xf, g[0], wb, F32, bm=512, bn=1280).reshape(b, l, -1)
        o_gla = gla(pa, pb, w2p[li], b_gate[li].reshape(1, -1),
                    gla_head_gain[li].reshape(1, -1), tri, tl_gla)
        o_dsa = dsa(pa, pb, corr, tq_dsa)
        xf = mm_norm_res([o_gla.reshape(m, -1), o_dsa.reshape(m, -1)],
                         w_out[li].astype(BF16), g[1], xf, bm=512)
        q = norm_matmul(xf, g[2], w_xq[li].astype(BF16), BF16, bm=512, bn=1024)
        kv = norm_matmul(memf, g[6], w_xkv[li].astype(BF16), BF16, bm=512, bn=1024)
        o_x = xattn(q.reshape(b, l, d), kv.reshape(b, nm, 2 * d), tq=512)
        xf = mm_norm_res([o_x.reshape(m, d)], w_xo[li].astype(BF16), g[3], xf, bm=512)
        act = ffn_in(xf, g[4], w_ffn_in[li].astype(BF16), d_ff, bm=512, bn=512)
        xf = mm_norm_res([act], w_ffn_out[li].astype(BF16), g[5], xf, bm=512, nk=4)
    return xf.reshape(b, l, d)
```

```python
import functools
import math

import jax
import jax.numpy as jnp
from jax import lax
from jax.experimental import pallas as pl
from jax.experimental.pallas import tpu as pltpu

F32 = jnp.float32
BF16 = jnp.bfloat16

EPS = 1e-6
GLA_HEADS = 4
GLA_RANK = 16
GLA_TAU = 16.0
GLA_CHUNK = 64
DSA_HEADS = 8
IDX_HEADS = 16
IDX_DH = 64
TOPK_MAX = 256
REL_BUCKETS = 32
REL_MAX_DIST = 128
XA_HEADS = 4

LANES = 128
VMEM_LIMIT_BYTES = 56 * 1024 * 1024

PACK16 = 16

NEG_BIG = -1e30
HALF16 = 2 ** 15
LOG2E = 1.4426950408889634

NT_DIMS = (((1,), (1,)), ((), ()))
TN_DIMS = (((0,), (0,)), ((), ()))


def _params(*sem):
    return pltpu.CompilerParams(dimension_semantics=sem, vmem_limit_bytes=VMEM_LIMIT_BYTES)


def _resident(shape, index_map):
    return pl.BlockSpec(shape, index_map, pipeline_mode=pl.Buffered(1))


def _rms(x, g):
    return x * lax.rsqrt(jnp.mean(x * x, axis=-1, keepdims=True) + EPS) * g


def _rms_to(x_ref, g_ref, h_ref, chunk=128):
    g = g_ref[...]

    def body(c, _):
        rows = pl.ds(pl.multiple_of(c * chunk, chunk), chunk)
        h_ref[rows, :] = _rms(x_ref[rows, :], g).astype(h_ref.dtype)
        return 0

    lax.fori_loop(0, x_ref.shape[0] // chunk, body, 0)


def _norm_matmul_kernel(x_ref, g_ref, w_ref, o_ref, h_ref):
    @pl.when(pl.program_id(1) == 0)
    def _():
        _rms_to(x_ref, g_ref, h_ref)

    o_ref[...] = jnp.dot(h_ref[...], w_ref[...], preferred_element_type=F32).astype(o_ref.dtype)


def norm_matmul(x, g, w, layer, out_dtype, bm, bn):
    m, k = x.shape
    n = w.shape[2]
    assert m % bm == 0 and n % bn == 0
    return pl.pallas_call(
        _norm_matmul_kernel,
        grid=(m // bm, n // bn),
        in_specs=[
            pl.BlockSpec((bm, k), lambda i, j: (i, 0)),
            _resident((1, k), lambda i, j: (0, 0)),
            pl.BlockSpec((None, k, bn), lambda i, j: (layer, 0, j)),
        ],
        out_specs=pl.BlockSpec((bm, bn), lambda i, j: (i, j)),
        out_shape=jax.ShapeDtypeStruct((m, n), out_dtype),
        scratch_shapes=[pltpu.VMEM((bm, k), BF16)],
        compiler_params=_params("parallel", "arbitrary"),
        name="norm_matmul",
    )(x, g.reshape(1, k), w)


def _ffn_in_kernel(x_ref, g_ref, wg_ref, wu_ref, o_ref, h_ref):
    @pl.when(pl.program_id(1) == 0)
    def _():
        _rms_to(x_ref, g_ref, h_ref)

    h = h_ref[...]
    a = jnp.dot(h, wg_ref[...], preferred_element_type=F32)
    b = jnp.dot(h, wu_ref[...], preferred_element_type=F32)
    o_ref[...] = (a * (1.0 / (1.0 + jnp.exp(-a))) * b).astype(o_ref.dtype)


def ffn_in(x, g, w, layer, d_ff, bm, bn):
    m, k = x.shape
    assert m % bm == 0 and d_ff % bn == 0
    nj = d_ff // bn
    return pl.pallas_call(
        _ffn_in_kernel,
        grid=(m // bm, nj),
        in_specs=[
            pl.BlockSpec((bm, k), lambda i, j: (i, 0)),
            _resident((1, k), lambda i, j: (0, 0)),
            pl.BlockSpec((None, k, bn), lambda i, j: (layer, 0, j)),
            pl.BlockSpec((None, k, bn), lambda i, j: (layer, 0, j + nj)),
        ],
        out_specs=pl.BlockSpec((bm, bn), lambda i, j: (i, j)),
        out_shape=jax.ShapeDtypeStruct((m, d_ff), BF16),
        scratch_shapes=[pltpu.VMEM((bm, k), BF16)],
        compiler_params=_params("parallel", "arbitrary"),
        name="ffn_in",
    )(x, g.reshape(1, k), w, w)


def _mm_norm_res_kernel(*refs, n_lhs, nk):
    lhs = refs[:n_lhs]
    w_ref, g_ref, res_ref, o_ref = refs[n_lhs:n_lhs + 4]
    if nk == 1:
        y = None
        off = 0
        for a in lhs:
            wd = a.shape[1]
            part = jnp.dot(a[...], w_ref[off:off + wd, :], preferred_element_type=F32)
            y = part if y is None else y + part
            off += wd
        o_ref[...] = res_ref[...] + _rms(y, g_ref[...])
    else:
        acc_ref = refs[n_lhs + 4]
        k = pl.program_id(1)
        part = jnp.dot(lhs[0][...], w_ref[...], preferred_element_type=F32)

        @pl.when(k == 0)
        def _():
            acc_ref[...] = part

        @pl.when(k > 0)
        def _():
            acc_ref[...] += part

        @pl.when(k == nk - 1)
        def _():
            o_ref[...] = res_ref[...] + _rms(acc_ref[...], g_ref[...])


def mm_norm_res(lhs_list, w, layer, g, res, bm, nk=1):
    m, n = res.shape
    kk = w.shape[1]
    assert m % bm == 0 and kk % nk == 0
    bk = kk // nk
    n_lhs = len(lhs_list)
    if nk == 1:
        lhs_specs = [pl.BlockSpec((bm, a.shape[1]), lambda i, k: (i, 0)) for a in lhs_list]
        w_spec = _resident((None, kk, n), lambda i, k: (layer, 0, 0))
        scratch = []
    else:
        assert n_lhs == 1
        lhs_specs = [pl.BlockSpec((bm, bk), lambda i, k: (i, k))]
        w_spec = pl.BlockSpec((None, bk, n), lambda i, k: (layer, k, 0))
        scratch = [pltpu.VMEM((bm, n), F32)]
    return pl.pallas_call(
        functools.partial(_mm_norm_res_kernel, n_lhs=n_lhs, nk=nk),
        grid=(m // bm, nk),
        in_specs=lhs_specs + [
            w_spec,
            _resident((1, n), lambda i, k: (0, 0)),
            pl.BlockSpec((bm, n), lambda i, k: (i, 0)),
        ],
        out_specs=pl.BlockSpec((bm, n), lambda i, k: (i, 0)),
        out_shape=jax.ShapeDtypeStruct((m, n), F32),
        scratch_shapes=scratch,
        compiler_params=_params("parallel", "arbitrary"),
        name="mm_norm_res",
    )(*lhs_list, w, g.reshape(1, n), res)


def _gla_kernel(q_ref, k_ref, v_ref, z_ref, r_ref, w2_ref, bg_ref, gh_ref, tri_ref,
                o_ref, st_ref, *, dk):
    tl = q_ref.shape[0]

    @pl.when(pl.program_id(2) == 0)
    def _():
        st_ref[...] = jnp.zeros_like(st_ref)

    pre = jnp.dot(z_ref[...].astype(BF16), w2_ref[...], preferred_element_type=F32) + bg_ref[...]
    logg = (jnp.minimum(pre, 0.0) - jnp.log1p(jnp.exp(-jnp.abs(pre)))) * (1.0 / GLA_TAU)
    hi = logg.astype(BF16)
    lo = (logg - hi.astype(F32)).astype(BF16)
    hl = jnp.concatenate([hi, lo], axis=1)
    incl = tri_ref[0]
    cs = jnp.dot(incl, hl, preferred_element_type=F32)
    sf = jnp.dot(tri_ref[1], hl, preferred_element_type=F32)
    bcum = cs[:, :dk] + cs[:, dk:]
    suf = sf[:, :dk] + sf[:, dk:]

    qf = q_ref[...].astype(F32) * dk ** -0.5
    kf = k_ref[...].astype(F32)
    q_dec = (qf * jnp.exp(bcum)).astype(BF16)
    k_intra = (kf * jnp.exp(-bcum)).astype(BF16)
    k_state = (kf * jnp.exp(suf)).astype(BF16)
    v = v_ref[...]

    a = lax.dot_general(q_dec, k_intra, NT_DIMS, preferred_element_type=F32)
    a = jnp.where(incl > 0, a, 0.0).astype(BF16)
    o_intra = jnp.dot(a, v, preferred_element_type=F32)

    st = st_ref[...]
    outs = []
    for c in range(tl // GLA_CHUNK):
        rows = slice(c * GLA_CHUNK, (c + 1) * GLA_CHUNK)
        o_inter = lax.dot_general(q_dec[rows], st.astype(BF16), NT_DIMS, preferred_element_type=F32)
        outs.append(o_intra[rows] + o_inter)
        last = (c + 1) * GLA_CHUNK - 1
        decay = jnp.exp(bcum[last:last + 1, :])
        upd = lax.dot_general(v[rows], k_state[rows], TN_DIMS, preferred_element_type=F32)
        st = decay * st + upd
    st_ref[...] = st
    o = jnp.concatenate(outs, axis=0)
    o = _rms(o, gh_ref[...])
    r = r_ref[...]
    o_ref[...] = (o * (r * (1.0 / (1.0 + jnp.exp(-r))))).astype(o_ref.dtype)


def gla(pa, pb, w2p, bgate, ghead, tri, tl):
    b, l, _ = pa.shape
    dk = LANES
    dv = 2 * dk
    hq = GLA_HEADS
    vb = 2 * GLA_HEADS * dk // dv
    zb = GLA_HEADS * dv // LANES
    return pl.pallas_call(
        functools.partial(_gla_kernel, dk=dk),
        grid=(b, GLA_HEADS, l // tl),
        in_specs=[
            pl.BlockSpec((None, tl, dk), lambda bi, h, t: (bi, t, h)),
            pl.BlockSpec((None, tl, dk), lambda bi, h, t: (bi, t, hq + h)),
            pl.BlockSpec((None, tl, dv), lambda bi, h, t: (bi, t, vb + h)),
            pl.BlockSpec((None, tl, LANES), lambda bi, h, t: (bi, t, zb)),
            pl.BlockSpec((None, tl, dv), lambda bi, h, t: (bi, t, h)),
            pl.BlockSpec((LANES, dk), lambda bi, h, t: (0, h)),
            pl.BlockSpec((1, dk), lambda bi, h, t: (0, h)),
            _resident((1, dv), lambda bi, h, t: (0, 0)),
            _resident((2, tl, tl), lambda bi, h, t: (0, 0, 0)),
        ],
        out_specs=pl.BlockSpec((None, tl, dv), lambda bi, h, t: (bi, t, h)),
        out_shape=jax.ShapeDtypeStruct((b, l, GLA_HEADS * dv), BF16),
        scratch_shapes=[pltpu.VMEM((dv, dk), F32)],
        compiler_params=_params("parallel", "parallel", "arbitrary"),
        name="gla",
    )(pa, pa, pa, pb, pb, w2p, bgate, ghead, tri)


def _dsa_kernel(q_ref, iq_ref, iw_ref, k_ref, v_ref, ik_ref, corr_ref, o_ref,
                key_ref, hi_ref, lo_ref, madd_ref, wt_ref, m_ref, l_ref, acc_ref, vt_ref, s_ref,
                *, topk):
    tq = q_ref.shape[0]
    tk = tq
    dh = LANES
    sub = 8
    i = pl.program_id(1)
    nkb = i + 1
    scale = dh ** -0.5

    wt_ref[...] = (iw_ref[...] * (IDX_HEADS ** -0.5 * IDX_DH ** -0.5)).T
    qpos = lax.broadcasted_iota(jnp.int32, (tk, tq), 1) + i * tq
    kpos0 = lax.broadcasted_iota(jnp.int32, (tk, tq), 0)

    def score_block(kb, _):
        ik = ik_ref[pl.ds(pl.multiple_of(kb * tk, tk), tk), :]
        ik_even, ik_odd = ik[:, :LANES], ik[:, LANES:]
        acc = jnp.zeros((tk, tq), F32)
        for p in range(IDX_HEADS // 2):
            qp = iq_ref[:, p * LANES:(p + 1) * LANES]
            for par, ikx in ((0, ik_even), (1, ik_odd)):
                h = 2 * p + par
                d = lax.dot_general(ikx, qp, NT_DIMS, preferred_element_type=F32)
                acc = acc + jnp.maximum(d, 0.0) * wt_ref[h:h + 1, :]
        sc = jnp.where(kpos0 + kb * tk <= qpos, acc, -jnp.inf)
        bits = lax.bitcast_convert_type(sc, jnp.int32)
        key = bits ^ ((bits >> 31) & 0x7FFFFFFF)
        key_ref[kb] = key
        hi_ref[kb] = (key >> 16).astype(jnp.int16)
        lo_ref[kb] = ((key & 0xFFFF) - HALF16).astype(jnp.int16)
        return 0

    lax.fori_loop(0, nkb, score_block, 0)

    n_acc = 4
    slabs = tk // PACK16
    one16, zero16 = jnp.int16(1), jnp.int16(0)

    def rows16(x):
        return jnp.broadcast_to(x, (PACK16, tq)).astype(jnp.int16)

    def count16(src_ref, cand, strict):
        cand16 = rows16(cand)

        def body(kb, cnts):
            cnts = list(cnts)
            for r in range(slabs):
                blk = src_ref[kb, r * PACK16:(r + 1) * PACK16, :]
                hit = (blk > cand16) if strict else (blk >= cand16)
                cnts[r % n_acc] = cnts[r % n_acc] + jnp.where(hit, one16, zero16)
            return tuple(cnts)

        cnts = lax.fori_loop(0, nkb, body, (jnp.zeros((PACK16, tq), jnp.int16),) * n_acc)
        cnt = ((cnts[0] + cnts[1]) + (cnts[2] + cnts[3])).astype(jnp.int32)
        return jnp.sum(cnt, axis=0, keepdims=True)

    def bisect16(src_ref, kth):
        def step(s, t):
            cand = t + lax.shift_left(jnp.int32(1), 15 - s)
            return jnp.where(count16(src_ref, cand, False) >= kth, cand, t)

        return lax.fori_loop(0, 16, step, jnp.full((1, tq), -HALF16, jnp.int32))

    t_hi = bisect16(hi_ref, topk)
    above = count16(hi_ref, t_hi, True)
    t_hi16 = rows16(t_hi)

    def keep_threshold_bucket(kb, _):
        for r in range(slabs):
            rows = slice(r * PACK16, (r + 1) * PACK16)
            lo_ref[kb, rows, :] = jnp.where(hi_ref[kb, rows, :] == t_hi16, lo_ref[kb, rows, :],
                                            jnp.int16(-HALF16))
        return 0

    lax.fori_loop(0, nkb, keep_threshold_bucket, 0)
    t_lo = bisect16(lo_ref, topk - above)
    thr = t_hi * (2 * HALF16) + (t_lo + HALF16)

    def mask_block(kb, _):
        vis = kpos0 + kb * tk <= qpos
        madd_ref[kb] = jnp.where(vis, jnp.where(key_ref[kb] >= thr, 0.0, NEG_BIG), NEG_BIG)
        return 0

    lax.fori_loop(0, nkb, mask_block, 0)

    @pl.when(i == 0)
    def _():
        def transpose_values(kb, _):
            vt_ref[kb] = v_ref[pl.ds(pl.multiple_of(kb * tk, tk), tk), :].T
            return 0

        lax.fori_loop(0, vt_ref.shape[0], transpose_values, 0)

    m_ref[...] = jnp.full(m_ref.shape, NEG_BIG, F32)
    l_ref[...] = jnp.zeros(l_ref.shape, F32)
    acc_ref[...] = jnp.zeros(acc_ref.shape, F32)

    def attend(kb, near):
        keys = pl.ds(pl.multiple_of(kb * tk, tk), tk)
        madd = madd_ref[kb]
        for h in range(DSA_HEADS):
            hs = slice(h * dh, (h + 1) * dh)
            s_ref[h] = lax.dot_general(k_ref[keys, hs], q_ref[:, hs], NT_DIMS,
                                       preferred_element_type=F32)
        for h in range(DSA_HEADS):
            hs = slice(h * dh, (h + 1) * dh)
            s = s_ref[h] * (scale * LOG2E)
            if near is not None:
                s = s + corr_ref[h, near]
            s = s + madd
            m_prev = m_ref[h, 0:1, :]
            m_next = jnp.maximum(m_prev, jnp.max(s, axis=0, keepdims=True))
            alpha = jnp.exp2(m_prev - m_next)
            p = jnp.exp2(s - m_next)
            l_next = alpha * l_ref[h, 0:1, :] + jnp.sum(p, axis=0, keepdims=True)
            pv = jnp.dot(vt_ref[kb, hs, :], p.astype(BF16),
                         preferred_element_type=F32)
            acc_ref[h] = alpha * acc_ref[h] + pv
            m_ref[h] = jnp.broadcast_to(m_next, (sub, tq))
            l_ref[h] = jnp.broadcast_to(l_next, (sub, tq))

    def far(kb, _):
        attend(kb, None)
        return 0

    lax.fori_loop(0, i - 1, far, 0)

    @pl.when(i >= 1)
    def _():
        attend(i - 1, 1)

    attend(i, 0)
    for h in range(DSA_HEADS):
        o_h = acc_ref[h] / l_ref[h, 0:1, :]
        o_ref[:, h * dh:(h + 1) * dh] = o_h.T.astype(o_ref.dtype)


def dsa(pa, pb, corr, tq):
    b, l, _ = pa.shape
    dh = LANES
    hd = DSA_HEADS * dh
    qb = (2 * GLA_HEADS * LANES + GLA_HEADS * 2 * LANES) // hd
    ikb = (qb + 4) * hd // (2 * LANES)
    iwb = GLA_HEADS * 2 * LANES // LANES + 1
    topk = min(TOPK_MAX, l // 4)
    assert l % tq == 0 and tq >= topk and tq >= REL_MAX_DIST
    nkb = l // tq
    assert nkb * tq // PACK16 < HALF16
    return pl.pallas_call(
        functools.partial(_dsa_kernel, topk=topk),
        grid=(b, l // tq),
        in_specs=[
            pl.BlockSpec((None, tq, hd), lambda bi, i: (bi, i, qb)),
            pl.BlockSpec((None, tq, IDX_HEADS * IDX_DH), lambda bi, i: (bi, i, qb + 3)),
            pl.BlockSpec((None, tq, LANES), lambda bi, i: (bi, i, iwb)),
            _resident((None, l, hd), lambda bi, i: (bi, 0, qb + 1)),
            _resident((None, l, hd), lambda bi, i: (bi, 0, qb + 2)),
            _resident((None, l, 2 * LANES), lambda bi, i: (bi, 0, ikb)),
            _resident((DSA_HEADS, 2, tq, tq), lambda bi, i: (0, 0, 0, 0)),
        ],
        out_specs=pl.BlockSpec((None, tq, hd), lambda bi, i: (bi, i, 0)),
        out_shape=jax.ShapeDtypeStruct((b, l, hd), BF16),
        scratch_shapes=[
            pltpu.VMEM((nkb, tq, tq), jnp.int32),
            pltpu.VMEM((nkb, tq, tq), jnp.int16),
            pltpu.VMEM((nkb, tq, tq), jnp.int16),
            pltpu.VMEM((nkb, tq, tq), F32),
            pltpu.VMEM((LANES, tq), F32),
            pltpu.VMEM((DSA_HEADS, 8, tq), F32),
            pltpu.VMEM((DSA_HEADS, 8, tq), F32),
            pltpu.VMEM((DSA_HEADS, dh, tq), F32),
            pltpu.VMEM((nkb, hd, tq), BF16),
            pltpu.VMEM((DSA_HEADS, tq, tq), F32),
        ],
        compiler_params=_params("parallel", "arbitrary"),
        name="dsa",
    )(pa, pa, pb, pa, pa, pa, corr)


def _xattn_kernel(q_ref, kv_ref, o_ref):
    d = q_ref.shape[1]
    dh = d // XA_HEADS
    scale = dh ** -0.5
    for h in range(XA_HEADS):
        hs = slice(h * dh, (h + 1) * dh)
        s = lax.dot_general(q_ref[:, hs], kv_ref[:, hs], NT_DIMS, preferred_element_type=F32) * scale
        e = jnp.exp(s - jnp.max(s, axis=-1, keepdims=True))
        o = jnp.dot(e.astype(BF16), kv_ref[:, d + h * dh:d + (h + 1) * dh], preferred_element_type=F32)
        o_ref[:, hs] = (o / jnp.sum(e, axis=-1, keepdims=True)).astype(o_ref.dtype)


def xattn(q, kv, tq):
    b, l, d = q.shape
    nm = kv.shape[1]
    return pl.pallas_call(
        _xattn_kernel,
        grid=(b, l // tq),
        in_specs=[
            pl.BlockSpec((None, tq, d), lambda bi, i: (bi, i, 0)),
            pl.BlockSpec((None, nm, 2 * d), lambda bi, i: (bi, 0, 0)),
        ],
        out_specs=pl.BlockSpec((None, tq, d), lambda bi, i: (bi, i, 0)),
        out_shape=jax.ShapeDtypeStruct((b, l, d), BF16),
        compiler_params=_params("parallel", "parallel"),
        name="xattn",
    )(q, kv)


def _rel_bucket(dist):
    n = jnp.maximum(dist, 0)
    max_exact = REL_BUCKETS // 2
    nf = jnp.maximum(n, 1).astype(F32)
    large = max_exact + (jnp.log(nf / max_exact) / math.log(REL_MAX_DIST / max_exact)
                         * (REL_BUCKETS - max_exact)).astype(jnp.int32)
    large = jnp.minimum(large, REL_BUCKETS - 1)
    return jnp.where(n < max_exact, n, large)


def _bias_correction_tiles(rel_table, tq):
    t = jnp.arange(tq)
    d0 = t[:, None] - t[None, :]
    buckets = jnp.stack([_rel_bucket(d0), _rel_bucket(d0 + tq)], axis=0)
    onehot = (buckets[..., None] == jnp.arange(REL_BUCKETS)).astype(F32)
    far = rel_table[_rel_bucket(jnp.int32(2 * tq))]
    return jnp.einsum("ntsk,kh->hnst", onehot, (rel_table - far) * LOG2E,
                      precision=lax.Precision.HIGHEST)


def _chunk_sum_matrices(tl):
    r = jnp.arange(tl)
    same = (r[:, None] // GLA_CHUNK) == (r[None, :] // GLA_CHUNK)
    incl = same & (r[None, :] <= r[:, None])
    after = same & (r[None, :] > r[:, None])
    return jnp.stack([incl, after]).astype(BF16)


def _split_in_proj(w):
    depth, d, _ = w.shape
    half = d // 2
    sizes = [GLA_HEADS * LANES, GLA_HEADS * LANES, half, GLA_RANK, half,
             half, half, half, IDX_HEADS * IDX_DH, IDX_DH, IDX_HEADS]
    offs = [0]
    for s in sizes:
        offs.append(offs[-1] + s)
    assert offs[-1] == w.shape[2]
    seg = lambda n: w[:, :, offs[n]:offs[n + 1]]
    zero = lambda n: jnp.zeros((depth, d, n), w.dtype)
    ik = seg(9)
    wa = jnp.concatenate([seg(0), seg(1), seg(2), seg(5), seg(6), seg(7), seg(8),
                          ik, zero(2 * IDX_DH), ik], axis=2)
    wb = jnp.concatenate([seg(4), seg(3), zero(LANES - GLA_RANK),
                          seg(10), zero(LANES - IDX_HEADS)], axis=2)
    return wa, wb


def kernel(x, mem, norm_gains, w_in, w_gate_up, b_gate, gla_head_gain, rel_table,
           w_out, w_xq, w_xkv, w_xo, w_ffn_in, w_ffn_out):
    b, l, d = x.shape
    nm = mem.shape[1]
    depth = w_in.shape[0]
    d_ff = w_ffn_out.shape[1]
    m = b * l
    assert d // 2 // GLA_HEADS == 2 * LANES and d // 2 // DSA_HEADS == LANES

    tq_dsa = 256
    tl_gla = 512
    corr = _bias_correction_tiles(rel_table, tq_dsa)
    tri = _chunk_sum_matrices(tl_gla)
    w2p = jnp.pad(w_gate_up, ((0, 0), (0, LANES - GLA_RANK), (0, 0))).astype(BF16)

    wa, wb = _split_in_proj(w_in.astype(BF16))
    w_out, w_xq, w_xkv, w_xo, w_ffn_in, w_ffn_out = (
        w.astype(BF16) for w in (w_out, w_xq, w_xkv, w_xo, w_ffn_in, w_ffn_out))

    xf = x.reshape(m, d)
    memf = mem.reshape(b * nm, d)
    for li in range(depth):
        g = norm_gains[li]
        pa = norm_matmul(xf, g[0], wa, li, BF16, bm=512, bn=1280).reshape(b, l, -1)
        pb = norm_matmul(xf, g[0], wb, li, F32, bm=512, bn=1280).reshape(b, l, -1)
        o_gla = gla(pa, pb, w2p[li], b_gate[li].reshape(1, -1),
                    gla_head_gain[li].reshape(1, -1), tri, tl_gla)
        o_dsa = dsa(pa, pb, corr, tq_dsa)
        xf = mm_norm_res([o_gla.reshape(m, -1), o_dsa.reshape(m, -1)], w_out, li, g[1], xf, bm=512)
        q = norm_matmul(xf, g[2], w_xq, li, BF16, bm=512, bn=1024)
        kv = norm_matmul(memf, g[6], w_xkv, li, BF16, bm=min(512, b * nm), bn=1024)
        o_x = xattn(q.reshape(b, l, d), kv.reshape(b, nm, 2 * d), tq=512)
        xf = mm_norm_res([o_x.reshape(m, d)], w_xo, li, g[3], xf, bm=512)
        act = ffn_in(xf, g[4], w_ffn_in, li, d_ff, bm=512, bn=512)
        xf = mm_norm_res([act], w_ffn_out, li, g[5], xf, bm=512, nk=4)
    return xf.reshape(b, l, d)
```

```python
import functools
import math

import jax
import jax.numpy as jnp
from jax import lax
from jax.experimental import pallas as pl
from jax.experimental.pallas import tpu as pltpu

F32 = jnp.float32
BF16 = jnp.bfloat16

EPS = 1e-6
GLA_HEADS = 4
GLA_RANK = 16
GLA_TAU = 16.0
GLA_CHUNK = 64
DSA_HEADS = 8
IDX_HEADS = 16
IDX_DH = 64
TOPK_MAX = 256
REL_BUCKETS = 32
REL_MAX_DIST = 128
XA_HEADS = 4

LANES = 128
VMEM_LIMIT_BYTES = 56 * 1024 * 1024

PACK16 = 16

NEG_BIG = -1e30
HALF16 = 2 ** 15
LOG2E = 1.4426950408889634

NT_DIMS = (((1,), (1,)), ((), ()))
TN_DIMS = (((0,), (0,)), ((), ()))


def _params(*sem):
    return pltpu.CompilerParams(dimension_semantics=sem, vmem_limit_bytes=VMEM_LIMIT_BYTES)


def _resident(shape, index_map):
    return pl.BlockSpec(shape, index_map, pipeline_mode=pl.Buffered(1))


def _rms(x, g):
    return x * lax.rsqrt(jnp.mean(x * x, axis=-1, keepdims=True) + EPS) * g


def _rms_to(x_ref, g_ref, h_ref, chunk=128):
    g = g_ref[...]

    def body(c, _):
        rows = pl.ds(pl.multiple_of(c * chunk, chunk), chunk)
        h_ref[rows, :] = _rms(x_ref[rows, :], g).astype(h_ref.dtype)
        return 0

    lax.fori_loop(0, x_ref.shape[0] // chunk, body, 0)


def _norm_matmul_kernel(x_ref, g_ref, w_ref, o_ref, h_ref):
    @pl.when(pl.program_id(1) == 0)
    def _():
        _rms_to(x_ref, g_ref, h_ref)

    o_ref[...] = jnp.dot(h_ref[...], w_ref[...], preferred_element_type=F32).astype(o_ref.dtype)


def norm_matmul(x, g, w, layer, out_dtype, bm, bn):
    m, k = x.shape
    n = w.shape[2]
    assert m % bm == 0 and n % bn == 0
    return pl.pallas_call(
        _norm_matmul_kernel,
        grid=(m // bm, n // bn),
        in_specs=[
            pl.BlockSpec((bm, k), lambda i, j: (i, 0)),
            _resident((1, k), lambda i, j: (0, 0)),
            pl.BlockSpec((None, k, bn), lambda i, j: (layer, 0, j)),
        ],
        out_specs=pl.BlockSpec((bm, bn), lambda i, j: (i, j)),
        out_shape=jax.ShapeDtypeStruct((m, n), out_dtype),
        scratch_shapes=[pltpu.VMEM((bm, k), BF16)],
        compiler_params=_params("parallel", "arbitrary"),
        name="norm_matmul",
    )(x, g.reshape(1, k), w)


def _in_proj_kernel(x_ref, g_ref, wh_ref, wm_ref, ws_ref, oa_ref, ob_ref, h_ref, *, n_head_blocks):
    j = pl.program_id(1)

    @pl.when(j == 0)
    def _():
        _rms_to(x_ref, g_ref, h_ref)
        ob_ref[...] = jnp.dot(h_ref[...], ws_ref[...], preferred_element_type=F32)

    @pl.when(j < n_head_blocks)
    def _():
        oa_ref[...] = jnp.dot(h_ref[...], wh_ref[...], preferred_element_type=F32).astype(oa_ref.dtype)

    @pl.when(j >= n_head_blocks)
    def _():
        oa_ref[...] = jnp.dot(h_ref[...], wm_ref[...], preferred_element_type=F32).astype(oa_ref.dtype)


def in_proj(x, g, w, n_head, w_mid, w_small, layer, bm, bn):
    m, k = x.shape
    n_mid, n_small = w_mid.shape[2], w_small.shape[2]
    assert m % bm == 0 and n_head % bn == 0 and n_mid % bn == 0
    nh, nm = n_head // bn, n_mid // bn
    return pl.pallas_call(
        functools.partial(_in_proj_kernel, n_head_blocks=nh),
        grid=(m // bm, nh + nm),
        in_specs=[
            pl.BlockSpec((bm, k), lambda i, j: (i, 0)),
            _resident((1, k), lambda i, j: (0, 0)),
            pl.BlockSpec((None, k, bn), lambda i, j: (layer, 0, jnp.minimum(j, nh - 1))),
            pl.BlockSpec((None, k, bn), lambda i, j: (layer, 0, jnp.maximum(j - nh, 0))),
            _resident((None, k, n_small), lambda i, j: (layer, 0, 0)),
        ],
        out_specs=[
            pl.BlockSpec((bm, bn), lambda i, j: (i, j)),
            pl.BlockSpec((bm, n_small), lambda i, j: (i, 0)),
        ],
        out_shape=[
            jax.ShapeDtypeStruct((m, n_head + n_mid), BF16),
            jax.ShapeDtypeStruct((m, n_small), F32),
        ],
        scratch_shapes=[pltpu.VMEM((bm, k), BF16)],
        compiler_params=_params("parallel", "arbitrary"),
        name="in_proj",
    )(x, g.reshape(1, k), w, w_mid, w_small)


def _ffn_in_kernel(x_ref, g_ref, wg_ref, wu_ref, o_ref, h_ref):
    @pl.when(pl.program_id(1) == 0)
    def _():
        _rms_to(x_ref, g_ref, h_ref)

    h = h_ref[...]
    a = jnp.dot(h, wg_ref[...], preferred_element_type=F32)
    b = jnp.dot(h, wu_ref[...], preferred_element_type=F32)
    o_ref[...] = (a * (1.0 / (1.0 + jnp.exp(-a))) * b).astype(o_ref.dtype)


def ffn_in(x, g, w, layer, d_ff, bm, bn):
    m, k = x.shape
    assert m % bm == 0 and d_ff % bn == 0
    nj = d_ff // bn
    return pl.pallas_call(
        _ffn_in_kernel,
        grid=(m // bm, nj),
        in_specs=[
            pl.BlockSpec((bm, k), lambda i, j: (i, 0)),
            _resident((1, k), lambda i, j: (0, 0)),
            pl.BlockSpec((None, k, bn), lambda i, j: (layer, 0, j)),
            pl.BlockSpec((None, k, bn), lambda i, j: (layer, 0, j + nj)),
        ],
        out_specs=pl.BlockSpec((bm, bn), lambda i, j: (i, j)),
        out_shape=jax.ShapeDtypeStruct((m, d_ff), BF16),
        scratch_shapes=[pltpu.VMEM((bm, k), BF16)],
        compiler_params=_params("parallel", "arbitrary"),
        name="ffn_in",
    )(x, g.reshape(1, k), w, w)


def _mm_norm_res_kernel(*refs, n_lhs, nk):
    lhs = refs[:n_lhs]
    w_ref, g_ref, res_ref, o_ref = refs[n_lhs:n_lhs + 4]
    if nk == 1:
        y = None
        off = 0
        for a in lhs:
            wd = a.shape[1]
            part = jnp.dot(a[...], w_ref[off:off + wd, :], preferred_element_type=F32)
            y = part if y is None else y + part
            off += wd
        o_ref[...] = res_ref[...] + _rms(y, g_ref[...])
    else:
        acc_ref = refs[n_lhs + 4]
        k = pl.program_id(1)
        part = jnp.dot(lhs[0][...], w_ref[...], preferred_element_type=F32)

        @pl.when(k == 0)
        def _():
            acc_ref[...] = part

        @pl.when(k > 0)
        def _():
            acc_ref[...] += part

        @pl.when(k == nk - 1)
        def _():
            o_ref[...] = res_ref[...] + _rms(acc_ref[...], g_ref[...])


def mm_norm_res(lhs_list, w, layer, g, res, bm, nk=1):
    m, n = res.shape
    kk = w.shape[1]
    assert m % bm == 0 and kk % nk == 0
    bk = kk // nk
    n_lhs = len(lhs_list)
    if nk == 1:
        lhs_specs = [pl.BlockSpec((bm, a.shape[1]), lambda i, k: (i, 0)) for a in lhs_list]
        w_spec = _resident((None, kk, n), lambda i, k: (layer, 0, 0))
        scratch = []
    else:
        assert n_lhs == 1
        lhs_specs = [pl.BlockSpec((bm, bk), lambda i, k: (i, k))]
        w_spec = pl.BlockSpec((None, bk, n), lambda i, k: (layer, k, 0))
        scratch = [pltpu.VMEM((bm, n), F32)]
    return pl.pallas_call(
        functools.partial(_mm_norm_res_kernel, n_lhs=n_lhs, nk=nk),
        grid=(m // bm, nk),
        in_specs=lhs_specs + [
            w_spec,
            _resident((1, n), lambda i, k: (0, 0)),
            pl.BlockSpec((bm, n), lambda i, k: (i, 0)),
        ],
        out_specs=pl.BlockSpec((bm, n), lambda i, k: (i, 0)),
        out_shape=jax.ShapeDtypeStruct((m, n), F32),
        scratch_shapes=scratch,
        compiler_params=_params("parallel", "arbitrary"),
        name="mm_norm_res",
    )(*lhs_list, w, g.reshape(1, n), res)


def _gla_kernel(q_ref, k_ref, v_ref, z_ref, r_ref, w2_ref, bg_ref, gh_ref, tri_ref,
                o_ref, st_ref, *, dk):
    tl = q_ref.shape[0]

    @pl.when(pl.program_id(2) == 0)
    def _():
        st_ref[...] = jnp.zeros_like(st_ref)

    pre = jnp.dot(z_ref[...].astype(BF16), w2_ref[...], preferred_element_type=F32) + bg_ref[...]
    logg = (jnp.minimum(pre, 0.0) - jnp.log1p(jnp.exp(-jnp.abs(pre)))) * (1.0 / GLA_TAU)
    hi = logg.astype(BF16)
    lo = (logg - hi.astype(F32)).astype(BF16)
    hl = jnp.concatenate([hi, lo], axis=1)
    incl = tri_ref[0]
    cs = jnp.dot(incl, hl, preferred_element_type=F32)
    sf = jnp.dot(tri_ref[1], hl, preferred_element_type=F32)
    bcum = cs[:, :dk] + cs[:, dk:]
    suf = sf[:, :dk] + sf[:, dk:]

    qf = q_ref[...].astype(F32) * dk ** -0.5
    kf = k_ref[...].astype(F32)
    q_dec = (qf * jnp.exp(bcum)).astype(BF16)
    k_intra = (kf * jnp.exp(-bcum)).astype(BF16)
    k_state = (kf * jnp.exp(suf)).astype(BF16)
    v = v_ref[...]

    a = lax.dot_general(q_dec, k_intra, NT_DIMS, preferred_element_type=F32)
    a = jnp.where(incl > 0, a, 0.0).astype(BF16)
    o_intra = jnp.dot(a, v, preferred_element_type=F32)

    st = st_ref[...]
    outs = []
    for c in range(tl // GLA_CHUNK):
        rows = slice(c * GLA_CHUNK, (c + 1) * GLA_CHUNK)
        o_inter = lax.dot_general(q_dec[rows], st.astype(BF16), NT_DIMS, preferred_element_type=F32)
        outs.append(o_intra[rows] + o_inter)
        last = (c + 1) * GLA_CHUNK - 1
        decay = jnp.exp(bcum[last:last + 1, :])
        upd = lax.dot_general(v[rows], k_state[rows], TN_DIMS, preferred_element_type=F32)
        st = decay * st + upd
    st_ref[...] = st
    o = jnp.concatenate(outs, axis=0)
    o = _rms(o, gh_ref[...])
    r = r_ref[...]
    o_ref[...] = (o * (r * (1.0 / (1.0 + jnp.exp(-r))))).astype(o_ref.dtype)


def gla(pa, pb, w2p, bgate, ghead, tri, tl):
    b, l, _ = pa.shape
    dk = LANES
    dv = 2 * dk
    hq = GLA_HEADS
    vb = 2 * GLA_HEADS * dk // dv
    zb = GLA_HEADS * dv // LANES
    return pl.pallas_call(
        functools.partial(_gla_kernel, dk=dk),
        grid=(b, GLA_HEADS, l // tl),
        in_specs=[
            pl.BlockSpec((None, tl, dk), lambda bi, h, t: (bi, t, h)),
            pl.BlockSpec((None, tl, dk), lambda bi, h, t: (bi, t, hq + h)),
            pl.BlockSpec((None, tl, dv), lambda bi, h, t: (bi, t, vb + h)),
            pl.BlockSpec((None, tl, LANES), lambda bi, h, t: (bi, t, zb)),
            pl.BlockSpec((None, tl, dv), lambda bi, h, t: (bi, t, h)),
            pl.BlockSpec((LANES, dk), lambda bi, h, t: (0, h)),
            pl.BlockSpec((1, dk), lambda bi, h, t: (0, h)),
            _resident((1, dv), lambda bi, h, t: (0, 0)),
            _resident((2, tl, tl), lambda bi, h, t: (0, 0, 0)),
        ],
        out_specs=pl.BlockSpec((None, tl, dv), lambda bi, h, t: (bi, t, h)),
        out_shape=jax.ShapeDtypeStruct((b, l, GLA_HEADS * dv), BF16),
        scratch_shapes=[pltpu.VMEM((dv, dk), F32)],
        compiler_params=_params("parallel", "parallel", "arbitrary"),
        name="gla",
    )(pa, pa, pa, pb, pb, w2p, bgate, ghead, tri)


def _dsa_kernel(q_ref, iq_ref, iw_ref, k_ref, v_ref, ik_ref, corr_ref, o_ref,
                key_ref, hi_ref, lo_ref, madd_ref, wt_ref, m_ref, l_ref, acc_ref, vt_ref, s_ref,
                *, topk):
    tq = q_ref.shape[0]
    tk = tq
    dh = LANES
    sub = 8
    i = pl.program_id(1)
    nkb = i + 1
    scale = dh ** -0.5

    wt_ref[...] = (iw_ref[...] * (IDX_HEADS ** -0.5 * IDX_DH ** -0.5)).T
    qpos = lax.broadcasted_iota(jnp.int32, (tk, tq), 1) + i * tq
    kpos0 = lax.broadcasted_iota(jnp.int32, (tk, tq), 0)

    def score_block(kb, _):
        ik = ik_ref[pl.ds(pl.multiple_of(kb * tk, tk), tk), :].astype(BF16)
        ik_even, ik_odd = ik[:, :LANES], ik[:, LANES:]
        acc = jnp.zeros((tk, tq), F32)
        for p in range(IDX_HEADS // 2):
            qp = iq_ref[:, p * LANES:(p + 1) * LANES]
            for par, ikx in ((0, ik_even), (1, ik_odd)):
                h = 2 * p + par
                d = lax.dot_general(ikx, qp, NT_DIMS, preferred_element_type=F32)
                acc = acc + jnp.maximum(d, 0.0) * wt_ref[h:h + 1, :]
        sc = jnp.where(kpos0 + kb * tk <= qpos, acc, -jnp.inf)
        bits = lax.bitcast_convert_type(sc, jnp.int32)
        key = bits ^ ((bits >> 31) & 0x7FFFFFFF)
        key_ref[kb] = key
        hi_ref[kb] = (key >> 16).astype(jnp.int16)
        lo_ref[kb] = ((key & 0xFFFF) - HALF16).astype(jnp.int16)
        return 0

    lax.fori_loop(0, nkb, score_block, 0)

    n_acc = 4
    slabs = tk // PACK16
    one16, zero16 = jnp.int16(1), jnp.int16(0)

    def rows16(x):
        return jnp.broadcast_to(x, (PACK16, tq)).astype(jnp.int16)

    def count16(src_ref, cand, strict):
        cand16 = rows16(cand)

        def body(kb, cnts):
            cnts = list(cnts)
            for r in range(slabs):
                blk = src_ref[kb, r * PACK16:(r + 1) * PACK16, :]
                hit = (blk > cand16) if strict else (blk >= cand16)
                cnts[r % n_acc] = cnts[r % n_acc] + jnp.where(hit, one16, zero16)
            return tuple(cnts)

        cnts = lax.fori_loop(0, nkb, body, (jnp.zeros((PACK16, tq), jnp.int16),) * n_acc)
        cnt = ((cnts[0] + cnts[1]) + (cnts[2] + cnts[3])).astype(jnp.int32)
        return jnp.sum(cnt, axis=0, keepdims=True)

    def bisect16(src_ref, kth):
        def step(s, t):
            cand = t + lax.shift_left(jnp.int32(1), 15 - s)
            return jnp.where(count16(src_ref, cand, False) >= kth, cand, t)

        return lax.fori_loop(0, 16, step, jnp.full((1, tq), -HALF16, jnp.int32))

    t_hi = bisect16(hi_ref, topk)
    above = count16(hi_ref, t_hi, True)
    t_hi16 = rows16(t_hi)

    def keep_threshold_bucket(kb, _):
        for r in range(slabs):
            rows = slice(r * PACK16, (r + 1) * PACK16)
            lo_ref[kb, rows, :] = jnp.where(hi_ref[kb, rows, :] == t_hi16, lo_ref[kb, rows, :],
                                            jnp.int16(-HALF16))
        return 0

    lax.fori_loop(0, nkb, keep_threshold_bucket, 0)
    t_lo = bisect16(lo_ref, topk - above)
    thr = t_hi * (2 * HALF16) + (t_lo + HALF16)

    def mask_block(kb, _):
        vis = kpos0 + kb * tk <= qpos
        madd_ref[kb] = jnp.where(vis, jnp.where(key_ref[kb] >= thr, 0.0, NEG_BIG), NEG_BIG)
        return 0

    lax.fori_loop(0, nkb, mask_block, 0)

    @pl.when(i == 0)
    def _():
        def transpose_values(kb, _):
            vt_ref[kb] = v_ref[pl.ds(pl.multiple_of(kb * tk, tk), tk), :].T
            return 0

        lax.fori_loop(0, vt_ref.shape[0], transpose_values, 0)

    m_ref[...] = jnp.full(m_ref.shape, NEG_BIG, F32)
    l_ref[...] = jnp.zeros(l_ref.shape, F32)
    acc_ref[...] = jnp.zeros(acc_ref.shape, F32)

    def attend(kb, near):
        keys = pl.ds(pl.multiple_of(kb * tk, tk), tk)
        madd = madd_ref[kb]
        for h in range(DSA_HEADS):
            hs = slice(h * dh, (h + 1) * dh)
            s_ref[h] = lax.dot_general(k_ref[keys, hs], q_ref[:, hs], NT_DIMS,
                                       preferred_element_type=F32)
        for h in range(DSA_HEADS):
            hs = slice(h * dh, (h + 1) * dh)
            s = s_ref[h] * (scale * LOG2E)
            if near is not None:
                s = s + corr_ref[h, near]
            s = s + madd
            m_prev = m_ref[h, 0:1, :]
            m_next = jnp.maximum(m_prev, jnp.max(s, axis=0, keepdims=True))
            alpha = jnp.exp2(m_prev - m_next)
            p = jnp.exp2(s - m_next)
            l_next = alpha * l_ref[h, 0:1, :] + jnp.sum(p, axis=0, keepdims=True)
            pv = jnp.dot(vt_ref[kb, hs, :], p.astype(BF16),
                         preferred_element_type=F32)
            acc_ref[h] = alpha * acc_ref[h] + pv
            m_ref[h] = jnp.broadcast_to(m_next, (sub, tq))
            l_ref[h] = jnp.broadcast_to(l_next, (sub, tq))

    def far(kb, _):
        attend(kb, None)
        return 0

    lax.fori_loop(0, i - 1, far, 0)

    @pl.when(i >= 1)
    def _():
        attend(i - 1, 1)

    attend(i, 0)
    for h in range(DSA_HEADS):
        o_h = acc_ref[h] / l_ref[h, 0:1, :]
        o_ref[:, h * dh:(h + 1) * dh] = o_h.T.astype(o_ref.dtype)


def dsa(pa, pb, corr, tq):
    b, l, _ = pa.shape
    dh = LANES
    hd = DSA_HEADS * dh
    qb = (2 * GLA_HEADS * LANES + GLA_HEADS * 2 * LANES) // hd
    iwb = GLA_HEADS * 2 * LANES // LANES + 1
    ikb = (iwb + 1) * LANES // (2 * LANES)
    topk = min(TOPK_MAX, l // 4)
    assert l % tq == 0 and tq >= topk and tq >= REL_MAX_DIST
    nkb = l // tq
    assert nkb * tq // PACK16 < HALF16
    return pl.pallas_call(
        functools.partial(_dsa_kernel, topk=topk),
        grid=(b, l // tq),
        in_specs=[
            pl.BlockSpec((None, tq, hd), lambda bi, i: (bi, i, qb)),
            pl.BlockSpec((None, tq, IDX_HEADS * IDX_DH), lambda bi, i: (bi, i, qb + 3)),
            pl.BlockSpec((None, tq, LANES), lambda bi, i: (bi, i, iwb)),
            _resident((None, l, hd), lambda bi, i: (bi, 0, qb + 1)),
            _resident((None, l, hd), lambda bi, i: (bi, 0, qb + 2)),
            _resident((None, l, 2 * LANES), lambda bi, i: (bi, 0, ikb)),
            _resident((DSA_HEADS, 2, tq, tq), lambda bi, i: (0, 0, 0, 0)),
        ],
        out_specs=pl.BlockSpec((None, tq, hd), lambda bi, i: (bi, i, 0)),
        out_shape=jax.ShapeDtypeStruct((b, l, hd), BF16),
        scratch_shapes=[
            pltpu.VMEM((nkb, tq, tq), jnp.int32),
            pltpu.VMEM((nkb, tq, tq), jnp.int16),
            pltpu.VMEM((nkb, tq, tq), jnp.int16),
            pltpu.VMEM((nkb, tq, tq), F32),
            pltpu.VMEM((LANES, tq), F32),
            pltpu.VMEM((DSA_HEADS, 8, tq), F32),
            pltpu.VMEM((DSA_HEADS, 8, tq), F32),
            pltpu.VMEM((DSA_HEADS, dh, tq), F32),
            pltpu.VMEM((nkb, hd, tq), BF16),
            pltpu.VMEM((DSA_HEADS, tq, tq), F32),
        ],
        compiler_params=_params("parallel", "arbitrary"),
        name="dsa",
    )(pa, pa, pb, pa, pa, pb, corr)


def _xattn_kernel(q_ref, kv_ref, o_ref):
    d = q_ref.shape[1]
    dh = d // XA_HEADS
    scale = dh ** -0.5
    for h in range(XA_HEADS):
        hs = slice(h * dh, (h + 1) * dh)
        s = lax.dot_general(q_ref[:, hs], kv_ref[:, hs], NT_DIMS, preferred_element_type=F32) * scale
        e = jnp.exp(s - jnp.max(s, axis=-1, keepdims=True))
        o = jnp.dot(e.astype(BF16), kv_ref[:, d + h * dh:d + (h + 1) * dh], preferred_element_type=F32)
        o_ref[:, hs] = (o / jnp.sum(e, axis=-1, keepdims=True)).astype(o_ref.dtype)


def xattn(q, kv, tq):
    b, l, d = q.shape
    nm = kv.shape[1]
    return pl.pallas_call(
        _xattn_kernel,
        grid=(b, l // tq),
        in_specs=[
            pl.BlockSpec((None, tq, d), lambda bi, i: (bi, i, 0)),
            pl.BlockSpec((None, nm, 2 * d), lambda bi, i: (bi, 0, 0)),
        ],
        out_specs=pl.BlockSpec((None, tq, d), lambda bi, i: (bi, i, 0)),
        out_shape=jax.ShapeDtypeStruct((b, l, d), BF16),
        compiler_params=_params("parallel", "parallel"),
        name="xattn",
    )(q, kv)


def _rel_bucket(dist):
    n = jnp.maximum(dist, 0)
    max_exact = REL_BUCKETS // 2
    nf = jnp.maximum(n, 1).astype(F32)
    large = max_exact + (jnp.log(nf / max_exact) / math.log(REL_MAX_DIST / max_exact)
                         * (REL_BUCKETS - max_exact)).astype(jnp.int32)
    large = jnp.minimum(large, REL_BUCKETS - 1)
    return jnp.where(n < max_exact, n, large)


def _bias_correction_tiles(rel_table, tq):
    t = jnp.arange(tq)
    d0 = t[:, None] - t[None, :]
    buckets = jnp.stack([_rel_bucket(d0), _rel_bucket(d0 + tq)], axis=0)
    onehot = (buckets[..., None] == jnp.arange(REL_BUCKETS)).astype(F32)
    far = rel_table[_rel_bucket(jnp.int32(2 * tq))]
    return jnp.einsum("ntsk,kh->hnst", onehot, (rel_table - far) * LOG2E,
                      precision=lax.Precision.HIGHEST)


def _chunk_sum_matrices(tl):
    r = jnp.arange(tl)
    same = (r[:, None] // GLA_CHUNK) == (r[None, :] // GLA_CHUNK)
    incl = same & (r[None, :] <= r[:, None])
    after = same & (r[None, :] > r[:, None])
    return jnp.stack([incl, after]).astype(BF16)


def _split_in_proj(w):
    depth, d, _ = w.shape
    half = d // 2
    sizes = [GLA_HEADS * LANES, GLA_HEADS * LANES, half, GLA_RANK, half,
             half, half, half, IDX_HEADS * IDX_DH, IDX_DH, IDX_HEADS]
    offs = [0]
    for s in sizes:
        offs.append(offs[-1] + s)
    assert offs[-1] == w.shape[2]
    seg = lambda n: w[:, :, offs[n]:offs[n + 1]]
    zero = lambda n: jnp.zeros((depth, d, n), w.dtype)
    ik = seg(9)
    w_mid = w[:, :, offs[5]:offs[9]]
    w_small = jnp.concatenate([seg(4), seg(3), zero(LANES - GLA_RANK), seg(10), zero(LANES - IDX_HEADS),
                               ik, zero(2 * IDX_DH), ik], axis=2)
    return offs[3], w_mid, w_small


def kernel(x, mem, norm_gains, w_in, w_gate_up, b_gate, gla_head_gain, rel_table,
           w_out, w_xq, w_xkv, w_xo, w_ffn_in, w_ffn_out):
    b, l, d = x.shape
    nm = mem.shape[1]
    depth = w_in.shape[0]
    d_ff = w_ffn_out.shape[1]
    m = b * l
    assert d // 2 // GLA_HEADS == 2 * LANES and d // 2 // DSA_HEADS == LANES

    tq_dsa = 256
    tl_gla = 512
    corr = _bias_correction_tiles(rel_table, tq_dsa)
    tri = _chunk_sum_matrices(tl_gla)
    w2p = jnp.pad(w_gate_up, ((0, 0), (0, LANES - GLA_RANK), (0, 0))).astype(BF16)

    w_in, w_out, w_xq, w_xkv, w_xo, w_ffn_in, w_ffn_out = (
        w.astype(BF16) for w in (w_in, w_out, w_xq, w_xkv, w_xo, w_ffn_in, w_ffn_out))
    n_head, w_mid, w_small = _split_in_proj(w_in)

    xf = x.reshape(m, d)
    memf = mem.reshape(b * nm, d)
    for li in range(depth):
        g = norm_gains[li]
        pa, pb = in_proj(xf, g[0], w_in, n_head, w_mid, w_small, li, bm=512, bn=1024)
        pa, pb = pa.reshape(b, l, -1), pb.reshape(b, l, -1)
        o_gla = gla(pa, pb, w2p[li], b_gate[li].reshape(1, -1),
                    gla_head_gain[li].reshape(1, -1), tri, tl_gla)
        o_dsa = dsa(pa, pb, corr, tq_dsa)
        xf = mm_norm_res([o_gla.reshape(m, -1), o_dsa.reshape(m, -1)], w_out, li, g[1], xf, bm=512)
        q = norm_matmul(xf, g[2], w_xq, li, BF16, bm=1024, bn=1024)
        kv = norm_matmul(memf, g[6], w_xkv, li, BF16, bm=min(512, b * nm), bn=1024)
        o_x = xattn(q.reshape(b, l, d), kv.reshape(b, nm, 2 * d), tq=512)
        xf = mm_norm_res([o_x.reshape(m, d)], w_xo, li, g[3], xf, bm=512)
        act = ffn_in(xf, g[4], w_ffn_in, li, d_ff, bm=1024, bn=512)
        xf = mm_norm_res([act], w_ffn_out, li, g[5], xf, bm=512, nk=4)
    return xf.reshape(b, l, d)
```

```python
import functools
import math

import jax
import jax.numpy as jnp
from jax import lax
from jax.experimental import pallas as pl
from jax.experimental.pallas import tpu as pltpu

F32 = jnp.float32
BF16 = jnp.bfloat16

EPS = 1e-6
GLA_HEADS = 4
GLA_RANK = 16
GLA_TAU = 16.0
GLA_CHUNK = 64
DSA_HEADS = 8
IDX_HEADS = 16
IDX_DH = 64
TOPK_MAX = 256
REL_BUCKETS = 32
REL_MAX_DIST = 128
XA_HEADS = 4

LANES = 128
VMEM_LIMIT_BYTES = 56 * 1024 * 1024

PACK16 = 16
ROW_CHUNK = 128

NEG_BIG = -1e30
HALF16 = 2 ** 15
LOG2E = 1.4426950408889634

NT_DIMS = (((1,), (1,)), ((), ()))
TN_DIMS = (((0,), (0,)), ((), ()))


def _params(*sem):
    return pltpu.CompilerParams(dimension_semantics=sem, vmem_limit_bytes=VMEM_LIMIT_BYTES)


def _resident(shape, index_map):
    return pl.BlockSpec(shape, index_map, pipeline_mode=pl.Buffered(1))


def _rms(x, g):
    return x * lax.rsqrt(jnp.mean(x * x, axis=-1, keepdims=True) + EPS) * g


def _rms_to(x_ref, g_ref, h_ref):
    g = g_ref[...]

    def body(c, _):
        rows = pl.ds(pl.multiple_of(c * ROW_CHUNK, ROW_CHUNK), ROW_CHUNK)
        h_ref[rows, :] = _rms(x_ref[rows, :], g).astype(h_ref.dtype)
        return 0

    lax.fori_loop(0, x_ref.shape[0] // ROW_CHUNK, body, 0)


def _norm_matmul_kernel(x_ref, g_ref, w_ref, o_ref, h_ref):
    @pl.when(pl.program_id(1) == 0)
    def _():
        _rms_to(x_ref, g_ref, h_ref)

    o_ref[...] = jnp.dot(h_ref[...], w_ref[...], preferred_element_type=F32).astype(o_ref.dtype)


def norm_matmul(x, g, w, layer, out_dtype, bm, bn):
    m, k = x.shape
    n = w.shape[2]
    assert m % bm == 0 and n % bn == 0
    return pl.pallas_call(
        _norm_matmul_kernel,
        grid=(m // bm, n // bn),
        in_specs=[
            pl.BlockSpec((bm, k), lambda i, j: (i, 0)),
            _resident((1, k), lambda i, j: (0, 0)),
            pl.BlockSpec((None, k, bn), lambda i, j: (layer, 0, j)),
        ],
        out_specs=pl.BlockSpec((bm, bn), lambda i, j: (i, j)),
        out_shape=jax.ShapeDtypeStruct((m, n), out_dtype),
        scratch_shapes=[pltpu.VMEM((bm, k), BF16)],
        compiler_params=_params("parallel", "arbitrary"),
        name="norm_matmul",
    )(x, g.reshape(1, k), w)


def _in_proj_kernel(x_ref, g_ref, wh_ref, wm_ref, ws_ref, oa_ref, ob_ref, h_ref, *, nh, nm):
    j = pl.program_id(1)

    @pl.when(j == 0)
    def _():
        _rms_to(x_ref, g_ref, h_ref)

    def project(w_ref, o_ref):
        o_ref[...] = jnp.dot(h_ref[...], w_ref[...], preferred_element_type=F32).astype(o_ref.dtype)

    pl.when(j < nh)(lambda: project(wh_ref, oa_ref))
    pl.when((j >= nh) & (j < nh + nm))(lambda: project(wm_ref, oa_ref))
    pl.when(j >= nh + nm)(lambda: project(ws_ref, ob_ref))


def in_proj(x, g, w_head, w_mid, w_small, layer, bm, bn, bs):
    m, k = x.shape
    n_head, n_mid, n_small = w_head.shape[2], w_mid.shape[2], w_small.shape[2]
    assert m % bm == 0 and n_head % bn == 0 and n_mid % bn == 0 and n_small % bs == 0
    nh, nm, ns = n_head // bn, n_mid // bn, n_small // bs
    na = nh + nm
    return pl.pallas_call(
        functools.partial(_in_proj_kernel, nh=nh, nm=nm),
        grid=(m // bm, na + ns),
        in_specs=[
            pl.BlockSpec((bm, k), lambda i, j: (i, 0)),
            _resident((1, k), lambda i, j: (0, 0)),
            pl.BlockSpec((None, k, bn), lambda i, j: (layer, 0, jnp.minimum(j, nh - 1))),
            pl.BlockSpec((None, k, bn), lambda i, j: (layer, 0, jnp.clip(j - nh, 0, nm - 1))),
            pl.BlockSpec((None, k, bs), lambda i, j: (layer, 0, jnp.clip(j - na, 0, ns - 1))),
        ],
        out_specs=[
            pl.BlockSpec((bm, bn), lambda i, j: (i, jnp.minimum(j, na - 1))),
            pl.BlockSpec((bm, bs), lambda i, j: (i, jnp.clip(j - na, 0, ns - 1))),
        ],
        out_shape=[
            jax.ShapeDtypeStruct((m, n_head + n_mid), BF16),
            jax.ShapeDtypeStruct((m, n_small), F32),
        ],
        scratch_shapes=[pltpu.VMEM((bm, k), BF16)],
        compiler_params=_params("parallel", "arbitrary"),
        name="in_proj",
    )(x, g.reshape(1, k), w_head, w_mid, w_small)


def _ffn_in_kernel(x_ref, g_ref, wg_ref, wu_ref, o_ref, h_ref):
    @pl.when(pl.program_id(1) == 0)
    def _():
        _rms_to(x_ref, g_ref, h_ref)

    h = h_ref[...]
    a = jnp.dot(h, wg_ref[...], preferred_element_type=F32)
    b = jnp.dot(h, wu_ref[...], preferred_element_type=F32)
    o_ref[...] = (a * (1.0 / (1.0 + jnp.exp(-a))) * b).astype(o_ref.dtype)


def ffn_in(x, g, w, layer, d_ff, bm, bn):
    m, k = x.shape
    assert m % bm == 0 and d_ff % bn == 0
    nj = d_ff // bn
    return pl.pallas_call(
        _ffn_in_kernel,
        grid=(m // bm, nj),
        in_specs=[
            pl.BlockSpec((bm, k), lambda i, j: (i, 0)),
            _resident((1, k), lambda i, j: (0, 0)),
            pl.BlockSpec((None, k, bn), lambda i, j: (layer, 0, j)),
            pl.BlockSpec((None, k, bn), lambda i, j: (layer, 0, j + nj)),
        ],
        out_specs=pl.BlockSpec((bm, bn), lambda i, j: (i, j)),
        out_shape=jax.ShapeDtypeStruct((m, d_ff), BF16),
        scratch_shapes=[pltpu.VMEM((bm, k), BF16)],
        compiler_params=_params("parallel", "arbitrary"),
        name="ffn_in",
    )(x, g.reshape(1, k), w, w)


def _mm_norm_res_kernel(*refs, n_lhs, nk):
    lhs = refs[:n_lhs]
    w_ref, g_ref, res_ref, o_ref = refs[n_lhs:n_lhs + 4]
    if nk == 1:
        y = None
        off = 0
        for a in lhs:
            wd = a.shape[1]
            part = jnp.dot(a[...], w_ref[off:off + wd, :], preferred_element_type=F32)
            y = part if y is None else y + part
            off += wd
        o_ref[...] = res_ref[...] + _rms(y, g_ref[...])
    else:
        k = pl.program_id(1)
        part = jnp.dot(lhs[0][...], w_ref[...], preferred_element_type=F32)

        @pl.when(k == 0)
        def _():
            o_ref[...] = part

        @pl.when(k > 0)
        def _():
            o_ref[...] += part

        @pl.when(k == nk - 1)
        def _():
            g = g_ref[...]

            def finish(c, _):
                rows = pl.ds(pl.multiple_of(c * ROW_CHUNK, ROW_CHUNK), ROW_CHUNK)
                o_ref[rows, :] = res_ref[rows, :] + _rms(o_ref[rows, :], g)
                return 0

            lax.fori_loop(0, o_ref.shape[0] // ROW_CHUNK, finish, 0)


def mm_norm_res(lhs_list, w, layer, g, res, bm, nk=1):
    m, n = res.shape
    kk = w.shape[1]
    assert m % bm == 0 and kk % nk == 0
    bk = kk // nk
    n_lhs = len(lhs_list)
    if nk == 1:
        lhs_specs = [pl.BlockSpec((bm, a.shape[1]), lambda i, k: (i, 0)) for a in lhs_list]
        w_spec = _resident((None, kk, n), lambda i, k: (layer, 0, 0))
    else:
        assert n_lhs == 1
        lhs_specs = [pl.BlockSpec((bm, bk), lambda i, k: (i, k))]
        w_spec = pl.BlockSpec((None, bk, n), lambda i, k: (layer, k, 0))
    return pl.pallas_call(
        functools.partial(_mm_norm_res_kernel, n_lhs=n_lhs, nk=nk),
        grid=(m // bm, nk),
        in_specs=lhs_specs + [
            w_spec,
            _resident((1, n), lambda i, k: (0, 0)),
            pl.BlockSpec((bm, n), lambda i, k: (i, 0)),
        ],
        out_specs=pl.BlockSpec((bm, n), lambda i, k: (i, 0)),
        out_shape=jax.ShapeDtypeStruct((m, n), F32),
        compiler_params=_params("parallel", "arbitrary"),
        name="mm_norm_res",
    )(*lhs_list, w, g.reshape(1, n), res)


def _gla_kernel(q_ref, k_ref, v_ref, z_ref, r_ref, w2_ref, bg_ref, gh_ref, tri_ref,
                o_ref, st_ref, *, dk):
    tl = q_ref.shape[0]

    @pl.when(pl.program_id(2) == 0)
    def _():
        st_ref[...] = jnp.zeros_like(st_ref)

    pre = jnp.dot(z_ref[...].astype(BF16), w2_ref[...], preferred_element_type=F32) + bg_ref[...]
    logg = (jnp.minimum(pre, 0.0) - jnp.log1p(jnp.exp(-jnp.abs(pre)))) * (1.0 / GLA_TAU)
    hi = logg.astype(BF16)
    lo = (logg - hi.astype(F32)).astype(BF16)
    hl = jnp.concatenate([hi, lo], axis=1)
    incl = tri_ref[0]
    cs = jnp.dot(incl, hl, preferred_element_type=F32)
    sf = jnp.dot(tri_ref[1], hl, preferred_element_type=F32)
    bcum = cs[:, :dk] + cs[:, dk:]
    suf = sf[:, :dk] + sf[:, dk:]

    qf = q_ref[...].astype(F32) * dk ** -0.5
    kf = k_ref[...].astype(F32)
    q_dec = (qf * jnp.exp(bcum)).astype(BF16)
    k_intra = (kf * jnp.exp(-bcum)).astype(BF16)
    k_state = (kf * jnp.exp(suf)).astype(BF16)
    v = v_ref[...]

    a = lax.dot_general(q_dec, k_intra, NT_DIMS, preferred_element_type=F32)
    a = jnp.where(incl > 0, a, 0.0).astype(BF16)
    o_intra = jnp.dot(a, v, preferred_element_type=F32)

    st = st_ref[...]
    outs = []
    for c in range(tl // GLA_CHUNK):
        rows = slice(c * GLA_CHUNK, (c + 1) * GLA_CHUNK)
        o_inter = lax.dot_general(q_dec[rows], st.astype(BF16), NT_DIMS, preferred_element_type=F32)
        outs.append(o_intra[rows] + o_inter)
        last = (c + 1) * GLA_CHUNK - 1
        decay = jnp.exp(bcum[last:last + 1, :])
        upd = lax.dot_general(v[rows], k_state[rows], TN_DIMS, preferred_element_type=F32)
        st = decay * st + upd
    st_ref[...] = st
    o = jnp.concatenate(outs, axis=0)
    o = _rms(o, gh_ref[...])
    r = r_ref[...]
    o_ref[...] = (o * (r * (1.0 / (1.0 + jnp.exp(-r))))).astype(o_ref.dtype)


def gla(pa, pb, w2p, bgate, ghead, tri, tl):
    b, l, _ = pa.shape
    dk = LANES
    dv = 2 * dk
    hq = GLA_HEADS
    vb = 2 * GLA_HEADS * dk // dv
    zb = GLA_HEADS * dv // LANES
    return pl.pallas_call(
        functools.partial(_gla_kernel, dk=dk),
        grid=(b, GLA_HEADS, l // tl),
        in_specs=[
            pl.BlockSpec((None, tl, dk), lambda bi, h, t: (bi, t, h)),
            pl.BlockSpec((None, tl, dk), lambda bi, h, t: (bi, t, hq + h)),
            pl.BlockSpec((None, tl, dv), lambda bi, h, t: (bi, t, vb + h)),
            pl.BlockSpec((None, tl, LANES), lambda bi, h, t: (bi, t, zb)),
            pl.BlockSpec((None, tl, dv), lambda bi, h, t: (bi, t, h)),
            pl.BlockSpec((LANES, dk), lambda bi, h, t: (0, h)),
            pl.BlockSpec((1, dk), lambda bi, h, t: (0, h)),
            _resident((1, dv), lambda bi, h, t: (0, 0)),
            _resident((2, tl, tl), lambda bi, h, t: (0, 0, 0)),
        ],
        out_specs=pl.BlockSpec((None, tl, dv), lambda bi, h, t: (bi, t, h)),
        out_shape=jax.ShapeDtypeStruct((b, l, GLA_HEADS * dv), BF16),
        scratch_shapes=[pltpu.VMEM((dv, dk), F32)],
        compiler_params=_params("parallel", "parallel", "arbitrary"),
        name="gla",
    )(pa, pa, pa, pb, pb, w2p, bgate, ghead, tri)


def _dsa_kernel(q_ref, iq_ref, iw_ref, k_ref, v_ref, ik_ref, corr_ref, o_ref,
                key_ref, hi_ref, lo_ref, madd_ref, wt_ref, m_ref, l_ref, acc_ref, vt_ref, s_ref,
                *, topk):
    tq = q_ref.shape[0]
    tk = tq
    dh = LANES
    sub = 8
    i = pl.program_id(1)
    nkb = i + 1
    scale = dh ** -0.5

    wt_ref[...] = (iw_ref[...] * (IDX_HEADS ** -0.5 * IDX_DH ** -0.5)).T
    qpos = lax.broadcasted_iota(jnp.int32, (tk, tq), 1) + i * tq
    kpos0 = lax.broadcasted_iota(jnp.int32, (tk, tq), 0)

    def score_block(kb, _):
        ik = ik_ref[pl.ds(pl.multiple_of(kb * tk, tk), tk), :].astype(BF16)
        ik_even, ik_odd = ik[:, :LANES], ik[:, LANES:]
        acc = jnp.zeros((tk, tq), F32)
        for p in range(IDX_HEADS // 2):
            qp = iq_ref[:, p * LANES:(p + 1) * LANES]
            for par, ikx in ((0, ik_even), (1, ik_odd)):
                h = 2 * p + par
                d = lax.dot_general(ikx, qp, NT_DIMS, preferred_element_type=F32)
                acc = acc + jnp.maximum(d, 0.0) * wt_ref[h:h + 1, :]
        sc = jnp.where(kpos0 + kb * tk <= qpos, acc, -jnp.inf)
        bits = lax.bitcast_convert_type(sc, jnp.int32)
        key = bits ^ ((bits >> 31) & 0x7FFFFFFF)
        key_ref[kb] = key
        hi_ref[kb] = (key >> 16).astype(jnp.int16)
        lo_ref[kb] = ((key & 0xFFFF) - HALF16).astype(jnp.int16)
        return 0

    lax.fori_loop(0, nkb, score_block, 0)

    n_acc = 4
    slabs = tk // PACK16
    one16, zero16 = jnp.int16(1), jnp.int16(0)

    def rows16(x):
        return jnp.broadcast_to(x, (PACK16, tq)).astype(jnp.int16)

    def count16(src_ref, cand, strict):
        cand16 = rows16(cand)

        def body(kb, cnts):
            cnts = list(cnts)
            for r in range(slabs):
                blk = src_ref[kb, r * PACK16:(r + 1) * PACK16, :]
                hit = (blk > cand16) if strict else (blk >= cand16)
                cnts[r % n_acc] = cnts[r % n_acc] + jnp.where(hit, one16, zero16)
            return tuple(cnts)

        cnts = lax.fori_loop(0, nkb, body, (jnp.zeros((PACK16, tq), jnp.int16),) * n_acc)
        cnt = ((cnts[0] + cnts[1]) + (cnts[2] + cnts[3])).astype(jnp.int32)
        return jnp.sum(cnt, axis=0, keepdims=True)

    def bisect16(src_ref, kth):
        def step(s, t):
            cand = t + lax.shift_left(jnp.int32(1), 15 - s)
            return jnp.where(count16(src_ref, cand, False) >= kth, cand, t)

        return lax.fori_loop(0, 16, step, jnp.full((1, tq), -HALF16, jnp.int32))

    t_hi = bisect16(hi_ref, topk)
    above = count16(hi_ref, t_hi, True)
    t_hi16 = rows16(t_hi)

    def keep_threshold_bucket(kb, _):
        for r in range(slabs):
            rows = slice(r * PACK16, (r + 1) * PACK16)
            lo_ref[kb, rows, :] = jnp.where(hi_ref[kb, rows, :] == t_hi16, lo_ref[kb, rows, :],
                                            jnp.int16(-HALF16))
        return 0

    lax.fori_loop(0, nkb, keep_threshold_bucket, 0)
    t_lo = bisect16(lo_ref, topk - above)
    thr = t_hi * (2 * HALF16) + (t_lo + HALF16)

    def mask_block(kb, _):
        vis = kpos0 + kb * tk <= qpos
        madd_ref[kb] = jnp.where(vis, jnp.where(key_ref[kb] >= thr, 0.0, NEG_BIG), NEG_BIG)
        return 0

    lax.fori_loop(0, nkb, mask_block, 0)

    @pl.when(i == 0)
    def _():
        def transpose_values(kb, _):
            vt_ref[kb] = v_ref[pl.ds(pl.multiple_of(kb * tk, tk), tk), :].T
            return 0

        lax.fori_loop(0, vt_ref.shape[0], transpose_values, 0)

    m_ref[...] = jnp.full(m_ref.shape, NEG_BIG, F32)
    l_ref[...] = jnp.zeros(l_ref.shape, F32)
    acc_ref[...] = jnp.zeros(acc_ref.shape, F32)

    def attend(kb, near):
        keys = pl.ds(pl.multiple_of(kb * tk, tk), tk)
        madd = madd_ref[kb]
        for h in range(DSA_HEADS):
            hs = slice(h * dh, (h + 1) * dh)
            s_ref[h] = lax.dot_general(k_ref[keys, hs], q_ref[:, hs], NT_DIMS,
                                       preferred_element_type=F32)
        for h in range(DSA_HEADS):
            hs = slice(h * dh, (h + 1) * dh)
            s = s_ref[h] * (scale * LOG2E)
            if near is not None:
                s = s + corr_ref[h, near]
            s = s + madd
            m_prev = m_ref[h, 0:1, :]
            m_next = jnp.maximum(m_prev, jnp.max(s, axis=0, keepdims=True))
            alpha = jnp.exp2(m_prev - m_next)
            p = jnp.exp2(s - m_next)
            l_next = alpha * l_ref[h, 0:1, :] + jnp.sum(p, axis=0, keepdims=True)
            pv = jnp.dot(vt_ref[kb, hs, :], p.astype(BF16),
                         preferred_element_type=F32)
            acc_ref[h] = alpha * acc_ref[h] + pv
            m_ref[h] = jnp.broadcast_to(m_next, (sub, tq))
            l_ref[h] = jnp.broadcast_to(l_next, (sub, tq))

    def far(kb, _):
        attend(kb, None)
        return 0

    lax.fori_loop(0, i - 1, far, 0)

    @pl.when(i >= 1)
    def _():
        attend(i - 1, 1)

    attend(i, 0)
    for h in range(DSA_HEADS):
        o_h = acc_ref[h] / l_ref[h, 0:1, :]
        o_ref[:, h * dh:(h + 1) * dh] = o_h.T.astype(o_ref.dtype)


def dsa(pa, pb, corr, tq):
    b, l, _ = pa.shape
    dh = LANES
    hd = DSA_HEADS * dh
    qb = (2 * GLA_HEADS * LANES + GLA_HEADS * 2 * LANES) // hd
    iwb = GLA_HEADS * 2 * LANES // LANES + 1
    ikb = (iwb + 1) * LANES // (2 * LANES)
    topk = min(TOPK_MAX, l // 4)
    assert l % tq == 0 and tq >= topk and tq >= REL_MAX_DIST
    nkb = l // tq
    assert nkb * tq // PACK16 < HALF16
    return pl.pallas_call(
        functools.partial(_dsa_kernel, topk=topk),
        grid=(b, l // tq),
        in_specs=[
            pl.BlockSpec((None, tq, hd), lambda bi, i: (bi, i, qb)),
            pl.BlockSpec((None, tq, IDX_HEADS * IDX_DH), lambda bi, i: (bi, i, qb + 3)),
            pl.BlockSpec((None, tq, LANES), lambda bi, i: (bi, i, iwb)),
            _resident((None, l, hd), lambda bi, i: (bi, 0, qb + 1)),
            _resident((None, l, hd), lambda bi, i: (bi, 0, qb + 2)),
            _resident((None, l, 2 * LANES), lambda bi, i: (bi, 0, ikb)),
            _resident((DSA_HEADS, 2, tq, tq), lambda bi, i: (0, 0, 0, 0)),
        ],
        out_specs=pl.BlockSpec((None, tq, hd), lambda bi, i: (bi, i, 0)),
        out_shape=jax.ShapeDtypeStruct((b, l, hd), BF16),
        scratch_shapes=[
            pltpu.VMEM((nkb, tq, tq), jnp.int32),
            pltpu.VMEM((nkb, tq, tq), jnp.int16),
            pltpu.VMEM((nkb, tq, tq), jnp.int16),
            pltpu.VMEM((nkb, tq, tq), F32),
            pltpu.VMEM((LANES, tq), F32),
            pltpu.VMEM((DSA_HEADS, 8, tq), F32),
            pltpu.VMEM((DSA_HEADS, 8, tq), F32),
            pltpu.VMEM((DSA_HEADS, dh, tq), F32),
            pltpu.VMEM((nkb, hd, tq), BF16),
            pltpu.VMEM((DSA_HEADS, tq, tq), F32),
        ],
        compiler_params=_params("parallel", "arbitrary"),
        name="dsa",
    )(pa, pa, pb, pa, pa, pb, corr)


def _xattn_kernel(q_ref, kv_ref, o_ref):
    d = q_ref.shape[1]
    dh = d // XA_HEADS
    scale = dh ** -0.5
    for h in range(XA_HEADS):
        hs = slice(h * dh, (h + 1) * dh)
        s = lax.dot_general(q_ref[:, hs], kv_ref[:, hs], NT_DIMS, preferred_element_type=F32) * scale
        e = jnp.exp(s - jnp.max(s, axis=-1, keepdims=True))
        o = jnp.dot(e.astype(BF16), kv_ref[:, d + h * dh:d + (h + 1) * dh], preferred_element_type=F32)
        o_ref[:, hs] = (o / jnp.sum(e, axis=-1, keepdims=True)).astype(o_ref.dtype)


def xattn(q, kv, tq):
    b, l, d = q.shape
    nm = kv.shape[1]
    return pl.pallas_call(
        _xattn_kernel,
        grid=(b, l // tq),
        in_specs=[
            pl.BlockSpec((None, tq, d), lambda bi, i: (bi, i, 0)),
            pl.BlockSpec((None, nm, 2 * d), lambda bi, i: (bi, 0, 0)),
        ],
        out_specs=pl.BlockSpec((None, tq, d), lambda bi, i: (bi, i, 0)),
        out_shape=jax.ShapeDtypeStruct((b, l, d), BF16),
        compiler_params=_params("parallel", "parallel"),
        name="xattn",
    )(q, kv)


def _rel_bucket(dist):
    n = jnp.maximum(dist, 0)
    max_exact = REL_BUCKETS // 2
    nf = jnp.maximum(n, 1).astype(F32)
    large = max_exact + (jnp.log(nf / max_exact) / math.log(REL_MAX_DIST / max_exact)
                         * (REL_BUCKETS - max_exact)).astype(jnp.int32)
    large = jnp.minimum(large, REL_BUCKETS - 1)
    return jnp.where(n < max_exact, n, large)


def _bias_correction_tiles(rel_table, tq):
    t = jnp.arange(tq)
    d0 = t[:, None] - t[None, :]
    buckets = jnp.stack([_rel_bucket(d0), _rel_bucket(d0 + tq)], axis=0)
    onehot = (buckets[..., None] == jnp.arange(REL_BUCKETS)).astype(F32)
    far = rel_table[_rel_bucket(jnp.int32(2 * tq))]
    return jnp.einsum("ntsk,kh->hnst", onehot, (rel_table - far) * LOG2E,
                      precision=lax.Precision.HIGHEST)


def _chunk_sum_matrices(tl):
    r = jnp.arange(tl)
    same = (r[:, None] // GLA_CHUNK) == (r[None, :] // GLA_CHUNK)
    incl = same & (r[None, :] <= r[:, None])
    after = same & (r[None, :] > r[:, None])
    return jnp.stack([incl, after]).astype(BF16)


def _split_in_proj(w):
    depth, d, _ = w.shape
    half = d // 2
    sizes = [GLA_HEADS * LANES, GLA_HEADS * LANES, half, GLA_RANK, half,
             half, half, half, IDX_HEADS * IDX_DH, IDX_DH, IDX_HEADS]
    offs = [0]
    for s in sizes:
        offs.append(offs[-1] + s)
    assert offs[-1] == w.shape[2]
    seg = lambda n: w[:, :, offs[n]:offs[n + 1]]
    zero = lambda n: jnp.zeros((depth, d, n), w.dtype)
    ik = seg(9)
    w_head = w[:, :, :offs[3]]
    w_mid = w[:, :, offs[5]:offs[9]]
    w_small = jnp.concatenate([seg(4), seg(3), zero(LANES - GLA_RANK), seg(10), zero(LANES - IDX_HEADS),
                               ik, zero(2 * IDX_DH), ik], axis=2)
    return w_head.astype(BF16), w_mid.astype(BF16), w_small.astype(BF16)


def kernel(x, mem, norm_gains, w_in, w_gate_up, b_gate, gla_head_gain, rel_table,
           w_out, w_xq, w_xkv, w_xo, w_ffn_in, w_ffn_out):
    b, l, d = x.shape
    nm = mem.shape[1]
    depth = w_in.shape[0]
    d_ff = w_ffn_out.shape[1]
    m = b * l
    assert d // 2 // GLA_HEADS == 2 * LANES and d // 2 // DSA_HEADS == LANES

    tq_dsa = 256
    tl_gla = 512
    corr = _bias_correction_tiles(rel_table, tq_dsa)
    tri = _chunk_sum_matrices(tl_gla)
    w2p = jnp.pad(w_gate_up, ((0, 0), (0, LANES - GLA_RANK), (0, 0))).astype(BF16)

    w_out, w_xq, w_xkv, w_xo, w_ffn_in, w_ffn_out = (
        w.astype(BF16) for w in (w_out, w_xq, w_xkv, w_xo, w_ffn_in, w_ffn_out))
    w_head, w_mid, w_small = _split_in_proj(w_in)

    xf = x.reshape(m, d)
    memf = mem.reshape(b * nm, d)
    for li in range(depth):
        g = norm_gains[li]
        pa, pb = in_proj(xf, g[0], w_head, w_mid, w_small, li, bm=1024, bn=1024, bs=512)
        pa, pb = pa.reshape(b, l, -1), pb.reshape(b, l, -1)
        o_gla = gla(pa, pb, w2p[li], b_gate[li].reshape(1, -1),
                    gla_head_gain[li].reshape(1, -1), tri, tl_gla)
        o_dsa = dsa(pa, pb, corr, tq_dsa)
        xf = mm_norm_res([o_gla.reshape(m, -1), o_dsa.reshape(m, -1)], w_out, li, g[1], xf, bm=512)
        q = norm_matmul(xf, g[2], w_xq, li, BF16, bm=1024, bn=1024)
        kv = norm_matmul(memf, g[6], w_xkv, li, BF16, bm=min(512, b * nm), bn=1024)
        o_x = xattn(q.reshape(b, l, d), kv.reshape(b, nm, 2 * d), tq=512)
        xf = mm_norm_res([o_x.reshape(m, d)], w_xo, li, g[3], xf, bm=512)
        act = ffn_in(xf, g[4], w_ffn_in, li, d_ff, bm=1024, bn=512)
        xf = mm_norm_res([act], w_ffn_out, li, g[5], xf, bm=1024, nk=11)
    return xf.reshape(b, l, d)
```

```python
import functools
import math

import jax
import jax.numpy as jnp
from jax import lax
from jax.experimental import pallas as pl
from jax.experimental.pallas import tpu as pltpu

F32 = jnp.float32
BF16 = jnp.bfloat16

EPS = 1e-6
GLA_HEADS = 4
GLA_RANK = 16
GLA_TAU = 16.0
GLA_CHUNK = 64
DSA_HEADS = 8
IDX_HEADS = 16
IDX_DH = 64
TOPK_MAX = 256
REL_BUCKETS = 32
REL_MAX_DIST = 128
XA_HEADS = 4

LANES = 128
VMEM_LIMIT_BYTES = 60000 * 1024

PACK16 = 16
ROW_CHUNK = 128

NEG_BIG = -1e30
HALF16 = 2 ** 15
LOG2E = 1.4426950408889634

NT_DIMS = (((1,), (1,)), ((), ()))
TN_DIMS = (((0,), (0,)), ((), ()))


def _params(*sem):
    return pltpu.CompilerParams(dimension_semantics=sem, vmem_limit_bytes=VMEM_LIMIT_BYTES)


def _resident(shape, index_map):
    return pl.BlockSpec(shape, index_map, pipeline_mode=pl.Buffered(1))


def _rms(x, g):
    return x * lax.rsqrt(jnp.mean(x * x, axis=-1, keepdims=True) + EPS) * g


def _rms_to(x_ref, g_ref, h_ref):
    g = g_ref[...]

    def body(c, _):
        rows = pl.ds(pl.multiple_of(c * ROW_CHUNK, ROW_CHUNK), ROW_CHUNK)
        h_ref[rows, :] = _rms(x_ref[rows, :], g).astype(h_ref.dtype)
        return 0

    lax.fori_loop(0, x_ref.shape[0] // ROW_CHUNK, body, 0)


def _norm_matmul_kernel(x_ref, g_ref, w_ref, o_ref, h_ref):
    @pl.when(pl.program_id(1) == 0)
    def _():
        _rms_to(x_ref, g_ref, h_ref)

    o_ref[...] = jnp.dot(h_ref[...], w_ref[...], preferred_element_type=F32).astype(o_ref.dtype)


def norm_matmul(x, g, w, layer, out_dtype, bm, bn):
    m, k = x.shape
    n = w.shape[2]
    assert m % bm == 0 and n % bn == 0
    return pl.pallas_call(
        _norm_matmul_kernel,
        grid=(m // bm, n // bn),
        in_specs=[
            pl.BlockSpec((bm, k), lambda i, j: (i, 0)),
            _resident((1, k), lambda i, j: (0, 0)),
            pl.BlockSpec((None, k, bn), lambda i, j: (layer, 0, j)),
        ],
        out_specs=pl.BlockSpec((bm, bn), lambda i, j: (i, j)),
        out_shape=jax.ShapeDtypeStruct((m, n), out_dtype),
        scratch_shapes=[pltpu.VMEM((bm, k), BF16)],
        compiler_params=_params("parallel", "arbitrary"),
        name="norm_matmul",
    )(x, g.reshape(1, k), w)


def _in_proj_kernel(x_ref, g_ref, wh_ref, wm_ref, ws_ref, oa_ref, ob_ref, h_ref, *, nh, nm):
    j = pl.program_id(1)

    @pl.when(j == 0)
    def _():
        _rms_to(x_ref, g_ref, h_ref)

    def project(w_ref, o_ref):
        o_ref[...] = jnp.dot(h_ref[...], w_ref[...], preferred_element_type=F32).astype(o_ref.dtype)

    pl.when(j < nh)(lambda: project(wh_ref, oa_ref))
    pl.when((j >= nh) & (j < nh + nm))(lambda: project(wm_ref, oa_ref))
    pl.when(j >= nh + nm)(lambda: project(ws_ref, ob_ref))


def in_proj(x, g, w_head, w_mid, w_small, layer, bm, bn, bs):
    m, k = x.shape
    n_head, n_mid, n_small = w_head.shape[2], w_mid.shape[2], w_small.shape[2]
    assert m % bm == 0 and n_head % bn == 0 and n_mid % bn == 0 and n_small % bs == 0
    nh, nm, ns = n_head // bn, n_mid // bn, n_small // bs
    na = nh + nm
    return pl.pallas_call(
        functools.partial(_in_proj_kernel, nh=nh, nm=nm),
        grid=(m // bm, na + ns),
        in_specs=[
            pl.BlockSpec((bm, k), lambda i, j: (i, 0)),
            _resident((1, k), lambda i, j: (0, 0)),
            pl.BlockSpec((None, k, bn), lambda i, j: (layer, 0, jnp.minimum(j, nh - 1))),
            pl.BlockSpec((None, k, bn), lambda i, j: (layer, 0, jnp.clip(j - nh, 0, nm - 1))),
            pl.BlockSpec((None, k, bs), lambda i, j: (layer, 0, jnp.clip(j - na, 0, ns - 1))),
        ],
        out_specs=[
            pl.BlockSpec((bm, bn), lambda i, j: (i, jnp.minimum(j, na - 1))),
            pl.BlockSpec((bm, bs), lambda i, j: (i, jnp.clip(j - na, 0, ns - 1))),
        ],
        out_shape=[
            jax.ShapeDtypeStruct((m, n_head + n_mid), BF16),
            jax.ShapeDtypeStruct((m, n_small), F32),
        ],
        scratch_shapes=[pltpu.VMEM((bm, k), BF16)],
        compiler_params=_params("parallel", "arbitrary"),
        name="in_proj",
    )(x, g.reshape(1, k), w_head, w_mid, w_small)


def _ffn_in_kernel(x_ref, g_ref, wg_ref, wu_ref, o_ref, h_ref):
    @pl.when(pl.program_id(1) == 0)
    def _():
        _rms_to(x_ref, g_ref, h_ref)

    h = h_ref[...]
    a = jnp.dot(h, wg_ref[...], preferred_element_type=F32)
    b = jnp.dot(h, wu_ref[...], preferred_element_type=F32)
    o_ref[...] = (a * (1.0 / (1.0 + jnp.exp(-a))) * b).astype(o_ref.dtype)


def ffn_in(x, g, w, layer, d_ff, bm, bn):
    m, k = x.shape
    assert m % bm == 0 and d_ff % bn == 0
    nj = d_ff // bn
    return pl.pallas_call(
        _ffn_in_kernel,
        grid=(m // bm, nj),
        in_specs=[
            pl.BlockSpec((bm, k), lambda i, j: (i, 0)),
            _resident((1, k), lambda i, j: (0, 0)),
            pl.BlockSpec((None, k, bn), lambda i, j: (layer, 0, j)),
            pl.BlockSpec((None, k, bn), lambda i, j: (layer, 0, j + nj)),
        ],
        out_specs=pl.BlockSpec((bm, bn), lambda i, j: (i, j)),
        out_shape=jax.ShapeDtypeStruct((m, d_ff), BF16),
        scratch_shapes=[pltpu.VMEM((bm, k), BF16)],
        compiler_params=_params("parallel", "arbitrary"),
        name="ffn_in",
    )(x, g.reshape(1, k), w, w)


def _mm_norm_res_kernel(*refs, n_lhs, nk):
    lhs = refs[:n_lhs]
    w_ref, g_ref, res_ref, o_ref = refs[n_lhs:n_lhs + 4]
    if nk == 1:
        y = None
        off = 0
        for a in lhs:
            wd = a.shape[1]
            part = jnp.dot(a[...], w_ref[off:off + wd, :], preferred_element_type=F32)
            y = part if y is None else y + part
            off += wd
        o_ref[...] = res_ref[...] + _rms(y, g_ref[...])
    else:
        k = pl.program_id(1)
        part = jnp.dot(lhs[0][...], w_ref[...], preferred_element_type=F32)

        @pl.when(k == 0)
        def _():
            o_ref[...] = part

        @pl.when(k > 0)
        def _():
            o_ref[...] += part

        @pl.when(k == nk - 1)
        def _():
            g = g_ref[...]

            def finish(c, _):
                rows = pl.ds(pl.multiple_of(c * ROW_CHUNK, ROW_CHUNK), ROW_CHUNK)
                o_ref[rows, :] = res_ref[rows, :] + _rms(o_ref[rows, :], g)
                return 0

            lax.fori_loop(0, o_ref.shape[0] // ROW_CHUNK, finish, 0)


def mm_norm_res(lhs_list, w, layer, g, res, bm, nk=1):
    m, n = res.shape
    kk = w.shape[1]
    assert m % bm == 0 and kk % nk == 0
    bk = kk // nk
    n_lhs = len(lhs_list)
    if nk == 1:
        lhs_specs = [pl.BlockSpec((bm, a.shape[1]), lambda i, k: (i, 0)) for a in lhs_list]
        w_spec = _resident((None, kk, n), lambda i, k: (layer, 0, 0))
    else:
        assert n_lhs == 1
        lhs_specs = [pl.BlockSpec((bm, bk), lambda i, k: (i, k))]
        w_spec = pl.BlockSpec((None, bk, n), lambda i, k: (layer, k, 0))
    return pl.pallas_call(
        functools.partial(_mm_norm_res_kernel, n_lhs=n_lhs, nk=nk),
        grid=(m // bm, nk),
        in_specs=lhs_specs + [
            w_spec,
            _resident((1, n), lambda i, k: (0, 0)),
            pl.BlockSpec((bm, n), lambda i, k: (i, 0)),
        ],
        out_specs=pl.BlockSpec((bm, n), lambda i, k: (i, 0)),
        out_shape=jax.ShapeDtypeStruct((m, n), F32),
        compiler_params=_params("parallel", "arbitrary"),
        name="mm_norm_res",
    )(*lhs_list, w, g.reshape(1, n), res)


def _gla_kernel(q_ref, k_ref, v_ref, z_ref, r_ref, w2_ref, bg_ref, gh_ref, tri_ref,
                o_ref, st_ref, *, dk):
    tl = q_ref.shape[0]

    @pl.when(pl.program_id(2) == 0)
    def _():
        st_ref[...] = jnp.zeros_like(st_ref)

    pre = jnp.dot(z_ref[...].astype(BF16), w2_ref[...], preferred_element_type=F32) + bg_ref[...]
    logg = (jnp.minimum(pre, 0.0) - jnp.log1p(jnp.exp(-jnp.abs(pre)))) * (1.0 / GLA_TAU)
    hi = logg.astype(BF16)
    lo = (logg - hi.astype(F32)).astype(BF16)
    incl = tri_ref[...]
    cs = jnp.dot(incl, jnp.concatenate([hi, lo], axis=1), preferred_element_type=F32)
    bcum = cs[:, :dk] + cs[:, dk:]
    n_chunks = tl // GLA_CHUNK
    b_last = [bcum[(c + 1) * GLA_CHUNK - 1:(c + 1) * GLA_CHUNK, :] for c in range(n_chunks)]
    suf = jnp.concatenate([jnp.broadcast_to(bl, (GLA_CHUNK, dk)) for bl in b_last], axis=0) - bcum

    qf = q_ref[...].astype(F32) * dk ** -0.5
    kf = k_ref[...].astype(F32)
    q_dec = (qf * jnp.exp(bcum)).astype(BF16)
    k_intra = (kf * jnp.exp(-bcum)).astype(BF16)
    k_state = (kf * jnp.exp(suf)).astype(BF16)
    v = v_ref[...]

    a = lax.dot_general(q_dec, k_intra, NT_DIMS, preferred_element_type=F32)
    a = jnp.where(incl > 0, a, 0.0).astype(BF16)
    o_intra = jnp.dot(a, v, preferred_element_type=F32)

    st = st_ref[...]
    outs = []
    for c in range(n_chunks):
        rows = slice(c * GLA_CHUNK, (c + 1) * GLA_CHUNK)
        o_inter = lax.dot_general(q_dec[rows], st.astype(BF16), NT_DIMS, preferred_element_type=F32)
        outs.append(o_intra[rows] + o_inter)
        decay = jnp.exp(b_last[c])
        upd = lax.dot_general(v[rows], k_state[rows], TN_DIMS, preferred_element_type=F32)
        st = decay * st + upd
    st_ref[...] = st
    o = jnp.concatenate(outs, axis=0)
    o = _rms(o, gh_ref[...])
    r = r_ref[...]
    o_ref[...] = (o * (r * (1.0 / (1.0 + jnp.exp(-r))))).astype(o_ref.dtype)


def gla(pa, pb, w2p, bgate, ghead, tri, tl):
    b, l, _ = pa.shape
    dk = LANES
    dv = 2 * dk
    hq = GLA_HEADS
    vb = 2 * GLA_HEADS * dk // dv
    zb = GLA_HEADS * dv // LANES
    return pl.pallas_call(
        functools.partial(_gla_kernel, dk=dk),
        grid=(b, GLA_HEADS, l // tl),
        in_specs=[
            pl.BlockSpec((None, tl, dk), lambda bi, h, t: (bi, t, h)),
            pl.BlockSpec((None, tl, dk), lambda bi, h, t: (bi, t, hq + h)),
            pl.BlockSpec((None, tl, dv), lambda bi, h, t: (bi, t, vb + h)),
            pl.BlockSpec((None, tl, LANES), lambda bi, h, t: (bi, t, zb)),
            pl.BlockSpec((None, tl, dv), lambda bi, h, t: (bi, t, h)),
            pl.BlockSpec((LANES, dk), lambda bi, h, t: (0, h)),
            pl.BlockSpec((1, dk), lambda bi, h, t: (0, h)),
            _resident((1, dv), lambda bi, h, t: (0, 0)),
            _resident((tl, tl), lambda bi, h, t: (0, 0)),
        ],
        out_specs=pl.BlockSpec((None, tl, dv), lambda bi, h, t: (bi, t, h)),
        out_shape=jax.ShapeDtypeStruct((b, l, GLA_HEADS * dv), BF16),
        scratch_shapes=[pltpu.VMEM((dv, dk), F32)],
        compiler_params=_params("parallel", "parallel", "arbitrary"),
        name="gla",
    )(pa, pa, pa, pb, pb, w2p, bgate, ghead, tri)


def _dsa_kernel(q_ref, iq_ref, iw_ref, k_ref, v_ref, ik_ref, corr_ref, o_ref,
                key_ref, hi_ref, lo_ref, madd_ref, wt_ref, m_ref, l_ref, acc_ref, vt_ref, s_ref,
                *, topk):
    tq = q_ref.shape[0]
    tk = tq
    dh = LANES
    sub = 8
    i = pl.program_id(1)
    nkb = i + 1
    scale = dh ** -0.5

    wt_ref[...] = (iw_ref[...] * (IDX_HEADS ** -0.5 * IDX_DH ** -0.5)).T
    qpos = lax.broadcasted_iota(jnp.int32, (tk, tq), 1) + i * tq
    kpos0 = lax.broadcasted_iota(jnp.int32, (tk, tq), 0)

    def score_block(kb, _):
        ik = ik_ref[pl.ds(pl.multiple_of(kb * tk, tk), tk), :].astype(BF16)
        ik_even, ik_odd = ik[:, :LANES], ik[:, LANES:]
        acc = jnp.zeros((tk, tq), F32)
        for p in range(IDX_HEADS // 2):
            qp = iq_ref[:, p * LANES:(p + 1) * LANES]
            for par, ikx in ((0, ik_even), (1, ik_odd)):
                h = 2 * p + par
                d = lax.dot_general(ikx, qp, NT_DIMS, preferred_element_type=F32)
                acc = acc + jnp.maximum(d, 0.0) * wt_ref[h:h + 1, :]
        sc = jnp.where(kpos0 + kb * tk <= qpos, acc, -jnp.inf)
        bits = lax.bitcast_convert_type(sc, jnp.int32)
        key = bits ^ ((bits >> 31) & 0x7FFFFFFF)
        key_ref[kb] = key
        hi_ref[kb] = (key >> 16).astype(jnp.int16)
        lo_ref[kb] = ((key & 0xFFFF) - HALF16).astype(jnp.int16)
        return 0

    lax.fori_loop(0, nkb, score_block, 0)

    n_acc = 4
    slabs = tk // PACK16
    one16, zero16 = jnp.int16(1), jnp.int16(0)

    def rows16(x):
        return jnp.broadcast_to(x, (PACK16, tq)).astype(jnp.int16)

    def count16(src_ref, cand, strict):
        cand16 = rows16(cand)

        def body(kb, cnts):
            cnts = list(cnts)
            for r in range(slabs):
                blk = src_ref[kb, r * PACK16:(r + 1) * PACK16, :]
                hit = (blk > cand16) if strict else (blk >= cand16)
                cnts[r % n_acc] = cnts[r % n_acc] + jnp.where(hit, one16, zero16)
            return tuple(cnts)

        cnts = lax.fori_loop(0, nkb, body, (jnp.zeros((PACK16, tq), jnp.int16),) * n_acc)
        cnt = ((cnts[0] + cnts[1]) + (cnts[2] + cnts[3])).astype(jnp.int32)
        return jnp.sum(cnt, axis=0, keepdims=True)

    def bisect16(src_ref, kth):
        def step(s, t):
            cand = t + lax.shift_left(jnp.int32(1), 15 - s)
            return jnp.where(count16(src_ref, cand, False) >= kth, cand, t)

        return lax.fori_loop(0, 16, step, jnp.full((1, tq), -HALF16, jnp.int32))

    t_hi = bisect16(hi_ref, topk)
    above = count16(hi_ref, t_hi, True)
    t_hi16 = rows16(t_hi)

    def keep_threshold_bucket(kb, _):
        for r in range(slabs):
            rows = slice(r * PACK16, (r + 1) * PACK16)
            lo_ref[kb, rows, :] = jnp.where(hi_ref[kb, rows, :] == t_hi16, lo_ref[kb, rows, :],
                                            jnp.int16(-HALF16))
        return 0

    lax.fori_loop(0, nkb, keep_threshold_bucket, 0)
    t_lo = bisect16(lo_ref, topk - above)
    thr = t_hi * (2 * HALF16) + (t_lo + HALF16)

    def mask_block(kb, _):
        vis = kpos0 + kb * tk <= qpos
        madd_ref[kb] = jnp.where(vis, jnp.where(key_ref[kb] >= thr, 0.0, NEG_BIG), NEG_BIG)
        return 0

    lax.fori_loop(0, nkb, mask_block, 0)

    @pl.when(i == 0)
    def _():
        def transpose_values(kb, _):
            vt_ref[kb] = v_ref[pl.ds(pl.multiple_of(kb * tk, tk), tk), :].T
            return 0

        lax.fori_loop(0, vt_ref.shape[0], transpose_values, 0)

    m_ref[...] = jnp.full(m_ref.shape, NEG_BIG, F32)
    l_ref[...] = jnp.zeros(l_ref.shape, F32)
    acc_ref[...] = jnp.zeros(acc_ref.shape, F32)

    def attend(kb, near):
        keys = pl.ds(pl.multiple_of(kb * tk, tk), tk)
        madd = madd_ref[kb]
        for h in range(DSA_HEADS):
            hs = slice(h * dh, (h + 1) * dh)
            s_ref[h] = lax.dot_general(k_ref[keys, hs], q_ref[:, hs], NT_DIMS,
                                       preferred_element_type=F32)
        for h in range(DSA_HEADS):
            hs = slice(h * dh, (h + 1) * dh)
            s = s_ref[h] * (scale * LOG2E)
            if near is not None:
                s = s + corr_ref[h, near]
            s = s + madd
            m_prev = m_ref[h, 0:1, :]
            m_next = jnp.maximum(m_prev, jnp.max(s, axis=0, keepdims=True))
            alpha = jnp.exp2(m_prev - m_next)
            p = jnp.exp2(s - m_next)
            l_next = alpha * l_ref[h, 0:1, :] + jnp.sum(p, axis=0, keepdims=True)
            pv = jnp.dot(vt_ref[kb, hs, :], p.astype(BF16),
                         preferred_element_type=F32)
            acc_ref[h] = alpha * acc_ref[h] + pv
            m_ref[h] = jnp.broadcast_to(m_next, (sub, tq))
            l_ref[h] = jnp.broadcast_to(l_next, (sub, tq))

    def far(kb, _):
        attend(kb, None)
        return 0

    lax.fori_loop(0, i - 1, far, 0)

    @pl.when(i >= 1)
    def _():
        attend(i - 1, 1)

    attend(i, 0)
    for h in range(DSA_HEADS):
        o_h = acc_ref[h] / l_ref[h, 0:1, :]
        o_ref[:, h * dh:(h + 1) * dh] = o_h.T.astype(o_ref.dtype)


def dsa(pa, pb, corr, tq):
    b, l, _ = pa.shape
    dh = LANES
    hd = DSA_HEADS * dh
    qb = (2 * GLA_HEADS * LANES + GLA_HEADS * 2 * LANES) // hd
    iwb = GLA_HEADS * 2 * LANES // LANES + 1
    ikb = (iwb + 1) * LANES // (2 * LANES)
    topk = min(TOPK_MAX, l // 4)
    assert l % tq == 0 and tq >= topk and tq >= REL_MAX_DIST
    nkb = l // tq
    assert nkb * tq // PACK16 < HALF16
    return pl.pallas_call(
        functools.partial(_dsa_kernel, topk=topk),
        grid=(b, l // tq),
        in_specs=[
            pl.BlockSpec((None, tq, hd), lambda bi, i: (bi, i, qb)),
            pl.BlockSpec((None, tq, IDX_HEADS * IDX_DH), lambda bi, i: (bi, i, qb + 3)),
            pl.BlockSpec((None, tq, LANES), lambda bi, i: (bi, i, iwb)),
            _resident((None, l, hd), lambda bi, i: (bi, 0, qb + 1)),
            _resident((None, l, hd), lambda bi, i: (bi, 0, qb + 2)),
            _resident((None, l, 2 * LANES), lambda bi, i: (bi, 0, ikb)),
            _resident((DSA_HEADS, 2, tq, tq), lambda bi, i: (0, 0, 0, 0)),
        ],
        out_specs=pl.BlockSpec((None, tq, hd), lambda bi, i: (bi, i, 0)),
        out_shape=jax.ShapeDtypeStruct((b, l, hd), BF16),
        scratch_shapes=[
            pltpu.VMEM((nkb, tq, tq), jnp.int32),
            pltpu.VMEM((nkb, tq, tq), jnp.int16),
            pltpu.VMEM((nkb, tq, tq), jnp.int16),
            pltpu.VMEM((nkb, tq, tq), F32),
            pltpu.VMEM((LANES, tq), F32),
            pltpu.VMEM((DSA_HEADS, 8, tq), F32),
            pltpu.VMEM((DSA_HEADS, 8, tq), F32),
            pltpu.VMEM((DSA_HEADS, dh, tq), F32),
            pltpu.VMEM((nkb, hd, tq), BF16),
            pltpu.VMEM((DSA_HEADS, tq, tq), F32),
        ],
        compiler_params=_params("parallel", "arbitrary"),
        name="dsa",
    )(pa, pa, pb, pa, pa, pb, corr)


def _xattn_kernel(x_ref, gq_ref, wq_ref, kv_ref, wo_ref, go_ref, o_ref, h_ref, q_ref, a_ref):
    d = x_ref.shape[1]
    dh = d // XA_HEADS
    scale = dh ** -0.5
    _rms_to(x_ref, gq_ref, h_ref)
    q_ref[...] = jnp.dot(h_ref[...], wq_ref[...], preferred_element_type=F32).astype(q_ref.dtype)
    for h in range(XA_HEADS):
        hs = slice(h * dh, (h + 1) * dh)
        s = lax.dot_general(q_ref[:, hs], kv_ref[:, hs], NT_DIMS, preferred_element_type=F32) * scale
        e = jnp.exp(s - jnp.max(s, axis=-1, keepdims=True))
        o = jnp.dot(e.astype(BF16), kv_ref[:, d + h * dh:d + (h + 1) * dh], preferred_element_type=F32)
        a_ref[:, hs] = (o / jnp.sum(e, axis=-1, keepdims=True)).astype(a_ref.dtype)
    y = jnp.dot(a_ref[...], wo_ref[...], preferred_element_type=F32)
    o_ref[...] = x_ref[...] + _rms(y, go_ref[...])


def xattn(x, kv, gq, wq, go, wo, layer, tq):
    b, l, d = x.shape
    nm = kv.shape[1]
    vec = lambda: _resident((1, d), lambda bi, i: (0, 0))
    mat = lambda: _resident((None, d, d), lambda bi, i: (layer, 0, 0))
    return pl.pallas_call(
        _xattn_kernel,
        grid=(b, l // tq),
        in_specs=[
            pl.BlockSpec((None, tq, d), lambda bi, i: (bi, i, 0)),
            vec(), mat(),
            pl.BlockSpec((None, nm, 2 * d), lambda bi, i: (bi, 0, 0)),
            mat(), vec(),
        ],
        out_specs=pl.BlockSpec((None, tq, d), lambda bi, i: (bi, i, 0)),
        out_shape=jax.ShapeDtypeStruct((b, l, d), F32),
        scratch_shapes=[pltpu.VMEM((tq, d), BF16)] * 3,
        compiler_params=_params("parallel", "parallel"),
        name="xattn",
    )(x, gq.reshape(1, d), wq, kv, wo, go.reshape(1, d))


def _rel_bucket(dist):
    n = jnp.maximum(dist, 0)
    max_exact = REL_BUCKETS // 2
    nf = jnp.maximum(n, 1).astype(F32)
    large = max_exact + (jnp.log(nf / max_exact) / math.log(REL_MAX_DIST / max_exact)
                         * (REL_BUCKETS - max_exact)).astype(jnp.int32)
    large = jnp.minimum(large, REL_BUCKETS - 1)
    return jnp.where(n < max_exact, n, large)


def _bias_correction_tiles(rel_table, tq):
    t = jnp.arange(tq)
    d0 = t[:, None] - t[None, :]
    buckets = jnp.stack([_rel_bucket(d0), _rel_bucket(d0 + tq)], axis=0)
    onehot = (buckets[..., None] == jnp.arange(REL_BUCKETS)).astype(F32)
    far = rel_table[_rel_bucket(jnp.int32(2 * tq))]
    return jnp.einsum("ntsk,kh->hnst", onehot, (rel_table - far) * LOG2E,
                      precision=lax.Precision.HIGHEST)


def _chunk_sum_matrices(tl):
    r = jnp.arange(tl)
    same = (r[:, None] // GLA_CHUNK) == (r[None, :] // GLA_CHUNK)
    return (same & (r[None, :] <= r[:, None])).astype(BF16)


def _split_in_proj(w):
    depth, d, _ = w.shape
    half = d // 2
    sizes = [GLA_HEADS * LANES, GLA_HEADS * LANES, half, GLA_RANK, half,
             half, half, half, IDX_HEADS * IDX_DH, IDX_DH, IDX_HEADS]
    offs = [0]
    for s in sizes:
        offs.append(offs[-1] + s)
    assert offs[-1] == w.shape[2]
    seg = lambda n: w[:, :, offs[n]:offs[n + 1]]
    zero = lambda n: jnp.zeros((depth, d, n), w.dtype)
    ik = seg(9)
    w_head = w[:, :, :offs[3]]
    w_mid = w[:, :, offs[5]:offs[9]]
    w_small = jnp.concatenate([seg(4), seg(3), zero(LANES - GLA_RANK), seg(10), zero(LANES - IDX_HEADS),
                               ik, zero(2 * IDX_DH), ik], axis=2)
    return w_head.astype(BF16), w_mid.astype(BF16), w_small.astype(BF16)


def kernel(x, mem, norm_gains, w_in, w_gate_up, b_gate, gla_head_gain, rel_table,
           w_out, w_xq, w_xkv, w_xo, w_ffn_in, w_ffn_out):
    b, l, d = x.shape
    nm = mem.shape[1]
    depth = w_in.shape[0]
    d_ff = w_ffn_out.shape[1]
    m = b * l
    assert d // 2 // GLA_HEADS == 2 * LANES and d // 2 // DSA_HEADS == LANES

    tq_dsa = 256
    tl_gla = 512
    corr = _bias_correction_tiles(rel_table, tq_dsa)
    tri = _chunk_sum_matrices(tl_gla)
    w2p = jnp.pad(w_gate_up, ((0, 0), (0, LANES - GLA_RANK), (0, 0))).astype(BF16)

    w_out, w_xq, w_xkv, w_xo, w_ffn_in, w_ffn_out = (
        w.astype(BF16) for w in (w_out, w_xq, w_xkv, w_xo, w_ffn_in, w_ffn_out))
    w_head, w_mid, w_small = _split_in_proj(w_in)

    xf = x.reshape(m, d)
    memf = mem.reshape(b * nm, d)
    for li in range(depth):
        g = norm_gains[li]
        pa, pb = in_proj(xf, g[0], w_head, w_mid, w_small, li, bm=1024, bn=1024, bs=512)
        pa, pb = pa.reshape(b, l, -1), pb.reshape(b, l, -1)
        o_gla = gla(pa, pb, w2p[li], b_gate[li].reshape(1, -1),
                    gla_head_gain[li].reshape(1, -1), tri, tl_gla)
        o_dsa = dsa(pa, pb, corr, tq_dsa)
        xf = mm_norm_res([o_gla.reshape(m, -1), o_dsa.reshape(m, -1)], w_out, li, g[1], xf, bm=512)
        kv = norm_matmul(memf, g[6], w_xkv, li, BF16, bm=min(512, b * nm), bn=1024)
        xf = xattn(xf.reshape(b, l, d), kv.reshape(b, nm, 2 * d), g[2], w_xq, g[3], w_xo, li,
                   tq=512).reshape(m, d)
        act = ffn_in(xf, g[4], w_ffn_in, li, d_ff, bm=1024, bn=512)
        xf = mm_norm_res([act], w_ffn_out, li, g[5], xf, bm=1024, nk=4)
    return xf.reshape(b, l, d)
```

```python
import functools
import math

import jax
import jax.numpy as jnp
from jax import lax
from jax.experimental import pallas as pl
from jax.experimental.pallas import tpu as pltpu

F32 = jnp.float32
BF16 = jnp.bfloat16

EPS = 1e-6
GLA_HEADS = 4
GLA_RANK = 16
GLA_TAU = 16.0
GLA_CHUNK = 64
DSA_HEADS = 8
IDX_HEADS = 16
IDX_DH = 64
TOPK_MAX = 256
REL_BUCKETS = 32
REL_MAX_DIST = 128
XA_HEADS = 4

LANES = 128
VMEM_LIMIT_BYTES = 60000 * 1024

PACK16 = 16
ROW_CHUNK = 128

NEG_BIG = -1e30
HALF16 = 2 ** 15
LOG2E = 1.4426950408889634

NT_DIMS = (((1,), (1,)), ((), ()))
TN_DIMS = (((0,), (0,)), ((), ()))


def _params(*sem):
    return pltpu.CompilerParams(dimension_semantics=sem, vmem_limit_bytes=VMEM_LIMIT_BYTES)


def _resident(shape, index_map):
    return pl.BlockSpec(shape, index_map, pipeline_mode=pl.Buffered(1))


def _rms(x, g):
    return x * lax.rsqrt(jnp.mean(x * x, axis=-1, keepdims=True) + EPS) * g


def _rms_to(x_ref, g_ref, h_ref):
    g = g_ref[...]

    def body(c, _):
        rows = pl.ds(pl.multiple_of(c * ROW_CHUNK, ROW_CHUNK), ROW_CHUNK)
        h_ref[rows, :] = _rms(x_ref[rows, :], g).astype(h_ref.dtype)
        return 0

    lax.fori_loop(0, x_ref.shape[0] // ROW_CHUNK, body, 0)


def _norm_matmul_kernel(x_ref, g_ref, w_ref, o_ref, h_ref):
    @pl.when(pl.program_id(1) == 0)
    def _():
        _rms_to(x_ref, g_ref, h_ref)

    o_ref[...] = jnp.dot(h_ref[...], w_ref[...], preferred_element_type=F32).astype(o_ref.dtype)


def norm_matmul(x, g, w, layer, out_dtype, bm, bn):
    m, k = x.shape
    n = w.shape[2]
    assert m % bm == 0 and n % bn == 0
    return pl.pallas_call(
        _norm_matmul_kernel,
        grid=(m // bm, n // bn),
        in_specs=[
            pl.BlockSpec((bm, k), lambda i, j: (i, 0)),
            _resident((1, k), lambda i, j: (0, 0)),
            pl.BlockSpec((None, k, bn), lambda i, j: (layer, 0, j)),
        ],
        out_specs=pl.BlockSpec((bm, bn), lambda i, j: (i, j)),
        out_shape=jax.ShapeDtypeStruct((m, n), out_dtype),
        scratch_shapes=[pltpu.VMEM((bm, k), BF16)],
        compiler_params=_params("parallel", "arbitrary"),
        name="norm_matmul",
    )(x, g.reshape(1, k), w)


def _in_proj_slot(i, j, n_slots):
    return jnp.where(i % 2 == 0, j, n_slots - 1 - j)


def _in_proj_kernel(x_ref, g_ref, wh_ref, wm_ref, ws_ref, oa_ref, ob_ref, h_ref, *, nh, ns, nm):
    s = _in_proj_slot(pl.program_id(0), pl.program_id(1), nh + ns + nm)

    @pl.when(pl.program_id(1) == 0)
    def _():
        _rms_to(x_ref, g_ref, h_ref)

    def project(w_ref, o_ref):
        o_ref[...] = jnp.dot(h_ref[...], w_ref[...], preferred_element_type=F32).astype(o_ref.dtype)

    pl.when(s < nh)(lambda: project(wh_ref, oa_ref))
    pl.when((s >= nh) & (s < nh + ns))(lambda: project(ws_ref, ob_ref))
    pl.when(s >= nh + ns)(lambda: project(wm_ref, oa_ref))


def in_proj(x, g, w_head, w_mid, w_small, layer, bm, bn, bs):
    m, k = x.shape
    n_head, n_mid, n_small = w_head.shape[2], w_mid.shape[2], w_small.shape[2]
    assert m % bm == 0 and n_head % bn == 0 and n_mid % bn == 0 and n_small % bs == 0
    nh, nm, ns = n_head // bn, n_mid // bn, n_small // bs
    n_slots = nh + ns + nm

    def slot(i, j):
        return _in_proj_slot(i, j, n_slots)

    def pa_block(i, j):
        s = slot(i, j)
        hold = jnp.where(i % 2 == 0, nh - 1, nh)
        return jnp.where(s < nh, s, jnp.where(s >= nh + ns, s - ns, hold))

    return pl.pallas_call(
        functools.partial(_in_proj_kernel, nh=nh, ns=ns, nm=nm),
        grid=(m // bm, n_slots),
        in_specs=[
            pl.BlockSpec((bm, k), lambda i, j: (i, 0)),
            _resident((1, k), lambda i, j: (0, 0)),
            pl.BlockSpec((None, k, bn), lambda i, j: (layer, 0, jnp.clip(slot(i, j), 0, nh - 1))),
            pl.BlockSpec((None, k, bn), lambda i, j: (layer, 0, jnp.clip(slot(i, j) - nh - ns, 0, nm - 1))),
            pl.BlockSpec((None, k, bs), lambda i, j: (layer, 0, jnp.clip(slot(i, j) - nh, 0, ns - 1))),
        ],
        out_specs=[
            pl.BlockSpec((bm, bn), lambda i, j: (i, pa_block(i, j))),
            pl.BlockSpec((bm, bs), lambda i, j: (i, jnp.clip(slot(i, j) - nh, 0, ns - 1))),
        ],
        out_shape=[
            jax.ShapeDtypeStruct((m, n_head + n_mid), BF16),
            jax.ShapeDtypeStruct((m, n_small), F32),
        ],
        scratch_shapes=[pltpu.VMEM((bm, k), BF16)],
        compiler_params=_params("parallel", "arbitrary"),
        name="in_proj",
    )(x, g.reshape(1, k), w_head, w_mid, w_small)


def _ffn_in_kernel(x_ref, g_ref, wg_ref, wu_ref, o_ref, h_ref):
    @pl.when(pl.program_id(1) == 0)
    def _():
        _rms_to(x_ref, g_ref, h_ref)

    h = h_ref[...]
    a = jnp.dot(h, wg_ref[...], preferred_element_type=F32)
    b = jnp.dot(h, wu_ref[...], preferred_element_type=F32)
    o_ref[...] = (a * (1.0 / (1.0 + jnp.exp(-a))) * b).astype(o_ref.dtype)


def ffn_in(x, g, w, layer, d_ff, bm, bn):
    m, k = x.shape
    assert m % bm == 0 and d_ff % bn == 0
    nj = d_ff // bn
    return pl.pallas_call(
        _ffn_in_kernel,
        grid=(m // bm, nj),
        in_specs=[
            pl.BlockSpec((bm, k), lambda i, j: (i, 0)),
            _resident((1, k), lambda i, j: (0, 0)),
            pl.BlockSpec((None, k, bn), lambda i, j: (layer, 0, j)),
            pl.BlockSpec((None, k, bn), lambda i, j: (layer, 0, j + nj)),
        ],
        out_specs=pl.BlockSpec((bm, bn), lambda i, j: (i, j)),
        out_shape=jax.ShapeDtypeStruct((m, d_ff), BF16),
        scratch_shapes=[pltpu.VMEM((bm, k), BF16)],
        compiler_params=_params("parallel", "arbitrary"),
        name="ffn_in",
    )(x, g.reshape(1, k), w, w)


def _mm_norm_res_kernel(*refs, n_lhs, nk):
    lhs = refs[:n_lhs]
    w_ref, g_ref, res_ref, o_ref = refs[n_lhs:n_lhs + 4]
    if nk == 1:
        y = None
        off = 0
        for a in lhs:
            wd = a.shape[1]
            part = jnp.dot(a[...], w_ref[off:off + wd, :], preferred_element_type=F32)
            y = part if y is None else y + part
            off += wd
        o_ref[...] = res_ref[...] + _rms(y, g_ref[...])
    else:
        k = pl.program_id(1)
        part = jnp.dot(lhs[0][...], w_ref[...], preferred_element_type=F32)

        @pl.when(k == 0)
        def _():
            o_ref[...] = part

        @pl.when(k > 0)
        def _():
            o_ref[...] += part

        @pl.when(k == nk - 1)
        def _():
            g = g_ref[...]

            def finish(c, _):
                rows = pl.ds(pl.multiple_of(c * ROW_CHUNK, ROW_CHUNK), ROW_CHUNK)
                o_ref[rows, :] = res_ref[rows, :] + _rms(o_ref[rows, :], g)
                return 0

            lax.fori_loop(0, o_ref.shape[0] // ROW_CHUNK, finish, 0)


def mm_norm_res(lhs_list, w, layer, g, res, bm, nk=1):
    m, n = res.shape
    kk = w.shape[1]
    assert m % bm == 0 and kk % nk == 0
    bk = kk // nk
    n_lhs = len(lhs_list)
    if nk == 1:
        lhs_specs = [pl.BlockSpec((bm, a.shape[1]), lambda i, k: (i, 0)) for a in lhs_list]
        w_spec = _resident((None, kk, n), lambda i, k: (layer, 0, 0))
    else:
        assert n_lhs == 1
        lhs_specs = [pl.BlockSpec((bm, bk), lambda i, k: (i, k))]
        w_spec = pl.BlockSpec((None, bk, n), lambda i, k: (layer, k, 0))
    return pl.pallas_call(
        functools.partial(_mm_norm_res_kernel, n_lhs=n_lhs, nk=nk),
        grid=(m // bm, nk),
        in_specs=lhs_specs + [
            w_spec,
            _resident((1, n), lambda i, k: (0, 0)),
            pl.BlockSpec((bm, n), lambda i, k: (i, 0)),
        ],
        out_specs=pl.BlockSpec((bm, n), lambda i, k: (i, 0)),
        out_shape=jax.ShapeDtypeStruct((m, n), F32),
        compiler_params=_params("parallel", "arbitrary"),
        name="mm_norm_res",
    )(*lhs_list, w, g.reshape(1, n), res)


def _gla_kernel(q_ref, k_ref, v_ref, z_ref, r_ref, w2_ref, bg_ref, gh_ref, tri_ref,
                o_ref, st_ref, *, dk):
    tl = q_ref.shape[0]

    @pl.when(pl.program_id(2) == 0)
    def _():
        st_ref[...] = jnp.zeros_like(st_ref)

    pre = jnp.dot(z_ref[...].astype(BF16), w2_ref[...], preferred_element_type=F32) + bg_ref[...]
    logg = (jnp.minimum(pre, 0.0) - jnp.log1p(jnp.exp(-jnp.abs(pre)))) * (1.0 / GLA_TAU)
    hi = logg.astype(BF16)
    lo = (logg - hi.astype(F32)).astype(BF16)
    incl = tri_ref[...]
    cs = jnp.dot(incl, jnp.concatenate([hi, lo], axis=1), preferred_element_type=F32)
    bcum = cs[:, :dk] + cs[:, dk:]
    n_chunks = tl // GLA_CHUNK
    b_last = [bcum[(c + 1) * GLA_CHUNK - 1:(c + 1) * GLA_CHUNK, :] for c in range(n_chunks)]
    suf = jnp.concatenate([jnp.broadcast_to(bl, (GLA_CHUNK, dk)) for bl in b_last], axis=0) - bcum

    qf = q_ref[...].astype(F32) * dk ** -0.5
    kf = k_ref[...].astype(F32)
    q_dec = (qf * jnp.exp(bcum)).astype(BF16)
    k_intra = (kf * jnp.exp(-bcum)).astype(BF16)
    k_state = (kf * jnp.exp(suf)).astype(BF16)
    v = v_ref[...]

    a = lax.dot_general(q_dec, k_intra, NT_DIMS, preferred_element_type=F32)
    a = jnp.where(incl > 0, a, 0.0).astype(BF16)
    o_intra = jnp.dot(a, v, preferred_element_type=F32)

    st = st_ref[...]
    outs = []
    for c in range(n_chunks):
        rows = slice(c * GLA_CHUNK, (c + 1) * GLA_CHUNK)
        o_inter = lax.dot_general(q_dec[rows], st.astype(BF16), NT_DIMS, preferred_element_type=F32)
        outs.append(o_intra[rows] + o_inter)
        decay = jnp.exp(b_last[c])
        upd = lax.dot_general(v[rows], k_state[rows], TN_DIMS, preferred_element_type=F32)
        st = decay * st + upd
    st_ref[...] = st
    o = jnp.concatenate(outs, axis=0)
    o = _rms(o, gh_ref[...])
    r = r_ref[...]
    o_ref[...] = (o * (r * (1.0 / (1.0 + jnp.exp(-r))))).astype(o_ref.dtype)


def gla(pa, pb, w2p, bgate, ghead, tri, tl):
    b, l, _ = pa.shape
    dk = LANES
    dv = 2 * dk
    hq = GLA_HEADS
    vb = 2 * GLA_HEADS * dk // dv
    zb = GLA_HEADS * dv // LANES
    return pl.pallas_call(
        functools.partial(_gla_kernel, dk=dk),
        grid=(b, GLA_HEADS, l // tl),
        in_specs=[
            pl.BlockSpec((None, tl, dk), lambda bi, h, t: (bi, t, h)),
            pl.BlockSpec((None, tl, dk), lambda bi, h, t: (bi, t, hq + h)),
            pl.BlockSpec((None, tl, dv), lambda bi, h, t: (bi, t, vb + h)),
            pl.BlockSpec((None, tl, LANES), lambda bi, h, t: (bi, t, zb)),
            pl.BlockSpec((None, tl, dv), lambda bi, h, t: (bi, t, h)),
            pl.BlockSpec((LANES, dk), lambda bi, h, t: (0, h)),
            pl.BlockSpec((1, dk), lambda bi, h, t: (0, h)),
            _resident((1, dv), lambda bi, h, t: (0, 0)),
            _resident((tl, tl), lambda bi, h, t: (0, 0)),
        ],
        out_specs=pl.BlockSpec((None, tl, dv), lambda bi, h, t: (bi, t, h)),
        out_shape=jax.ShapeDtypeStruct((b, l, GLA_HEADS * dv), BF16),
        scratch_shapes=[pltpu.VMEM((dv, dk), F32)],
        compiler_params=_params("parallel", "parallel", "arbitrary"),
        name="gla",
    )(pa, pa, pa, pb, pb, w2p, bgate, ghead, tri)


def _dsa_kernel(q_ref, iq_ref, iw_ref, k_ref, v_ref, ik_ref, corr_ref, o_ref,
                key_ref, hi_ref, lo_ref, madd_ref, wt_ref, m_ref, l_ref, acc_ref, vt_ref, s_ref,
                *, topk):
    tq = q_ref.shape[0]
    tk = tq
    dh = LANES
    sub = 8
    i = pl.program_id(1)
    nkb = i + 1
    scale = dh ** -0.5

    wt_ref[...] = (iw_ref[...] * (IDX_HEADS ** -0.5 * IDX_DH ** -0.5)).T
    qpos = lax.broadcasted_iota(jnp.int32, (tk, tq), 1) + i * tq
    kpos0 = lax.broadcasted_iota(jnp.int32, (tk, tq), 0)

    def score_block(kb, _):
        ik = ik_ref[pl.ds(pl.multiple_of(kb * tk, tk), tk), :].astype(BF16)
        ik_even, ik_odd = ik[:, :LANES], ik[:, LANES:]
        acc = jnp.zeros((tk, tq), F32)
        for p in range(IDX_HEADS // 2):
            qp = iq_ref[:, p * LANES:(p + 1) * LANES]
            for par, ikx in ((0, ik_even), (1, ik_odd)):
                h = 2 * p + par
                d = lax.dot_general(ikx, qp, NT_DIMS, preferred_element_type=F32)
                acc = acc + jnp.maximum(d, 0.0) * wt_ref[h:h + 1, :]
        sc = jnp.where(kpos0 + kb * tk <= qpos, acc, -jnp.inf)
        bits = lax.bitcast_convert_type(sc, jnp.int32)
        key = bits ^ ((bits >> 31) & 0x7FFFFFFF)
        key_ref[kb] = key
        hi_ref[kb] = (key >> 16).astype(jnp.int16)
        lo_ref[kb] = ((key & 0xFFFF) - HALF16).astype(jnp.int16)
        return 0

    lax.fori_loop(0, nkb, score_block, 0)

    n_acc = 4
    slabs = tk // PACK16
    one16, zero16 = jnp.int16(1), jnp.int16(0)

    def rows16(x):
        return jnp.broadcast_to(x, (PACK16, tq)).astype(jnp.int16)

    def count16(src_ref, cand, strict):
        cand16 = rows16(cand)

        def body(kb, cnts):
            cnts = list(cnts)
            for r in range(slabs):
                blk = src_ref[kb, r * PACK16:(r + 1) * PACK16, :]
                hit = (blk > cand16) if strict else (blk >= cand16)
                cnts[r % n_acc] = cnts[r % n_acc] + jnp.where(hit, one16, zero16)
            return tuple(cnts)

        cnts = lax.fori_loop(0, nkb, body, (jnp.zeros((PACK16, tq), jnp.int16),) * n_acc)
        cnt = ((cnts[0] + cnts[1]) + (cnts[2] + cnts[3])).astype(jnp.int32)
        return jnp.sum(cnt, axis=0, keepdims=True)

    def bisect16(src_ref, kth):
        def step(s, t):
            cand = t + lax.shift_left(jnp.int32(1), 15 - s)
            return jnp.where(count16(src_ref, cand, False) >= kth, cand, t)

        return lax.fori_loop(0, 16, step, jnp.full((1, tq), -HALF16, jnp.int32))

    t_hi = bisect16(hi_ref, topk)
    above = count16(hi_ref, t_hi, True)
    t_hi16 = rows16(t_hi)

    def keep_threshold_bucket(kb, _):
        for r in range(slabs):
            rows = slice(r * PACK16, (r + 1) * PACK16)
            lo_ref[kb, rows, :] = jnp.where(hi_ref[kb, rows, :] == t_hi16, lo_ref[kb, rows, :],
                                            jnp.int16(-HALF16))
        return 0

    lax.fori_loop(0, nkb, keep_threshold_bucket, 0)
    t_lo = bisect16(lo_ref, topk - above)
    thr = t_hi * (2 * HALF16) + (t_lo + HALF16)

    def mask_block(kb, _):
        vis = kpos0 + kb * tk <= qpos
        madd_ref[kb] = jnp.where(vis, jnp.where(key_ref[kb] >= thr, 0.0, NEG_BIG), NEG_BIG)
        return 0

    lax.fori_loop(0, nkb, mask_block, 0)

    @pl.when(i == 0)
    def _():
        def transpose_values(kb, _):
            vt_ref[kb] = v_ref[pl.ds(pl.multiple_of(kb * tk, tk), tk), :].T
            return 0

        lax.fori_loop(0, vt_ref.shape[0], transpose_values, 0)

    m_ref[...] = jnp.full(m_ref.shape, NEG_BIG, F32)
    l_ref[...] = jnp.zeros(l_ref.shape, F32)
    acc_ref[...] = jnp.zeros(acc_ref.shape, F32)

    def attend(kb, near):
        keys = pl.ds(pl.multiple_of(kb * tk, tk), tk)
        madd = madd_ref[kb]
        for h in range(DSA_HEADS):
            hs = slice(h * dh, (h + 1) * dh)
            s_ref[h] = lax.dot_general(k_ref[keys, hs], q_ref[:, hs], NT_DIMS,
                                       preferred_element_type=F32)
        for h in range(DSA_HEADS):
            hs = slice(h * dh, (h + 1) * dh)
            s = s_ref[h] * (scale * LOG2E)
            if near is not None:
                s = s + corr_ref[h, near]
            s = s + madd
            m_prev = m_ref[h, 0:1, :]
            m_next = jnp.maximum(m_prev, jnp.max(s, axis=0, keepdims=True))
            alpha = jnp.exp2(m_prev - m_next)
            p = jnp.exp2(s - m_next)
            l_next = alpha * l_ref[h, 0:1, :] + jnp.sum(p, axis=0, keepdims=True)
            pv = jnp.dot(vt_ref[kb, hs, :], p.astype(BF16),
                         preferred_element_type=F32)
            acc_ref[h] = alpha * acc_ref[h] + pv
            m_ref[h] = jnp.broadcast_to(m_next, (sub, tq))
            l_ref[h] = jnp.broadcast_to(l_next, (sub, tq))

    def far(kb, _):
        attend(kb, None)
        return 0

    lax.fori_loop(0, i - 1, far, 0)

    @pl.when(i >= 1)
    def _():
        attend(i - 1, 1)

    attend(i, 0)
    for h in range(DSA_HEADS):
        o_h = acc_ref[h] / l_ref[h, 0:1, :]
        o_ref[:, h * dh:(h + 1) * dh] = o_h.T.astype(o_ref.dtype)


def dsa(pa, pb, corr, tq):
    b, l, _ = pa.shape
    dh = LANES
    hd = DSA_HEADS * dh
    qb = (2 * GLA_HEADS * LANES + GLA_HEADS * 2 * LANES) // hd
    iwb = GLA_HEADS * 2 * LANES // LANES + 1
    ikb = (iwb + 1) * LANES // (2 * LANES)
    topk = min(TOPK_MAX, l // 4)
    assert l % tq == 0 and tq >= topk and tq >= REL_MAX_DIST
    nkb = l // tq
    assert nkb * tq // PACK16 < HALF16
    return pl.pallas_call(
        functools.partial(_dsa_kernel, topk=topk),
        grid=(b, l // tq),
        in_specs=[
            pl.BlockSpec((None, tq, hd), lambda bi, i: (bi, i, qb)),
            pl.BlockSpec((None, tq, IDX_HEADS * IDX_DH), lambda bi, i: (bi, i, qb + 3)),
            pl.BlockSpec((None, tq, LANES), lambda bi, i: (bi, i, iwb)),
            _resident((None, l, hd), lambda bi, i: (bi, 0, qb + 1)),
            _resident((None, l, hd), lambda bi, i: (bi, 0, qb + 2)),
            _resident((None, l, 2 * LANES), lambda bi, i: (bi, 0, ikb)),
            _resident((DSA_HEADS, 2, tq, tq), lambda bi, i: (0, 0, 0, 0)),
        ],
        out_specs=pl.BlockSpec((None, tq, hd), lambda bi, i: (bi, i, 0)),
        out_shape=jax.ShapeDtypeStruct((b, l, hd), BF16),
        scratch_shapes=[
            pltpu.VMEM((nkb, tq, tq), jnp.int32),
            pltpu.VMEM((nkb, tq, tq), jnp.int16),
            pltpu.VMEM((nkb, tq, tq), jnp.int16),
            pltpu.VMEM((nkb, tq, tq), F32),
            pltpu.VMEM((LANES, tq), F32),
            pltpu.VMEM((DSA_HEADS, 8, tq), F32),
            pltpu.VMEM((DSA_HEADS, 8, tq), F32),
            pltpu.VMEM((DSA_HEADS, dh, tq), F32),
            pltpu.VMEM((nkb, hd, tq), BF16),
            pltpu.VMEM((DSA_HEADS, tq, tq), F32),
        ],
        compiler_params=_params("parallel", "arbitrary"),
        name="dsa",
    )(pa, pa, pb, pa, pa, pb, corr)


def _xattn_kernel(x_ref, gq_ref, wq_ref, kv_ref, wo_ref, go_ref, o_ref, h_ref, q_ref, a_ref):
    d = x_ref.shape[1]
    dh = d // XA_HEADS
    scale = dh ** -0.5
    _rms_to(x_ref, gq_ref, h_ref)
    q_ref[...] = jnp.dot(h_ref[...], wq_ref[...], preferred_element_type=F32).astype(q_ref.dtype)
    for h in range(XA_HEADS):
        hs = slice(h * dh, (h + 1) * dh)
        s = lax.dot_general(q_ref[:, hs], kv_ref[:, hs], NT_DIMS, preferred_element_type=F32) * scale
        e = jnp.exp(s - jnp.max(s, axis=-1, keepdims=True))
        o = jnp.dot(e.astype(BF16), kv_ref[:, d + h * dh:d + (h + 1) * dh], preferred_element_type=F32)
        a_ref[:, hs] = (o / jnp.sum(e, axis=-1, keepdims=True)).astype(a_ref.dtype)
    y = jnp.dot(a_ref[...], wo_ref[...], preferred_element_type=F32)
    o_ref[...] = x_ref[...] + _rms(y, go_ref[...])


def xattn(x, kv, gq, wq, go, wo, layer, tq):
    b, l, d = x.shape
    nm = kv.shape[1]
    vec = lambda: _resident((1, d), lambda bi, i: (0, 0))
    mat = lambda: _resident((None, d, d), lambda bi, i: (layer, 0, 0))
    return pl.pallas_call(
        _xattn_kernel,
        grid=(b, l // tq),
        in_specs=[
            pl.BlockSpec((None, tq, d), lambda bi, i: (bi, i, 0)),
            vec(), mat(),
            pl.BlockSpec((None, nm, 2 * d), lambda bi, i: (bi, 0, 0)),
            mat(), vec(),
        ],
        out_specs=pl.BlockSpec((None, tq, d), lambda bi, i: (bi, i, 0)),
        out_shape=jax.ShapeDtypeStruct((b, l, d), F32),
        scratch_shapes=[pltpu.VMEM((tq, d), BF16)] * 3,
        compiler_params=_params("parallel", "parallel"),
        name="xattn",
    )(x, gq.reshape(1, d), wq, kv, wo, go.reshape(1, d))


def _rel_bucket(dist):
    n = jnp.maximum(dist, 0)
    max_exact = REL_BUCKETS // 2
    nf = jnp.maximum(n, 1).astype(F32)
    large = max_exact + (jnp.log(nf / max_exact) / math.log(REL_MAX_DIST / max_exact)
                         * (REL_BUCKETS - max_exact)).astype(jnp.int32)
    large = jnp.minimum(large, REL_BUCKETS - 1)
    return jnp.where(n < max_exact, n, large)


def _bias_correction_tiles(rel_table, tq):
    t = jnp.arange(tq)
    d0 = t[:, None] - t[None, :]
    buckets = jnp.stack([_rel_bucket(d0), _rel_bucket(d0 + tq)], axis=0)
    onehot = (buckets[..., None] == jnp.arange(REL_BUCKETS)).astype(F32)
    far = rel_table[_rel_bucket(jnp.int32(2 * tq))]
    return jnp.einsum("ntsk,kh->hnst", onehot, (rel_table - far) * LOG2E,
                      precision=lax.Precision.HIGHEST)


def _chunk_sum_matrices(tl):
    r = jnp.arange(tl)
    same = (r[:, None] // GLA_CHUNK) == (r[None, :] // GLA_CHUNK)
    return (same & (r[None, :] <= r[:, None])).astype(BF16)


def _split_in_proj(w):
    depth, d, _ = w.shape
    half = d // 2
    sizes = [GLA_HEADS * LANES, GLA_HEADS * LANES, half, GLA_RANK, half,
             half, half, half, IDX_HEADS * IDX_DH, IDX_DH, IDX_HEADS]
    offs = [0]
    for s in sizes:
        offs.append(offs[-1] + s)
    assert offs[-1] == w.shape[2]
    seg = lambda n: w[:, :, offs[n]:offs[n + 1]]
    zero = lambda n: jnp.zeros((depth, d, n), w.dtype)
    ik = seg(9)
    w_head = w[:, :, :offs[3]]
    w_mid = w[:, :, offs[5]:offs[9]]
    w_small = jnp.concatenate([seg(4), seg(3), zero(LANES - GLA_RANK), seg(10), zero(LANES - IDX_HEADS),
                               ik, zero(2 * IDX_DH), ik], axis=2)
    return w_head.astype(BF16), w_mid.astype(BF16), w_small.astype(BF16)


def kernel(x, mem, norm_gains, w_in, w_gate_up, b_gate, gla_head_gain, rel_table,
           w_out, w_xq, w_xkv, w_xo, w_ffn_in, w_ffn_out):
    b, l, d = x.shape
    nm = mem.shape[1]
    depth = w_in.shape[0]
    d_ff = w_ffn_out.shape[1]
    m = b * l
    assert d // 2 // GLA_HEADS == 2 * LANES and d // 2 // DSA_HEADS == LANES

    tq_dsa = 256
    tl_gla = 512
    corr = _bias_correction_tiles(rel_table, tq_dsa)
    tri = _chunk_sum_matrices(tl_gla)
    w2p = jnp.pad(w_gate_up, ((0, 0), (0, LANES - GLA_RANK), (0, 0))).astype(BF16)

    w_out, w_xq, w_xkv, w_xo, w_ffn_in, w_ffn_out = (
        w.astype(BF16) for w in (w_out, w_xq, w_xkv, w_xo, w_ffn_in, w_ffn_out))
    w_head, w_mid, w_small = _split_in_proj(w_in)

    xf = x.reshape(m, d)
    memf = mem.reshape(b * nm, d)
    for li in range(depth):
        g = norm_gains[li]
        pa, pb = in_proj(xf, g[0], w_head, w_mid, w_small, li, bm=1024, bn=1024, bs=512)
        pa, pb = pa.reshape(b, l, -1), pb.reshape(b, l, -1)
        o_gla = gla(pa, pb, w2p[li], b_gate[li].reshape(1, -1),
                    gla_head_gain[li].reshape(1, -1), tri, tl_gla)
        o_dsa = dsa(pa, pb, corr, tq_dsa)
        xf = mm_norm_res([o_gla.reshape(m, -1), o_dsa.reshape(m, -1)], w_out, li, g[1], xf, bm=512)
        kv = norm_matmul(memf, g[6], w_xkv, li, BF16, bm=min(512, b * nm), bn=1024)
        xf = xattn(xf.reshape(b, l, d), kv.reshape(b, nm, 2 * d), g[2], w_xq, g[3], w_xo, li,
                   tq=512).reshape(m, d)
        act = ffn_in(xf, g[4], w_ffn_in, li, d_ff, bm=1024, bn=512)
        xf = mm_norm_res([act], w_ffn_out, li, g[5], xf, bm=1024, nk=4)
    return xf.reshape(b, l, d)
```

```python
import functools
import math

import jax
import jax.numpy as jnp
from jax import lax
from jax.experimental import pallas as pl
from jax.experimental.pallas import tpu as pltpu

F32 = jnp.float32
BF16 = jnp.bfloat16

EPS = 1e-6
GLA_HEADS = 4
GLA_RANK = 16
GLA_TAU = 16.0
GLA_CHUNK = 64
DSA_HEADS = 8
IDX_HEADS = 16
IDX_DH = 64
TOPK_MAX = 256
REL_BUCKETS = 32
REL_MAX_DIST = 128
XA_HEADS = 4

LANES = 128
VMEM_LIMIT_BYTES = 60000 * 1024

PACK16 = 16
ROW_CHUNK = 128

NEG_BIG = -1e30
HALF16 = 2 ** 15
LOG2E = 1.4426950408889634

NT_DIMS = (((1,), (1,)), ((), ()))
TN_DIMS = (((0,), (0,)), ((), ()))


def _params(*sem):
    return pltpu.CompilerParams(dimension_semantics=sem, vmem_limit_bytes=VMEM_LIMIT_BYTES)


def _resident(shape, index_map):
    return pl.BlockSpec(shape, index_map, pipeline_mode=pl.Buffered(1))


def _rms(x, g):
    return x * lax.rsqrt(jnp.mean(x * x, axis=-1, keepdims=True) + EPS) * g


def _rms_to(x_ref, g_ref, h_ref):
    g = g_ref[...]

    def body(c, _):
        rows = pl.ds(pl.multiple_of(c * ROW_CHUNK, ROW_CHUNK), ROW_CHUNK)
        h_ref[rows, :] = _rms(x_ref[rows, :], g).astype(h_ref.dtype)
        return 0

    lax.fori_loop(0, x_ref.shape[0] // ROW_CHUNK, body, 0)


def _norm_matmul_kernel(x_ref, g_ref, w_ref, o_ref, h_ref):
    @pl.when(pl.program_id(1) == 0)
    def _():
        _rms_to(x_ref, g_ref, h_ref)

    o_ref[...] = jnp.dot(h_ref[...], w_ref[...], preferred_element_type=F32).astype(o_ref.dtype)


def norm_matmul(x, g, w, layer, out_dtype, bm, bn):
    m, k = x.shape
    n = w.shape[2]
    assert m % bm == 0 and n % bn == 0
    return pl.pallas_call(
        _norm_matmul_kernel,
        grid=(m // bm, n // bn),
        in_specs=[
            pl.BlockSpec((bm, k), lambda i, j: (i, 0)),
            _resident((1, k), lambda i, j: (0, 0)),
            pl.BlockSpec((None, k, bn), lambda i, j: (layer, 0, j)),
        ],
        out_specs=pl.BlockSpec((bm, bn), lambda i, j: (i, j)),
        out_shape=jax.ShapeDtypeStruct((m, n), out_dtype),
        scratch_shapes=[pltpu.VMEM((bm, k), BF16)],
        compiler_params=_params("parallel", "arbitrary"),
        name="norm_matmul",
    )(x, g.reshape(1, k), w)


def _in_proj_slot(i, j, n_slots):
    return jnp.where(i % 2 == 0, j, n_slots - 1 - j)


def _in_proj_kernel(x_ref, g_ref, wh_ref, wm_ref, ws_ref, oa_ref, ob_ref, h_ref, *, nh, ns, nm):
    s = _in_proj_slot(pl.program_id(0), pl.program_id(1), nh + ns + nm)

    @pl.when(pl.program_id(1) == 0)
    def _():
        _rms_to(x_ref, g_ref, h_ref)

    def project(w_ref, o_ref):
        o_ref[...] = jnp.dot(h_ref[...], w_ref[...], preferred_element_type=F32).astype(o_ref.dtype)

    pl.when(s < nh)(lambda: project(wh_ref, oa_ref))
    pl.when((s >= nh) & (s < nh + ns))(lambda: project(ws_ref, ob_ref))
    pl.when(s >= nh + ns)(lambda: project(wm_ref, oa_ref))


def in_proj(x, g, w_head, w_mid, w_small, layer, bm, bn, bs):
    m, k = x.shape
    n_head, n_mid, n_small = w_head.shape[2], w_mid.shape[2], w_small.shape[2]
    assert m % bm == 0 and n_head % bn == 0 and n_mid % bn == 0 and n_small % bs == 0
    nh, nm, ns = n_head // bn, n_mid // bn, n_small // bs
    n_slots = nh + ns + nm

    def slot(i, j):
        return _in_proj_slot(i, j, n_slots)

    def pa_block(i, j):
        s = slot(i, j)
        hold = jnp.where(i % 2 == 0, nh - 1, nh)
        return jnp.where(s < nh, s, jnp.where(s >= nh + ns, s - ns, hold))

    return pl.pallas_call(
        functools.partial(_in_proj_kernel, nh=nh, ns=ns, nm=nm),
        grid=(m // bm, n_slots),
        in_specs=[
            pl.BlockSpec((bm, k), lambda i, j: (i, 0)),
            _resident((1, k), lambda i, j: (0, 0)),
            pl.BlockSpec((None, k, bn), lambda i, j: (layer, 0, jnp.clip(slot(i, j), 0, nh - 1))),
            pl.BlockSpec((None, k, bn), lambda i, j: (layer, 0, jnp.clip(slot(i, j) - nh - ns, 0, nm - 1))),
            pl.BlockSpec((None, k, bs), lambda i, j: (layer, 0, jnp.clip(slot(i, j) - nh, 0, ns - 1))),
        ],
        out_specs=[
            pl.BlockSpec((bm, bn), lambda i, j: (i, pa_block(i, j))),
            pl.BlockSpec((bm, bs), lambda i, j: (i, jnp.clip(slot(i, j) - nh, 0, ns - 1))),
        ],
        out_shape=[
            jax.ShapeDtypeStruct((m, n_head + n_mid), BF16),
            jax.ShapeDtypeStruct((m, n_small), F32),
        ],
        scratch_shapes=[pltpu.VMEM((bm, k), BF16)],
        compiler_params=_params("parallel", "arbitrary"),
        name="in_proj",
    )(x, g.reshape(1, k), w_head, w_mid, w_small)


def _ffn_in_kernel(x_ref, g_ref, wg_ref, wu_ref, o_ref, h_ref):
    @pl.when(pl.program_id(1) == 0)
    def _():
        _rms_to(x_ref, g_ref, h_ref)

    h = h_ref[...]
    a = jnp.dot(h, wg_ref[...], preferred_element_type=F32)
    b = jnp.dot(h, wu_ref[...], preferred_element_type=F32)
    o_ref[...] = (a * (1.0 / (1.0 + jnp.exp(-a))) * b).astype(o_ref.dtype)


def ffn_in(x, g, w, layer, d_ff, bm, bn):
    m, k = x.shape
    assert m % bm == 0 and d_ff % bn == 0
    nj = d_ff // bn
    return pl.pallas_call(
        _ffn_in_kernel,
        grid=(m // bm, nj),
        in_specs=[
            pl.BlockSpec((bm, k), lambda i, j: (i, 0)),
            _resident((1, k), lambda i, j: (0, 0)),
            pl.BlockSpec((None, k, bn), lambda i, j: (layer, 0, j)),
            pl.BlockSpec((None, k, bn), lambda i, j: (layer, 0, j + nj)),
        ],
        out_specs=pl.BlockSpec((bm, bn), lambda i, j: (i, j)),
        out_shape=jax.ShapeDtypeStruct((m, d_ff), BF16),
        scratch_shapes=[pltpu.VMEM((bm, k), BF16)],
        compiler_params=_params("parallel", "arbitrary"),
        name="ffn_in",
    )(x, g.reshape(1, k), w, w)


def _mm_norm_res_kernel(*refs, n_lhs, nk):
    lhs = refs[:n_lhs]
    w_ref, g_ref, res_ref, o_ref = refs[n_lhs:n_lhs + 4]
    if nk == 1:
        y = None
        off = 0
        for a in lhs:
            wd = a.shape[1]
            part = jnp.dot(a[...], w_ref[off:off + wd, :], preferred_element_type=F32)
            y = part if y is None else y + part
            off += wd
        o_ref[...] = res_ref[...] + _rms(y, g_ref[...])
    else:
        k = pl.program_id(1)
        part = jnp.dot(lhs[0][...], w_ref[...], preferred_element_type=F32)

        @pl.when(k == 0)
        def _():
            o_ref[...] = part

        @pl.when(k > 0)
        def _():
            o_ref[...] += part

        @pl.when(k == nk - 1)
        def _():
            g = g_ref[...]

            def finish(c, _):
                rows = pl.ds(pl.multiple_of(c * ROW_CHUNK, ROW_CHUNK), ROW_CHUNK)
                o_ref[rows, :] = res_ref[rows, :] + _rms(o_ref[rows, :], g)
                return 0

            lax.fori_loop(0, o_ref.shape[0] // ROW_CHUNK, finish, 0)


def mm_norm_res(lhs_list, w, layer, g, res, bm, nk=1):
    m, n = res.shape
    kk = w.shape[1]
    assert m % bm == 0 and kk % nk == 0
    bk = kk // nk
    n_lhs = len(lhs_list)
    if nk == 1:
        lhs_specs = [pl.BlockSpec((bm, a.shape[1]), lambda i, k: (i, 0)) for a in lhs_list]
        w_spec = _resident((None, kk, n), lambda i, k: (layer, 0, 0))
    else:
        assert n_lhs == 1
        kblock = lambda i, k: jnp.where(i % 2 == 0, k, nk - 1 - k)
        lhs_specs = [pl.BlockSpec((bm, bk), lambda i, k: (i, kblock(i, k)))]
        w_spec = pl.BlockSpec((None, bk, n), lambda i, k: (layer, kblock(i, k), 0))
    return pl.pallas_call(
        functools.partial(_mm_norm_res_kernel, n_lhs=n_lhs, nk=nk),
        grid=(m // bm, nk),
        in_specs=lhs_specs + [
            w_spec,
            _resident((1, n), lambda i, k: (0, 0)),
            pl.BlockSpec((bm, n), lambda i, k: (i, 0)),
        ],
        out_specs=pl.BlockSpec((bm, n), lambda i, k: (i, 0)),
        out_shape=jax.ShapeDtypeStruct((m, n), F32),
        compiler_params=_params("parallel", "arbitrary"),
        name="mm_norm_res",
    )(*lhs_list, w, g.reshape(1, n), res)


def _gla_kernel(q_ref, k_ref, v_ref, z_ref, r_ref, w2_ref, bg_ref, gh_ref, tri_ref,
                o_ref, st_ref, *, dk):
    tl = q_ref.shape[0]

    @pl.when(pl.program_id(2) == 0)
    def _():
        st_ref[...] = jnp.zeros_like(st_ref)

    pre = jnp.dot(z_ref[...].astype(BF16), w2_ref[...], preferred_element_type=F32) + bg_ref[...]
    logg = (jnp.minimum(pre, 0.0) - jnp.log1p(jnp.exp(-jnp.abs(pre)))) * (1.0 / GLA_TAU)
    hi = logg.astype(BF16)
    lo = (logg - hi.astype(F32)).astype(BF16)
    incl = tri_ref[...]
    cs = jnp.dot(incl, jnp.concatenate([hi, lo], axis=1), preferred_element_type=F32)
    bcum = cs[:, :dk] + cs[:, dk:]
    n_chunks = tl // GLA_CHUNK
    b_last = [bcum[(c + 1) * GLA_CHUNK - 1:(c + 1) * GLA_CHUNK, :] for c in range(n_chunks)]
    suf = jnp.concatenate([jnp.broadcast_to(bl, (GLA_CHUNK, dk)) for bl in b_last], axis=0) - bcum

    qf = q_ref[...].astype(F32) * dk ** -0.5
    kf = k_ref[...].astype(F32)
    q_dec = (qf * jnp.exp(bcum)).astype(BF16)
    k_intra = (kf * jnp.exp(-bcum)).astype(BF16)
    k_state = (kf * jnp.exp(suf)).astype(BF16)
    v = v_ref[...]

    a = lax.dot_general(q_dec, k_intra, NT_DIMS, preferred_element_type=F32)
    a = jnp.where(incl > 0, a, 0.0).astype(BF16)
    o_intra = jnp.dot(a, v, preferred_element_type=F32)

    st = st_ref[...]
    outs = []
    for c in range(n_chunks):
        rows = slice(c * GLA_CHUNK, (c + 1) * GLA_CHUNK)
        o_inter = lax.dot_general(q_dec[rows], st.astype(BF16), NT_DIMS, preferred_element_type=F32)
        outs.append(o_intra[rows] + o_inter)
        decay = jnp.exp(b_last[c])
        upd = lax.dot_general(v[rows], k_state[rows], TN_DIMS, preferred_element_type=F32)
        st = decay * st + upd
    st_ref[...] = st
    o = jnp.concatenate(outs, axis=0)
    o = _rms(o, gh_ref[...])
    r = r_ref[...]
    o_ref[...] = (o * (r * (1.0 / (1.0 + jnp.exp(-r))))).astype(o_ref.dtype)


def gla(pa, pb, w2p, bgate, ghead, tri, tl):
    b, l, _ = pa.shape
    dk = LANES
    dv = 2 * dk
    hq = GLA_HEADS
    vb = 2 * GLA_HEADS * dk // dv
    zb = GLA_HEADS * dv // LANES
    return pl.pallas_call(
        functools.partial(_gla_kernel, dk=dk),
        grid=(b, GLA_HEADS, l // tl),
        in_specs=[
            pl.BlockSpec((None, tl, dk), lambda bi, h, t: (bi, t, h)),
            pl.BlockSpec((None, tl, dk), lambda bi, h, t: (bi, t, hq + h)),
            pl.BlockSpec((None, tl, dv), lambda bi, h, t: (bi, t, vb + h)),
            pl.BlockSpec((None, tl, LANES), lambda bi, h, t: (bi, t, zb)),
            pl.BlockSpec((None, tl, dv), lambda bi, h, t: (bi, t, h)),
            pl.BlockSpec((LANES, dk), lambda bi, h, t: (0, h)),
            pl.BlockSpec((1, dk), lambda bi, h, t: (0, h)),
            _resident((1, dv), lambda bi, h, t: (0, 0)),
            _resident((tl, tl), lambda bi, h, t: (0, 0)),
        ],
        out_specs=pl.BlockSpec((None, tl, dv), lambda bi, h, t: (bi, t, h)),
        out_shape=jax.ShapeDtypeStruct((b, l, GLA_HEADS * dv), BF16),
        scratch_shapes=[pltpu.VMEM((dv, dk), F32)],
        compiler_params=_params("parallel", "parallel", "arbitrary"),
        name="gla",
    )(pa, pa, pa, pb, pb, w2p, bgate, ghead, tri)


def _for_each_block_paired(n, body):
    def pair(p, _):
        body(2 * p)
        body(2 * p + 1)
        return 0

    lax.fori_loop(0, n >> 1, pair, 0)

    @pl.when(n & 1 == 1)
    def _():
        body(n - 1)


def _dsa_kernel(q_ref, iq_ref, iw_ref, k_ref, v_ref, ik_ref, corr_ref, o_ref,
                key_ref, hi_ref, lo_ref, madd_ref, wt_ref, m_ref, l_ref, acc_ref, vt_ref, s_ref,
                *, topk):
    tq = q_ref.shape[0]
    tk = tq
    dh = LANES
    sub = 8
    i = pl.program_id(1)
    nkb = i + 1
    scale = dh ** -0.5

    wt_ref[...] = (iw_ref[...] * (IDX_HEADS ** -0.5 * IDX_DH ** -0.5)).T
    qpos = lax.broadcasted_iota(jnp.int32, (tk, tq), 1) + i * tq
    kpos0 = lax.broadcasted_iota(jnp.int32, (tk, tq), 0)

    def score_block(kb):
        ik = ik_ref[pl.ds(pl.multiple_of(kb * tk, tk), tk), :].astype(BF16)
        ik_even, ik_odd = ik[:, :LANES], ik[:, LANES:]
        acc = jnp.zeros((tk, tq), F32)
        for p in range(IDX_HEADS // 2):
            qp = iq_ref[:, p * LANES:(p + 1) * LANES]
            for par, ikx in ((0, ik_even), (1, ik_odd)):
                h = 2 * p + par
                d = lax.dot_general(ikx, qp, NT_DIMS, preferred_element_type=F32)
                acc = acc + jnp.maximum(d, 0.0) * wt_ref[h:h + 1, :]
        sc = jnp.where(kpos0 + kb * tk <= qpos, acc, -jnp.inf)
        bits = lax.bitcast_convert_type(sc, jnp.int32)
        key = bits ^ ((bits >> 31) & 0x7FFFFFFF)
        key_ref[kb] = key
        hi_ref[kb] = (key >> 16).astype(jnp.int16)
        lo_ref[kb] = ((key & 0xFFFF) - HALF16).astype(jnp.int16)

    _for_each_block_paired(nkb, score_block)

    n_acc = 4
    slabs = tk // PACK16
    one16, zero16 = jnp.int16(1), jnp.int16(0)

    def rows16(x):
        return jnp.broadcast_to(x, (PACK16, tq)).astype(jnp.int16)

    def count16(src_ref, cand, strict):
        cand16 = rows16(cand)

        def body(kb, cnts):
            cnts = list(cnts)
            for r in range(slabs):
                blk = src_ref[kb, r * PACK16:(r + 1) * PACK16, :]
                hit = (blk > cand16) if strict else (blk >= cand16)
                cnts[r % n_acc] = cnts[r % n_acc] + jnp.where(hit, one16, zero16)
            return tuple(cnts)

        cnts = lax.fori_loop(0, nkb, body, (jnp.zeros((PACK16, tq), jnp.int16),) * n_acc)
        cnt = ((cnts[0] + cnts[1]) + (cnts[2] + cnts[3])).astype(jnp.int32)
        return jnp.sum(cnt, axis=0, keepdims=True)

    def bisect16(src_ref, kth):
        def step(s, t):
            cand = t + lax.shift_left(jnp.int32(1), 15 - s)
            return jnp.where(count16(src_ref, cand, False) >= kth, cand, t)

        return lax.fori_loop(0, 16, step, jnp.full((1, tq), -HALF16, jnp.int32))

    t_hi = bisect16(hi_ref, topk)
    above = count16(hi_ref, t_hi, True)
    t_hi16 = rows16(t_hi)

    def keep_threshold_bucket(kb, _):
        for r in range(slabs):
            rows = slice(r * PACK16, (r + 1) * PACK16)
            lo_ref[kb, rows, :] = jnp.where(hi_ref[kb, rows, :] == t_hi16, lo_ref[kb, rows, :],
                                            jnp.int16(-HALF16))
        return 0

    lax.fori_loop(0, nkb, keep_threshold_bucket, 0)
    t_lo = bisect16(lo_ref, topk - above)
    thr = t_hi * (2 * HALF16) + (t_lo + HALF16)

    def mask_block(kb, _):
        vis = kpos0 + kb * tk <= qpos
        madd_ref[kb] = jnp.where(vis, jnp.where(key_ref[kb] >= thr, 0.0, NEG_BIG), NEG_BIG)
        return 0

    lax.fori_loop(0, nkb, mask_block, 0)

    @pl.when(i == 0)
    def _():
        def transpose_values(kb, _):
            vt_ref[kb] = v_ref[pl.ds(pl.multiple_of(kb * tk, tk), tk), :].T
            return 0

        lax.fori_loop(0, vt_ref.shape[0], transpose_values, 0)

    m_ref[...] = jnp.full(m_ref.shape, NEG_BIG, F32)
    l_ref[...] = jnp.zeros(l_ref.shape, F32)
    acc_ref[...] = jnp.zeros(acc_ref.shape, F32)

    def attend(kb, near, buf=0):
        keys = pl.ds(pl.multiple_of(kb * tk, tk), tk)
        madd = madd_ref[kb]
        for h in range(DSA_HEADS):
            hs = slice(h * dh, (h + 1) * dh)
            s_ref[buf, h] = lax.dot_general(k_ref[keys, hs], q_ref[:, hs], NT_DIMS,
                                            preferred_element_type=F32)
        for h in range(DSA_HEADS):
            hs = slice(h * dh, (h + 1) * dh)
            s = s_ref[buf, h] * (scale * LOG2E)
            if near is not None:
                s = s + corr_ref[h, near]
            s = s + madd
            m_prev = m_ref[h, 0:1, :]
            m_next = jnp.maximum(m_prev, jnp.max(s, axis=0, keepdims=True))
            alpha = jnp.exp2(m_prev - m_next)
            p = jnp.exp2(s - m_next)
            l_next = alpha * l_ref[h, 0:1, :] + jnp.sum(p, axis=0, keepdims=True)
            pv = jnp.dot(vt_ref[kb, hs, :], p.astype(BF16),
                         preferred_element_type=F32)
            acc_ref[h] = alpha * acc_ref[h] + pv
            m_ref[h] = jnp.broadcast_to(m_next, (sub, tq))
            l_ref[h] = jnp.broadcast_to(l_next, (sub, tq))

    def far_pair(p, _):
        attend(2 * p, None, 0)
        attend(2 * p + 1, None, 1)
        return 0

    n_far = jnp.maximum(i - 1, 0)
    lax.fori_loop(0, n_far >> 1, far_pair, 0)

    @pl.when(n_far & 1 == 1)
    def _():
        attend(n_far - 1, None)

    @pl.when(i >= 1)
    def _():
        attend(i - 1, 1)

    attend(i, 0)
    for h in range(DSA_HEADS):
        o_h = acc_ref[h] / l_ref[h, 0:1, :]
        o_ref[:, h * dh:(h + 1) * dh] = o_h.T.astype(o_ref.dtype)


def dsa(pa, pb, corr, tq):
    b, l, _ = pa.shape
    dh = LANES
    hd = DSA_HEADS * dh
    qb = (2 * GLA_HEADS * LANES + GLA_HEADS * 2 * LANES) // hd
    iwb = GLA_HEADS * 2 * LANES // LANES + 1
    ikb = (iwb + 1) * LANES // (2 * LANES)
    topk = min(TOPK_MAX, l // 4)
    assert l % tq == 0 and tq >= topk and tq >= REL_MAX_DIST
    nkb = l // tq
    assert nkb * tq // PACK16 < HALF16
    return pl.pallas_call(
        functools.partial(_dsa_kernel, topk=topk),
        grid=(b, l // tq),
        in_specs=[
            pl.BlockSpec((None, tq, hd), lambda bi, i: (bi, i, qb)),
            pl.BlockSpec((None, tq, IDX_HEADS * IDX_DH), lambda bi, i: (bi, i, qb + 3)),
            pl.BlockSpec((None, tq, LANES), lambda bi, i: (bi, i, iwb)),
            _resident((None, l, hd), lambda bi, i: (bi, 0, qb + 1)),
            _resident((None, l, hd), lambda bi, i: (bi, 0, qb + 2)),
            _resident((None, l, 2 * LANES), lambda bi, i: (bi, 0, ikb)),
            _resident((DSA_HEADS, 2, tq, tq), lambda bi, i: (0, 0, 0, 0)),
        ],
        out_specs=pl.BlockSpec((None, tq, hd), lambda bi, i: (bi, i, 0)),
        out_shape=jax.ShapeDtypeStruct((b, l, hd), BF16),
        scratch_shapes=[
            pltpu.VMEM((nkb, tq, tq), jnp.int32),
            pltpu.VMEM((nkb, tq, tq), jnp.int16),
            pltpu.VMEM((nkb, tq, tq), jnp.int16),
            pltpu.VMEM((nkb, tq, tq), F32),
            pltpu.VMEM((LANES, tq), F32),
            pltpu.VMEM((DSA_HEADS, 8, tq), F32),
            pltpu.VMEM((DSA_HEADS, 8, tq), F32),
            pltpu.VMEM((DSA_HEADS, dh, tq), F32),
            pltpu.VMEM((nkb, hd, tq), BF16),
            pltpu.VMEM((2, DSA_HEADS, tq, tq), F32),
        ],
        compiler_params=_params("parallel", "arbitrary"),
        name="dsa",
    )(pa, pa, pb, pa, pa, pb, corr)


def _xattn_kernel(x_ref, gq_ref, wq_ref, kv_ref, wo_ref, go_ref, o_ref, h_ref, q_ref, a_ref):
    d = x_ref.shape[1]
    dh = d // XA_HEADS
    scale = dh ** -0.5
    _rms_to(x_ref, gq_ref, h_ref)
    q_ref[...] = jnp.dot(h_ref[...], wq_ref[...], preferred_element_type=F32).astype(q_ref.dtype)
    for h in range(XA_HEADS):
        hs = slice(h * dh, (h + 1) * dh)
        s = lax.dot_general(q_ref[:, hs], kv_ref[:, hs], NT_DIMS, preferred_element_type=F32) * scale
        e = jnp.exp(s - jnp.max(s, axis=-1, keepdims=True))
        o = jnp.dot(e.astype(BF16), kv_ref[:, d + h * dh:d + (h + 1) * dh], preferred_element_type=F32)
        a_ref[:, hs] = (o / jnp.sum(e, axis=-1, keepdims=True)).astype(a_ref.dtype)
    y = jnp.dot(a_ref[...], wo_ref[...], preferred_element_type=F32)
    o_ref[...] = x_ref[...] + _rms(y, go_ref[...])


def xattn(x, kv, gq, wq, go, wo, layer, tq):
    b, l, d = x.shape
    nm = kv.shape[1]
    vec = lambda: _resident((1, d), lambda bi, i: (0, 0))
    mat = lambda: _resident((None, d, d), lambda bi, i: (layer, 0, 0))
    return pl.pallas_call(
        _xattn_kernel,
        grid=(b, l // tq),
        in_specs=[
            pl.BlockSpec((None, tq, d), lambda bi, i: (bi, i, 0)),
            vec(), mat(),
            pl.BlockSpec((None, nm, 2 * d), lambda bi, i: (bi, 0, 0)),
            mat(), vec(),
        ],
        out_specs=pl.BlockSpec((None, tq, d), lambda bi, i: (bi, i, 0)),
        out_shape=jax.ShapeDtypeStruct((b, l, d), F32),
        scratch_shapes=[pltpu.VMEM((tq, d), BF16)] * 3,
        compiler_params=_params("parallel", "parallel"),
        name="xattn",
    )(x, gq.reshape(1, d), wq, kv, wo, go.reshape(1, d))


def _rel_bucket(dist):
    n = jnp.maximum(dist, 0)
    max_exact = REL_BUCKETS // 2
    nf = jnp.maximum(n, 1).astype(F32)
    large = max_exact + (jnp.log(nf / max_exact) / math.log(REL_MAX_DIST / max_exact)
                         * (REL_BUCKETS - max_exact)).astype(jnp.int32)
    large = jnp.minimum(large, REL_BUCKETS - 1)
    return jnp.where(n < max_exact, n, large)


def _bias_correction_tiles(rel_table, tq):
    t = jnp.arange(tq)
    d0 = t[:, None] - t[None, :]
    buckets = jnp.stack([_rel_bucket(d0), _rel_bucket(d0 + tq)], axis=0)
    onehot = (buckets[..., None] == jnp.arange(REL_BUCKETS)).astype(F32)
    far = rel_table[_rel_bucket(jnp.int32(2 * tq))]
    return jnp.einsum("ntsk,kh->hnst", onehot, (rel_table - far) * LOG2E,
                      precision=lax.Precision.HIGHEST)


def _chunk_sum_matrices(tl):
    r = jnp.arange(tl)
    same = (r[:, None] // GLA_CHUNK) == (r[None, :] // GLA_CHUNK)
    return (same & (r[None, :] <= r[:, None])).astype(BF16)


def _split_in_proj(w):
    depth, d, _ = w.shape
    half = d // 2
    sizes = [GLA_HEADS * LANES, GLA_HEADS * LANES, half, GLA_RANK, half,
             half, half, half, IDX_HEADS * IDX_DH, IDX_DH, IDX_HEADS]
    offs = [0]
    for s in sizes:
        offs.append(offs[-1] + s)
    assert offs[-1] == w.shape[2]
    seg = lambda n: w[:, :, offs[n]:offs[n + 1]]
    zero = lambda n: jnp.zeros((depth, d, n), w.dtype)
    ik = seg(9)
    w_head = w[:, :, :offs[3]]
    w_mid = w[:, :, offs[5]:offs[9]]
    w_small = jnp.concatenate([seg(4), seg(3), zero(LANES - GLA_RANK), seg(10), zero(LANES - IDX_HEADS),
                               ik, zero(2 * IDX_DH), ik], axis=2)
    return w_head.astype(BF16), w_mid.astype(BF16), w_small.astype(BF16)


def kernel(x, mem, norm_gains, w_in, w_gate_up, b_gate, gla_head_gain, rel_table,
           w_out, w_xq, w_xkv, w_xo, w_ffn_in, w_ffn_out):
    b, l, d = x.shape
    nm = mem.shape[1]
    depth = w_in.shape[0]
    d_ff = w_ffn_out.shape[1]
    m = b * l
    assert d // 2 // GLA_HEADS == 2 * LANES and d // 2 // DSA_HEADS == LANES

    tq_dsa = 256
    tl_gla = 512
    corr = _bias_correction_tiles(rel_table, tq_dsa)
    tri = _chunk_sum_matrices(tl_gla)
    w2p = jnp.pad(w_gate_up, ((0, 0), (0, LANES - GLA_RANK), (0, 0))).astype(BF16)

    w_out, w_xq, w_xkv, w_xo, w_ffn_in, w_ffn_out = (
        w.astype(BF16) for w in (w_out, w_xq, w_xkv, w_xo, w_ffn_in, w_ffn_out))
    w_head, w_mid, w_small = _split_in_proj(w_in)

    xf = x.reshape(m, d)
    memf = mem.reshape(b * nm, d)
    for li in range(depth):
        g = norm_gains[li]
        pa, pb = in_proj(xf, g[0], w_head, w_mid, w_small, li, bm=1024, bn=1024, bs=512)
        pa, pb = pa.reshape(b, l, -1), pb.reshape(b, l, -1)
        o_gla = gla(pa, pb, w2p[li], b_gate[li].reshape(1, -1),
                    gla_head_gain[li].reshape(1, -1), tri, tl_gla)
        o_dsa = dsa(pa, pb, corr, tq_dsa)
        xf = mm_norm_res([o_gla.reshape(m, -1), o_dsa.reshape(m, -1)], w_out, li, g[1], xf, bm=512)
        kv = norm_matmul(memf, g[6], w_xkv, li, BF16, bm=min(512, b * nm), bn=1024)
        xf = xattn(xf.reshape(b, l, d), kv.reshape(b, nm, 2 * d), g[2], w_xq, g[3], w_xo, li,
                   tq=512).reshape(m, d)
        act = ffn_in(xf, g[4], w_ffn_in, li, d_ff, bm=1024, bn=512)
        xf = mm_norm_res([act], w_ffn_out, li, g[5], xf, bm=1024, nk=4)
    return xf.reshape(b, l, d)
```

```python
import functools
import math

import jax
import jax.numpy as jnp
from jax import lax
from jax.experimental import pallas as pl
from jax.experimental.pallas import tpu as pltpu

F32 = jnp.float32
BF16 = jnp.bfloat16

EPS = 1e-6
GLA_HEADS = 4
GLA_RANK = 16
GLA_TAU = 16.0
GLA_CHUNK = 64
DSA_HEADS = 8
IDX_HEADS = 16
IDX_DH = 64
TOPK_MAX = 256
REL_BUCKETS = 32
REL_MAX_DIST = 128
XA_HEADS = 4

LANES = 128
VMEM_LIMIT_BYTES = 60000 * 1024

PACK16 = 16
ROW_CHUNK = 128

NEG_BIG = -1e30
HALF16 = 2 ** 15
LOG2E = 1.4426950408889634

NT_DIMS = (((1,), (1,)), ((), ()))
TN_DIMS = (((0,), (0,)), ((), ()))


def _params(*sem):
    return pltpu.CompilerParams(dimension_semantics=sem, vmem_limit_bytes=VMEM_LIMIT_BYTES)


def _resident(shape, index_map):
    return pl.BlockSpec(shape, index_map, pipeline_mode=pl.Buffered(1))


def _rms(x, g):
    return x * lax.rsqrt(jnp.mean(x * x, axis=-1, keepdims=True) + EPS) * g


def _rms_to(x_ref, g_ref, h_ref):
    g = g_ref[...]

    def body(c, _):
        rows = pl.ds(pl.multiple_of(c * ROW_CHUNK, ROW_CHUNK), ROW_CHUNK)
        h_ref[rows, :] = _rms(x_ref[rows, :], g).astype(h_ref.dtype)
        return 0

    lax.fori_loop(0, x_ref.shape[0] // ROW_CHUNK, body, 0)


def _norm_matmul_kernel(x_ref, g_ref, w_ref, o_ref, h_ref):
    @pl.when(pl.program_id(1) == 0)
    def _():
        _rms_to(x_ref, g_ref, h_ref)

    o_ref[...] = jnp.dot(h_ref[...], w_ref[...], preferred_element_type=F32).astype(o_ref.dtype)


def norm_matmul(x, g, w, layer, out_dtype, bm, bn):
    m, k = x.shape
    n = w.shape[2]
    assert m % bm == 0 and n % bn == 0
    return pl.pallas_call(
        _norm_matmul_kernel,
        grid=(m // bm, n // bn),
        in_specs=[
            pl.BlockSpec((bm, k), lambda i, j: (i, 0)),
            _resident((1, k), lambda i, j: (0, 0)),
            pl.BlockSpec((None, k, bn), lambda i, j: (layer, 0, j)),
        ],
        out_specs=pl.BlockSpec((bm, bn), lambda i, j: (i, j)),
        out_shape=jax.ShapeDtypeStruct((m, n), out_dtype),
        scratch_shapes=[pltpu.VMEM((bm, k), BF16)],
        compiler_params=_params("parallel", "arbitrary"),
        name="norm_matmul",
    )(x, g.reshape(1, k), w)


def _in_proj_slot(i, j, n_slots):
    return jnp.where(i % 2 == 0, j, n_slots - 1 - j)


def _in_proj_kernel(x_ref, g_ref, wh_ref, wm_ref, ws_ref, oa_ref, ob_ref, h_ref, *, nh, ns, nm):
    s = _in_proj_slot(pl.program_id(0), pl.program_id(1), nh + ns + nm)

    @pl.when(pl.program_id(1) == 0)
    def _():
        _rms_to(x_ref, g_ref, h_ref)

    def project(w_ref, o_ref):
        o_ref[...] = jnp.dot(h_ref[...], w_ref[...], preferred_element_type=F32).astype(o_ref.dtype)

    pl.when(s < nh)(lambda: project(wh_ref, oa_ref))
    pl.when((s >= nh) & (s < nh + ns))(lambda: project(ws_ref, ob_ref))
    pl.when(s >= nh + ns)(lambda: project(wm_ref, oa_ref))


def in_proj(x, g, w_head, w_mid, w_small, layer, bm, bn, bs):
    m, k = x.shape
    n_head, n_mid, n_small = w_head.shape[2], w_mid.shape[2], w_small.shape[2]
    assert m % bm == 0 and n_head % bn == 0 and n_mid % bn == 0 and n_small % bs == 0
    nh, nm, ns = n_head // bn, n_mid // bn, n_small // bs
    n_slots = nh + ns + nm

    def slot(i, j):
        return _in_proj_slot(i, j, n_slots)

    def pa_block(i, j):
        s = slot(i, j)
        hold = jnp.where(i % 2 == 0, nh - 1, nh)
        return jnp.where(s < nh, s, jnp.where(s >= nh + ns, s - ns, hold))

    return pl.pallas_call(
        functools.partial(_in_proj_kernel, nh=nh, ns=ns, nm=nm),
        grid=(m // bm, n_slots),
        in_specs=[
            pl.BlockSpec((bm, k), lambda i, j: (i, 0)),
            _resident((1, k), lambda i, j: (0, 0)),
            pl.BlockSpec((None, k, bn), lambda i, j: (layer, 0, jnp.clip(slot(i, j), 0, nh - 1))),
            pl.BlockSpec((None, k, bn), lambda i, j: (layer, 0, jnp.clip(slot(i, j) - nh - ns, 0, nm - 1))),
            pl.BlockSpec((None, k, bs), lambda i, j: (layer, 0, jnp.clip(slot(i, j) - nh, 0, ns - 1))),
        ],
        out_specs=[
            pl.BlockSpec((bm, bn), lambda i, j: (i, pa_block(i, j))),
            pl.BlockSpec((bm, bs), lambda i, j: (i, jnp.clip(slot(i, j) - nh, 0, ns - 1))),
        ],
        out_shape=[
            jax.ShapeDtypeStruct((m, n_head + n_mid), BF16),
            jax.ShapeDtypeStruct((m, n_small), F32),
        ],
        scratch_shapes=[pltpu.VMEM((bm, k), BF16)],
        compiler_params=_params("parallel", "arbitrary"),
        name="in_proj",
    )(x, g.reshape(1, k), w_head, w_mid, w_small)


def _ffn_in_kernel(x_ref, g_ref, wg_ref, wu_ref, o_ref, h_ref):
    @pl.when(pl.program_id(1) == 0)
    def _():
        _rms_to(x_ref, g_ref, h_ref)

    h = h_ref[...]
    a = jnp.dot(h, wg_ref[...], preferred_element_type=F32)
    b = jnp.dot(h, wu_ref[...], preferred_element_type=F32)
    o_ref[...] = (a * (1.0 / (1.0 + jnp.exp(-a))) * b).astype(o_ref.dtype)


def ffn_in(x, g, w, layer, d_ff, bm, bn):
    m, k = x.shape
    assert m % bm == 0 and d_ff % bn == 0
    nj = d_ff // bn
    return pl.pallas_call(
        _ffn_in_kernel,
        grid=(m // bm, nj),
        in_specs=[
            pl.BlockSpec((bm, k), lambda i, j: (i, 0)),
            _resident((1, k), lambda i, j: (0, 0)),
            pl.BlockSpec((None, k, bn), lambda i, j: (layer, 0, j)),
            pl.BlockSpec((None, k, bn), lambda i, j: (layer, 0, j + nj)),
        ],
        out_specs=pl.BlockSpec((bm, bn), lambda i, j: (i, j)),
        out_shape=jax.ShapeDtypeStruct((m, d_ff), BF16),
        scratch_shapes=[pltpu.VMEM((bm, k), BF16)],
        compiler_params=_params("parallel", "arbitrary"),
        name="ffn_in",
    )(x, g.reshape(1, k), w, w)


def _mm_norm_res_kernel(*refs, n_lhs, nk):
    lhs = refs[:n_lhs]
    w_ref, g_ref, res_ref, o_ref = refs[n_lhs:n_lhs + 4]
    if nk == 1:
        y = None
        off = 0
        for a in lhs:
            wd = a.shape[1]
            part = jnp.dot(a[...], w_ref[off:off + wd, :], preferred_element_type=F32)
            y = part if y is None else y + part
            off += wd
        o_ref[...] = res_ref[...] + _rms(y, g_ref[...])
    else:
        k = pl.program_id(1)
        part = jnp.dot(lhs[0][...], w_ref[...], preferred_element_type=F32)

        @pl.when(k == 0)
        def _():
            o_ref[...] = part

        @pl.when(k > 0)
        def _():
            o_ref[...] += part

        @pl.when(k == nk - 1)
        def _():
            g = g_ref[...]

            def finish(c, _):
                rows = pl.ds(pl.multiple_of(c * ROW_CHUNK, ROW_CHUNK), ROW_CHUNK)
                o_ref[rows, :] = res_ref[rows, :] + _rms(o_ref[rows, :], g)
                return 0

            lax.fori_loop(0, o_ref.shape[0] // ROW_CHUNK, finish, 0)


def mm_norm_res(lhs_list, w, layer, g, res, bm, nk=1):
    m, n = res.shape
    kk = w.shape[1]
    assert m % bm == 0 and kk % nk == 0
    bk = kk // nk
    n_lhs = len(lhs_list)
    if nk == 1:
        lhs_specs = [pl.BlockSpec((bm, a.shape[1]), lambda i, k: (i, 0)) for a in lhs_list]
        w_spec = _resident((None, kk, n), lambda i, k: (layer, 0, 0))
    else:
        assert n_lhs == 1
        kblock = lambda i, k: jnp.where(i % 2 == 0, k, nk - 1 - k)
        lhs_specs = [pl.BlockSpec((bm, bk), lambda i, k: (i, kblock(i, k)))]
        w_spec = pl.BlockSpec((None, bk, n), lambda i, k: (layer, kblock(i, k), 0))
    return pl.pallas_call(
        functools.partial(_mm_norm_res_kernel, n_lhs=n_lhs, nk=nk),
        grid=(m // bm, nk),
        in_specs=lhs_specs + [
            w_spec,
            _resident((1, n), lambda i, k: (0, 0)),
            pl.BlockSpec((bm, n), lambda i, k: (i, 0)),
        ],
        out_specs=pl.BlockSpec((bm, n), lambda i, k: (i, 0)),
        out_shape=jax.ShapeDtypeStruct((m, n), F32),
        compiler_params=_params("parallel", "arbitrary"),
        name="mm_norm_res",
    )(*lhs_list, w, g.reshape(1, n), res)


def _gla_kernel(q_ref, k_ref, v_ref, z_ref, r_ref, w2_ref, bg_ref, gh_ref, tri_ref,
                o_ref, st_ref, *, dk, dv):
    tl = q_ref.shape[0]
    heads = q_ref.shape[1] // dk
    n_chunks = tl // GLA_CHUNK

    @pl.when(pl.program_id(2) == 0)
    def _():
        st_ref[...] = jnp.zeros_like(st_ref)

    pre = jnp.dot(z_ref[...].astype(BF16), w2_ref[...], preferred_element_type=F32) + bg_ref[...]
    logg = (jnp.minimum(pre, 0.0) - jnp.log1p(jnp.exp(-jnp.abs(pre)))) * (1.0 / GLA_TAU)
    hi = logg.astype(BF16)
    lo = (logg - hi.astype(F32)).astype(BF16)
    incl = tri_ref[...]
    cs = jnp.dot(incl, jnp.concatenate([hi, lo], axis=1), preferred_element_type=F32)
    bcum_all = cs[:, :heads * dk] + cs[:, heads * dk:]
    gh = gh_ref[...]

    for hh in range(heads):
        ks = slice(hh * dk, (hh + 1) * dk)
        vs = slice(hh * dv, (hh + 1) * dv)
        bcum = bcum_all[:, ks]
        b_last = [bcum[(c + 1) * GLA_CHUNK - 1:(c + 1) * GLA_CHUNK, :] for c in range(n_chunks)]
        suf = jnp.concatenate([jnp.broadcast_to(bl, (GLA_CHUNK, dk)) for bl in b_last], axis=0) - bcum

        qf = q_ref[:, ks].astype(F32) * dk ** -0.5
        kf = k_ref[:, ks].astype(F32)
        q_dec = (qf * jnp.exp(bcum)).astype(BF16)
        k_intra = (kf * jnp.exp(-bcum)).astype(BF16)
        k_state = (kf * jnp.exp(suf)).astype(BF16)
        v = v_ref[:, vs]

        a = lax.dot_general(q_dec, k_intra, NT_DIMS, preferred_element_type=F32)
        a = jnp.where(incl > 0, a, 0.0).astype(BF16)
        o_intra = jnp.dot(a, v, preferred_element_type=F32)

        st = st_ref[hh]
        outs = []
        for c in range(n_chunks):
            rows = slice(c * GLA_CHUNK, (c + 1) * GLA_CHUNK)
            o_inter = lax.dot_general(q_dec[rows], st.astype(BF16), NT_DIMS,
                                      preferred_element_type=F32)
            outs.append(o_intra[rows] + o_inter)
            decay = jnp.exp(b_last[c])
            upd = lax.dot_general(v[rows], k_state[rows], TN_DIMS, preferred_element_type=F32)
            st = decay * st + upd
        st_ref[hh] = st
        o = _rms(jnp.concatenate(outs, axis=0), gh)
        r = r_ref[:, vs]
        o_ref[:, vs] = (o * (r * (1.0 / (1.0 + jnp.exp(-r))))).astype(o_ref.dtype)


def gla(pa, pb, w2p, bgate, ghead, tri, tl, heads_per_step):
    b, l, _ = pa.shape
    dk = LANES
    dv = 2 * dk
    hp = heads_per_step
    assert GLA_HEADS % hp == 0
    groups = GLA_HEADS // hp
    wk, wv = hp * dk, hp * dv
    kb0 = GLA_HEADS * dk // wk
    vb0 = 2 * GLA_HEADS * dk // wv
    zb = GLA_HEADS * dv // LANES
    return pl.pallas_call(
        functools.partial(_gla_kernel, dk=dk, dv=dv),
        grid=(b, groups, l // tl),
        in_specs=[
            pl.BlockSpec((None, tl, wk), lambda bi, h, t: (bi, t, h)),
            pl.BlockSpec((None, tl, wk), lambda bi, h, t: (bi, t, kb0 + h)),
            pl.BlockSpec((None, tl, wv), lambda bi, h, t: (bi, t, vb0 + h)),
            pl.BlockSpec((None, tl, LANES), lambda bi, h, t: (bi, t, zb)),
            pl.BlockSpec((None, tl, wv), lambda bi, h, t: (bi, t, h)),
            pl.BlockSpec((LANES, wk), lambda bi, h, t: (0, h)),
            pl.BlockSpec((1, wk), lambda bi, h, t: (0, h)),
            _resident((1, dv), lambda bi, h, t: (0, 0)),
            _resident((tl, tl), lambda bi, h, t: (0, 0)),
        ],
        out_specs=pl.BlockSpec((None, tl, wv), lambda bi, h, t: (bi, t, h)),
        out_shape=jax.ShapeDtypeStruct((b, l, GLA_HEADS * dv), BF16),
        scratch_shapes=[pltpu.VMEM((hp, dv, dk), F32)],
        compiler_params=_params("parallel", "parallel", "arbitrary"),
        name="gla",
    )(pa, pa, pa, pb, pb, w2p, bgate, ghead, tri)


def _for_each_block_paired(n, body):
    def pair(p, _):
        body(2 * p)
        body(2 * p + 1)
        return 0

    lax.fori_loop(0, n >> 1, pair, 0)

    @pl.when(n & 1 == 1)
    def _():
        body(n - 1)


def _dsa_kernel(q_ref, iq_ref, iw_ref, k_ref, v_ref, ik_ref, corr_ref, o_ref,
                key_ref, hi_ref, lo_ref, madd_ref, wt_ref, m_ref, l_ref, acc_ref, vt_ref, s_ref,
                *, topk):
    tq = q_ref.shape[0]
    tk = tq
    dh = LANES
    sub = 8
    i = pl.program_id(1)
    nkb = i + 1
    scale = dh ** -0.5

    wt_ref[...] = (iw_ref[...] * (IDX_HEADS ** -0.5 * IDX_DH ** -0.5)).T
    qpos = lax.broadcasted_iota(jnp.int32, (tk, tq), 1) + i * tq
    kpos0 = lax.broadcasted_iota(jnp.int32, (tk, tq), 0)

    def score_block(kb):
        ik = ik_ref[pl.ds(pl.multiple_of(kb * tk, tk), tk), :].astype(BF16)
        ik_even, ik_odd = ik[:, :LANES], ik[:, LANES:]
        acc = jnp.zeros((tk, tq), F32)
        for p in range(IDX_HEADS // 2):
            qp = iq_ref[:, p * LANES:(p + 1) * LANES]
            for par, ikx in ((0, ik_even), (1, ik_odd)):
                h = 2 * p + par
                d = lax.dot_general(ikx, qp, NT_DIMS, preferred_element_type=F32)
                acc = acc + jnp.maximum(d, 0.0) * wt_ref[h:h + 1, :]
        sc = jnp.where(kpos0 + kb * tk <= qpos, acc, -jnp.inf)
        bits = lax.bitcast_convert_type(sc, jnp.int32)
        key = bits ^ ((bits >> 31) & 0x7FFFFFFF)
        key_ref[kb] = key
        hi_ref[kb] = (key >> 16).astype(jnp.int16)
        lo_ref[kb] = ((key & 0xFFFF) - HALF16).astype(jnp.int16)

    _for_each_block_paired(nkb, score_block)

    n_acc = 4
    slabs = tk // PACK16
    one16, zero16 = jnp.int16(1), jnp.int16(0)

    def rows16(x):
        return jnp.broadcast_to(x, (PACK16, tq)).astype(jnp.int16)

    def count16(src_ref, cand, strict):
        cand16 = rows16(cand)

        def body(kb, cnts):
            cnts = list(cnts)
            for r in range(slabs):
                blk = src_ref[kb, r * PACK16:(r + 1) * PACK16, :]
                hit = (blk > cand16) if strict else (blk >= cand16)
                cnts[r % n_acc] = cnts[r % n_acc] + jnp.where(hit, one16, zero16)
            return tuple(cnts)

        cnts = lax.fori_loop(0, nkb, body, (jnp.zeros((PACK16, tq), jnp.int16),) * n_acc)
        cnt = ((cnts[0] + cnts[1]) + (cnts[2] + cnts[3])).astype(jnp.int32)
        return jnp.sum(cnt, axis=0, keepdims=True)

    def bisect16(src_ref, kth):
        def step(s, t):
            cand = t + lax.shift_left(jnp.int32(1), 15 - s)
            return jnp.where(count16(src_ref, cand, False) >= kth, cand, t)

        return lax.fori_loop(0, 16, step, jnp.full((1, tq), -HALF16, jnp.int32))

    t_hi = bisect16(hi_ref, topk)
    above = count16(hi_ref, t_hi, True)
    t_hi16 = rows16(t_hi)

    def keep_threshold_bucket(kb, _):
        for r in range(slabs):
            rows = slice(r * PACK16, (r + 1) * PACK16)
            lo_ref[kb, rows, :] = jnp.where(hi_ref[kb, rows, :] == t_hi16, lo_ref[kb, rows, :],
                                            jnp.int16(-HALF16))
        return 0

    lax.fori_loop(0, nkb, keep_threshold_bucket, 0)
    t_lo = bisect16(lo_ref, topk - above)
    thr = t_hi * (2 * HALF16) + (t_lo + HALF16)

    def mask_block(kb, _):
        vis = kpos0 + kb * tk <= qpos
        madd_ref[kb] = jnp.where(vis, jnp.where(key_ref[kb] >= thr, 0.0, NEG_BIG), NEG_BIG)
        return 0

    lax.fori_loop(0, nkb, mask_block, 0)

    @pl.when(i == 0)
    def _():
        def transpose_values(kb, _):
            vt_ref[kb] = v_ref[pl.ds(pl.multiple_of(kb * tk, tk), tk), :].T
            return 0

        lax.fori_loop(0, vt_ref.shape[0], transpose_values, 0)

    m_ref[...] = jnp.full(m_ref.shape, NEG_BIG, F32)
    l_ref[...] = jnp.zeros(l_ref.shape, F32)
    acc_ref[...] = jnp.zeros(acc_ref.shape, F32)

    def attend(kb, near, buf=0):
        keys = pl.ds(pl.multiple_of(kb * tk, tk), tk)
        madd = madd_ref[kb]
        for h in range(DSA_HEADS):
            hs = slice(h * dh, (h + 1) * dh)
            s_ref[buf, h] = lax.dot_general(k_ref[keys, hs], q_ref[:, hs], NT_DIMS,
                                            preferred_element_type=F32)
        for h in range(DSA_HEADS):
            hs = slice(h * dh, (h + 1) * dh)
            s = s_ref[buf, h] * (scale * LOG2E)
            if near is not None:
                s = s + corr_ref[h, near]
            s = s + madd
            m_prev = m_ref[h, 0:1, :]
            m_next = jnp.maximum(m_prev, jnp.max(s, axis=0, keepdims=True))
            alpha = jnp.exp2(m_prev - m_next)
            p = jnp.exp2(s - m_next)
            l_next = alpha * l_ref[h, 0:1, :] + jnp.sum(p, axis=0, keepdims=True)
            pv = jnp.dot(vt_ref[kb, hs, :], p.astype(BF16),
                         preferred_element_type=F32)
            acc_ref[h] = alpha * acc_ref[h] + pv
            m_ref[h] = jnp.broadcast_to(m_next, (sub, tq))
            l_ref[h] = jnp.broadcast_to(l_next, (sub, tq))

    def far_pair(p, _):
        attend(2 * p, None, 0)
        attend(2 * p + 1, None, 1)
        return 0

    n_far = jnp.maximum(i - 1, 0)
    lax.fori_loop(0, n_far >> 1, far_pair, 0)

    @pl.when(n_far & 1 == 1)
    def _():
        attend(n_far - 1, None)

    @pl.when(i >= 1)
    def _():
        attend(i - 1, 1, 0)
        attend(i, 0, 1)

    @pl.when(i == 0)
    def _():
        attend(i, 0)

    for h in range(DSA_HEADS):
        o_h = acc_ref[h] / l_ref[h, 0:1, :]
        o_ref[:, h * dh:(h + 1) * dh] = o_h.T.astype(o_ref.dtype)


def dsa(pa, pb, corr, tq):
    b, l, _ = pa.shape
    dh = LANES
    hd = DSA_HEADS * dh
    qb = (2 * GLA_HEADS * LANES + GLA_HEADS * 2 * LANES) // hd
    iwb = GLA_HEADS * 2 * LANES // LANES + 1
    ikb = (iwb + 1) * LANES // (2 * LANES)
    topk = min(TOPK_MAX, l // 4)
    assert l % tq == 0 and tq >= topk and tq >= REL_MAX_DIST
    nkb = l // tq
    assert nkb * tq // PACK16 < HALF16
    return pl.pallas_call(
        functools.partial(_dsa_kernel, topk=topk),
        grid=(b, l // tq),
        in_specs=[
            pl.BlockSpec((None, tq, hd), lambda bi, i: (bi, i, qb)),
            pl.BlockSpec((None, tq, IDX_HEADS * IDX_DH), lambda bi, i: (bi, i, qb + 3)),
            pl.BlockSpec((None, tq, LANES), lambda bi, i: (bi, i, iwb)),
            _resident((None, l, hd), lambda bi, i: (bi, 0, qb + 1)),
            _resident((None, l, hd), lambda bi, i: (bi, 0, qb + 2)),
            _resident((None, l, 2 * LANES), lambda bi, i: (bi, 0, ikb)),
            _resident((DSA_HEADS, 2, tq, tq), lambda bi, i: (0, 0, 0, 0)),
        ],
        out_specs=pl.BlockSpec((None, tq, hd), lambda bi, i: (bi, i, 0)),
        out_shape=jax.ShapeDtypeStruct((b, l, hd), BF16),
        scratch_shapes=[
            pltpu.VMEM((nkb, tq, tq), jnp.int32),
            pltpu.VMEM((nkb, tq, tq), jnp.int16),
            pltpu.VMEM((nkb, tq, tq), jnp.int16),
            pltpu.VMEM((nkb, tq, tq), F32),
            pltpu.VMEM((LANES, tq), F32),
            pltpu.VMEM((DSA_HEADS, 8, tq), F32),
            pltpu.VMEM((DSA_HEADS, 8, tq), F32),
            pltpu.VMEM((DSA_HEADS, dh, tq), F32),
            pltpu.VMEM((nkb, hd, tq), BF16),
            pltpu.VMEM((2, DSA_HEADS, tq, tq), F32),
        ],
        compiler_params=_params("parallel", "arbitrary"),
        name="dsa",
    )(pa, pa, pb, pa, pa, pb, corr)


def _xattn_kernel(x_ref, gq_ref, wq_ref, kv_ref, wo_ref, go_ref, o_ref, h_ref, q_ref, a_ref):
    d = x_ref.shape[1]
    dh = d // XA_HEADS
    scale = dh ** -0.5
    _rms_to(x_ref, gq_ref, h_ref)
    q_ref[...] = jnp.dot(h_ref[...], wq_ref[...], preferred_element_type=F32).astype(q_ref.dtype)
    for h in range(XA_HEADS):
        hs = slice(h * dh, (h + 1) * dh)
        s = lax.dot_general(q_ref[:, hs], kv_ref[:, hs], NT_DIMS, preferred_element_type=F32) * scale
        e = jnp.exp(s - jnp.max(s, axis=-1, keepdims=True))
        o = jnp.dot(e.astype(BF16), kv_ref[:, d + h * dh:d + (h + 1) * dh], preferred_element_type=F32)
        a_ref[:, hs] = (o / jnp.sum(e, axis=-1, keepdims=True)).astype(a_ref.dtype)
    y = jnp.dot(a_ref[...], wo_ref[...], preferred_element_type=F32)
    o_ref[...] = x_ref[...] + _rms(y, go_ref[...])


def xattn(x, kv, gq, wq, go, wo, layer, tq):
    b, l, d = x.shape
    nm = kv.shape[1]
    vec = lambda: _resident((1, d), lambda bi, i: (0, 0))
    mat = lambda: _resident((None, d, d), lambda bi, i: (layer, 0, 0))
    return pl.pallas_call(
        _xattn_kernel,
        grid=(b, l // tq),
        in_specs=[
            pl.BlockSpec((None, tq, d), lambda bi, i: (bi, i, 0)),
            vec(), mat(),
            pl.BlockSpec((None, nm, 2 * d), lambda bi, i: (bi, 0, 0)),
            mat(), vec(),
        ],
        out_specs=pl.BlockSpec((None, tq, d), lambda bi, i: (bi, i, 0)),
        out_shape=jax.ShapeDtypeStruct((b, l, d), F32),
        scratch_shapes=[pltpu.VMEM((tq, d), BF16)] * 3,
        compiler_params=_params("parallel", "parallel"),
        name="xattn",
    )(x, gq.reshape(1, d), wq, kv, wo, go.reshape(1, d))


def _rel_bucket(dist):
    n = jnp.maximum(dist, 0)
    max_exact = REL_BUCKETS // 2
    nf = jnp.maximum(n, 1).astype(F32)
    large = max_exact + (jnp.log(nf / max_exact) / math.log(REL_MAX_DIST / max_exact)
                         * (REL_BUCKETS - max_exact)).astype(jnp.int32)
    large = jnp.minimum(large, REL_BUCKETS - 1)
    return jnp.where(n < max_exact, n, large)


def _bias_correction_tiles(rel_table, tq):
    t = jnp.arange(tq)
    d0 = t[:, None] - t[None, :]
    buckets = jnp.stack([_rel_bucket(d0), _rel_bucket(d0 + tq)], axis=0)
    onehot = (buckets[..., None] == jnp.arange(REL_BUCKETS)).astype(F32)
    far = rel_table[_rel_bucket(jnp.int32(2 * tq))]
    return jnp.einsum("ntsk,kh->hnst", onehot, (rel_table - far) * LOG2E,
                      precision=lax.Precision.HIGHEST)


def _chunk_sum_matrices(tl):
    r = jnp.arange(tl)
    same = (r[:, None] // GLA_CHUNK) == (r[None, :] // GLA_CHUNK)
    return (same & (r[None, :] <= r[:, None])).astype(BF16)


def _split_in_proj(w):
    depth, d, _ = w.shape
    half = d // 2
    sizes = [GLA_HEADS * LANES, GLA_HEADS * LANES, half, GLA_RANK, half,
             half, half, half, IDX_HEADS * IDX_DH, IDX_DH, IDX_HEADS]
    offs = [0]
    for s in sizes:
        offs.append(offs[-1] + s)
    assert offs[-1] == w.shape[2]
    seg = lambda n: w[:, :, offs[n]:offs[n + 1]]
    zero = lambda n: jnp.zeros((depth, d, n), w.dtype)
    ik = seg(9)
    w_head = w[:, :, :offs[3]]
    w_mid = w[:, :, offs[5]:offs[9]]
    w_small = jnp.concatenate([seg(4), seg(3), zero(LANES - GLA_RANK), seg(10), zero(LANES - IDX_HEADS),
                               ik, zero(2 * IDX_DH), ik], axis=2)
    return w_head.astype(BF16), w_mid.astype(BF16), w_small.astype(BF16)


def kernel(x, mem, norm_gains, w_in, w_gate_up, b_gate, gla_head_gain, rel_table,
           w_out, w_xq, w_xkv, w_xo, w_ffn_in, w_ffn_out):
    b, l, d = x.shape
    nm = mem.shape[1]
    depth = w_in.shape[0]
    d_ff = w_ffn_out.shape[1]
    m = b * l
    assert d // 2 // GLA_HEADS == 2 * LANES and d // 2 // DSA_HEADS == LANES

    tq_dsa = 256
    tl_gla = 512
    corr = _bias_correction_tiles(rel_table, tq_dsa)
    tri = _chunk_sum_matrices(tl_gla)
    w2p = jnp.pad(w_gate_up, ((0, 0), (0, LANES - GLA_RANK), (0, 0))).astype(BF16)

    w_out, w_xq, w_xkv, w_xo, w_ffn_in, w_ffn_out = (
        w.astype(BF16) for w in (w_out, w_xq, w_xkv, w_xo, w_ffn_in, w_ffn_out))
    w_head, w_mid, w_small = _split_in_proj(w_in)

    xf = x.reshape(m, d)
    memf = mem.reshape(b * nm, d)
    for li in range(depth):
        g = norm_gains[li]
        pa, pb = in_proj(xf, g[0], w_head, w_mid, w_small, li, bm=1024, bn=1024, bs=512)
        pa, pb = pa.reshape(b, l, -1), pb.reshape(b, l, -1)
        o_gla = gla(pa, pb, w2p[li], b_gate[li].reshape(1, -1),
                    gla_head_gain[li].reshape(1, -1), tri, tl_gla, heads_per_step=2)
        o_dsa = dsa(pa, pb, corr, tq_dsa)
        xf = mm_norm_res([o_gla.reshape(m, -1), o_dsa.reshape(m, -1)], w_out, li, g[1], xf, bm=512)
        kv = norm_matmul(memf, g[6], w_xkv, li, BF16, bm=min(512, b * nm), bn=1024)
        xf = xattn(xf.reshape(b, l, d), kv.reshape(b, nm, 2 * d), g[2], w_xq, g[3], w_xo, li,
                   tq=512).reshape(m, d)
        act = ffn_in(xf, g[4], w_ffn_in, li, d_ff, bm=1024, bn=512)
        xf = mm_norm_res([act], w_ffn_out, li, g[5], xf, bm=1024, nk=4)
    return xf.reshape(b, l, d)
```

```python
import functools
import math

import jax
import jax.numpy as jnp
from jax import lax
from jax.experimental import pallas as pl
from jax.experimental.pallas import tpu as pltpu

F32 = jnp.float32
BF16 = jnp.bfloat16

EPS = 1e-6
GLA_HEADS = 4
GLA_RANK = 16
GLA_TAU = 16.0
GLA_CHUNK = 64
DSA_HEADS = 8
IDX_HEADS = 16
IDX_DH = 64
TOPK_MAX = 256
REL_BUCKETS = 32
REL_MAX_DIST = 128
XA_HEADS = 4

LANES = 128
VMEM_LIMIT_BYTES = 60000 * 1024

PACK16 = 16
ROW_CHUNK = 128

NEG_BIG = -1e30
HALF16 = 2 ** 15
LOG2E = 1.4426950408889634

NT_DIMS = (((1,), (1,)), ((), ()))
TN_DIMS = (((0,), (0,)), ((), ()))


def _params(*sem):
    return pltpu.CompilerParams(dimension_semantics=sem, vmem_limit_bytes=VMEM_LIMIT_BYTES)


def _resident(shape, index_map):
    return pl.BlockSpec(shape, index_map, pipeline_mode=pl.Buffered(1))


def _rms(x, g):
    return x * lax.rsqrt(jnp.mean(x * x, axis=-1, keepdims=True) + EPS) * g


def _rms_to(x_ref, g_ref, h_ref):
    g = g_ref[...]

    def body(c, _):
        rows = pl.ds(pl.multiple_of(c * ROW_CHUNK, ROW_CHUNK), ROW_CHUNK)
        h_ref[rows, :] = _rms(x_ref[rows, :], g).astype(h_ref.dtype)
        return 0

    lax.fori_loop(0, x_ref.shape[0] // ROW_CHUNK, body, 0)


def _norm_matmul_kernel(x_ref, g_ref, w_ref, o_ref, h_ref):
    @pl.when(pl.program_id(1) == 0)
    def _():
        _rms_to(x_ref, g_ref, h_ref)

    o_ref[...] = jnp.dot(h_ref[...], w_ref[...], preferred_element_type=F32).astype(o_ref.dtype)


def norm_matmul(x, g, w, layer, out_dtype, bm, bn):
    m, k = x.shape
    n = w.shape[2]
    assert m % bm == 0 and n % bn == 0
    return pl.pallas_call(
        _norm_matmul_kernel,
        grid=(m // bm, n // bn),
        in_specs=[
            pl.BlockSpec((bm, k), lambda i, j: (i, 0)),
            _resident((1, k), lambda i, j: (0, 0)),
            pl.BlockSpec((None, k, bn), lambda i, j: (layer, 0, j)),
        ],
        out_specs=pl.BlockSpec((bm, bn), lambda i, j: (i, j)),
        out_shape=jax.ShapeDtypeStruct((m, n), out_dtype),
        scratch_shapes=[pltpu.VMEM((bm, k), BF16)],
        compiler_params=_params("parallel", "arbitrary"),
        name="norm_matmul",
    )(x, g.reshape(1, k), w)


def _in_proj_slot(i, j, n_slots):
    return jnp.where(i % 2 == 0, j, n_slots - 1 - j)


def _in_proj_kernel(x_ref, g_ref, wh_ref, wm_ref, ws_ref, oa_ref, ob_ref, h_ref, *, nh, ns, nm):
    s = _in_proj_slot(pl.program_id(0), pl.program_id(1), nh + ns + nm)

    @pl.when(pl.program_id(1) == 0)
    def _():
        _rms_to(x_ref, g_ref, h_ref)

    def project(w_ref, o_ref):
        o_ref[...] = jnp.dot(h_ref[...], w_ref[...], preferred_element_type=F32).astype(o_ref.dtype)

    pl.when(s < nh)(lambda: project(wh_ref, oa_ref))
    pl.when((s >= nh) & (s < nh + ns))(lambda: project(ws_ref, ob_ref))
    pl.when(s >= nh + ns)(lambda: project(wm_ref, oa_ref))


def in_proj(x, g, w_head, w_mid, w_small, layer, bm, bn, bs):
    m, k = x.shape
    n_head, n_mid, n_small = w_head.shape[2], w_mid.shape[2], w_small.shape[2]
    assert m % bm == 0 and n_head % bn == 0 and n_mid % bn == 0 and n_small % bs == 0
    nh, nm, ns = n_head // bn, n_mid // bn, n_small // bs
    n_slots = nh + ns + nm

    def slot(i, j):
        return _in_proj_slot(i, j, n_slots)

    def pa_block(i, j):
        s = slot(i, j)
        hold = jnp.where(i % 2 == 0, nh - 1, nh)
        return jnp.where(s < nh, s, jnp.where(s >= nh + ns, s - ns, hold))

    return pl.pallas_call(
        functools.partial(_in_proj_kernel, nh=nh, ns=ns, nm=nm),
        grid=(m // bm, n_slots),
        in_specs=[
            pl.BlockSpec((bm, k), lambda i, j: (i, 0)),
            _resident((1, k), lambda i, j: (0, 0)),
            pl.BlockSpec((None, k, bn), lambda i, j: (layer, 0, jnp.clip(slot(i, j), 0, nh - 1))),
            pl.BlockSpec((None, k, bn), lambda i, j: (layer, 0, jnp.clip(slot(i, j) - nh - ns, 0, nm - 1))),
            pl.BlockSpec((None, k, bs), lambda i, j: (layer, 0, jnp.clip(slot(i, j) - nh, 0, ns - 1))),
        ],
        out_specs=[
            pl.BlockSpec((bm, bn), lambda i, j: (i, pa_block(i, j))),
            pl.BlockSpec((bm, bs), lambda i, j: (i, jnp.clip(slot(i, j) - nh, 0, ns - 1))),
        ],
        out_shape=[
            jax.ShapeDtypeStruct((m, n_head + n_mid), BF16),
            jax.ShapeDtypeStruct((m, n_small), F32),
        ],
        scratch_shapes=[pltpu.VMEM((bm, k), BF16)],
        compiler_params=_params("parallel", "arbitrary"),
        name="in_proj",
    )(x, g.reshape(1, k), w_head, w_mid, w_small)


def _ffn_in_kernel(x_ref, g_ref, wg_ref, wu_ref, o_ref, h_ref):
    @pl.when(pl.program_id(1) == 0)
    def _():
        _rms_to(x_ref, g_ref, h_ref)

    h = h_ref[...]
    a = jnp.dot(h, wg_ref[...], preferred_element_type=F32)
    b = jnp.dot(h, wu_ref[...], preferred_element_type=F32)
    o_ref[...] = (a * (1.0 / (1.0 + jnp.exp(-a))) * b).astype(o_ref.dtype)


def ffn_in(x, g, w, layer, d_ff, bm, bn):
    m, k = x.shape
    assert m % bm == 0 and d_ff % bn == 0
    nj = d_ff // bn
    return pl.pallas_call(
        _ffn_in_kernel,
        grid=(m // bm, nj),
        in_specs=[
            pl.BlockSpec((bm, k), lambda i, j: (i, 0)),
            _resident((1, k), lambda i, j: (0, 0)),
            pl.BlockSpec((None, k, bn), lambda i, j: (layer, 0, j)),
            pl.BlockSpec((None, k, bn), lambda i, j: (layer, 0, j + nj)),
        ],
        out_specs=pl.BlockSpec((bm, bn), lambda i, j: (i, j)),
        out_shape=jax.ShapeDtypeStruct((m, d_ff), BF16),
        scratch_shapes=[pltpu.VMEM((bm, k), BF16)],
        compiler_params=_params("parallel", "arbitrary"),
        name="ffn_in",
    )(x, g.reshape(1, k), w, w)


def _mm_norm_res_kernel(*refs, n_lhs, nk):
    lhs = refs[:n_lhs]
    w_ref, g_ref, res_ref, o_ref = refs[n_lhs:n_lhs + 4]
    if nk == 1:
        g = g_ref[...]
        half = o_ref.shape[0] // 2
        for rows in (slice(0, half), slice(half, 2 * half)):
            y = None
            off = 0
            for a in lhs:
                wd = a.shape[1]
                part = jnp.dot(a[rows, :], w_ref[off:off + wd, :], preferred_element_type=F32)
                y = part if y is None else y + part
                off += wd
            o_ref[rows, :] = res_ref[rows, :] + _rms(y, g)
    else:
        k = pl.program_id(1)
        part = jnp.dot(lhs[0][...], w_ref[...], preferred_element_type=F32)

        @pl.when(k == 0)
        def _():
            o_ref[...] = part

        @pl.when(k > 0)
        def _():
            o_ref[...] += part

        @pl.when(k == nk - 1)
        def _():
            g = g_ref[...]

            def finish(c, _):
                rows = pl.ds(pl.multiple_of(c * ROW_CHUNK, ROW_CHUNK), ROW_CHUNK)
                o_ref[rows, :] = res_ref[rows, :] + _rms(o_ref[rows, :], g)
                return 0

            lax.fori_loop(0, o_ref.shape[0] // ROW_CHUNK, finish, 0)


def mm_norm_res(lhs_list, w, layer, g, res, bm, nk=1):
    m, n = res.shape
    kk = w.shape[1]
    assert m % bm == 0 and kk % nk == 0
    bk = kk // nk
    n_lhs = len(lhs_list)
    if nk == 1:
        lhs_specs = [pl.BlockSpec((bm, a.shape[1]), lambda i, k: (i, 0)) for a in lhs_list]
        w_spec = _resident((None, kk, n), lambda i, k: (layer, 0, 0))
    else:
        assert n_lhs == 1
        kblock = lambda i, k: jnp.where(i % 2 == 0, k, nk - 1 - k)
        lhs_specs = [pl.BlockSpec((bm, bk), lambda i, k: (i, kblock(i, k)))]
        w_spec = pl.BlockSpec((None, bk, n), lambda i, k: (layer, kblock(i, k), 0))
    return pl.pallas_call(
        functools.partial(_mm_norm_res_kernel, n_lhs=n_lhs, nk=nk),
        grid=(m // bm, nk),
        in_specs=lhs_specs + [
            w_spec,
            _resident((1, n), lambda i, k: (0, 0)),
            pl.BlockSpec((bm, n), lambda i, k: (i, 0)),
        ],
        out_specs=pl.BlockSpec((bm, n), lambda i, k: (i, 0)),
        out_shape=jax.ShapeDtypeStruct((m, n), F32),
        compiler_params=_params("parallel", "arbitrary"),
        name="mm_norm_res",
    )(*lhs_list, w, g.reshape(1, n), res)


def _gla_kernel(q_ref, k_ref, v_ref, z_ref, r_ref, w2_ref, bg_ref, gh_ref, tri_ref,
                o_ref, st_ref, *, dk, dv):
    tl = q_ref.shape[0]
    heads = q_ref.shape[1] // dk
    n_chunks = tl // GLA_CHUNK

    @pl.when(pl.program_id(2) == 0)
    def _():
        st_ref[...] = jnp.zeros_like(st_ref)

    pre = jnp.dot(z_ref[...].astype(BF16), w2_ref[...], preferred_element_type=F32) + bg_ref[...]
    logg = (jnp.minimum(pre, 0.0) - jnp.log1p(jnp.exp(-jnp.abs(pre)))) * (1.0 / GLA_TAU)
    hi = logg.astype(BF16)
    lo = (logg - hi.astype(F32)).astype(BF16)
    incl = tri_ref[...]
    cs = jnp.dot(incl, jnp.concatenate([hi, lo], axis=1), preferred_element_type=F32)
    bcum_all = cs[:, :heads * dk] + cs[:, heads * dk:]
    gh = gh_ref[...]

    for hh in range(heads):
        ks = slice(hh * dk, (hh + 1) * dk)
        vs = slice(hh * dv, (hh + 1) * dv)
        bcum = bcum_all[:, ks]
        b_last = [bcum[(c + 1) * GLA_CHUNK - 1:(c + 1) * GLA_CHUNK, :] for c in range(n_chunks)]
        suf = jnp.concatenate([jnp.broadcast_to(bl, (GLA_CHUNK, dk)) for bl in b_last], axis=0) - bcum

        qf = q_ref[:, ks].astype(F32) * dk ** -0.5
        kf = k_ref[:, ks].astype(F32)
        q_dec = (qf * jnp.exp(bcum)).astype(BF16)
        k_intra = (kf * jnp.exp(-bcum)).astype(BF16)
        k_state = (kf * jnp.exp(suf)).astype(BF16)
        v = v_ref[:, vs]

        a = lax.dot_general(q_dec, k_intra, NT_DIMS, preferred_element_type=F32)
        a = jnp.where(incl > 0, a, 0.0).astype(BF16)
        o_intra = jnp.dot(a, v, preferred_element_type=F32)

        st = st_ref[hh]
        outs = []
        for c in range(n_chunks):
            rows = slice(c * GLA_CHUNK, (c + 1) * GLA_CHUNK)
            o_inter = lax.dot_general(q_dec[rows], st.astype(BF16), NT_DIMS,
                                      preferred_element_type=F32)
            outs.append(o_intra[rows] + o_inter)
            decay = jnp.exp(b_last[c])
            upd = lax.dot_general(v[rows], k_state[rows], TN_DIMS, preferred_element_type=F32)
            st = decay * st + upd
        st_ref[hh] = st
        o = _rms(jnp.concatenate(outs, axis=0), gh)
        r = r_ref[:, vs]
        o_ref[:, vs] = (o * (r * (1.0 / (1.0 + jnp.exp(-r))))).astype(o_ref.dtype)


def gla(pa, pb, w2p, bgate, ghead, tri, tl, heads_per_step):
    b, l, _ = pa.shape
    dk = LANES
    dv = 2 * dk
    hp = heads_per_step
    assert GLA_HEADS % hp == 0
    groups = GLA_HEADS // hp
    wk, wv = hp * dk, hp * dv
    kb0 = GLA_HEADS * dk // wk
    vb0 = 2 * GLA_HEADS * dk // wv
    zb = GLA_HEADS * dv // LANES
    return pl.pallas_call(
        functools.partial(_gla_kernel, dk=dk, dv=dv),
        grid=(b, groups, l // tl),
        in_specs=[
            pl.BlockSpec((None, tl, wk), lambda bi, h, t: (bi, t, h)),
            pl.BlockSpec((None, tl, wk), lambda bi, h, t: (bi, t, kb0 + h)),
            pl.BlockSpec((None, tl, wv), lambda bi, h, t: (bi, t, vb0 + h)),
            pl.BlockSpec((None, tl, LANES), lambda bi, h, t: (bi, t, zb)),
            pl.BlockSpec((None, tl, wv), lambda bi, h, t: (bi, t, h)),
            pl.BlockSpec((LANES, wk), lambda bi, h, t: (0, h)),
            pl.BlockSpec((1, wk), lambda bi, h, t: (0, h)),
            _resident((1, dv), lambda bi, h, t: (0, 0)),
            _resident((tl, tl), lambda bi, h, t: (0, 0)),
        ],
        out_specs=pl.BlockSpec((None, tl, wv), lambda bi, h, t: (bi, t, h)),
        out_shape=jax.ShapeDtypeStruct((b, l, GLA_HEADS * dv), BF16),
        scratch_shapes=[pltpu.VMEM((hp, dv, dk), F32)],
        compiler_params=_params("parallel", "parallel", "arbitrary"),
        name="gla",
    )(pa, pa, pa, pb, pb, w2p, bgate, ghead, tri)


def _for_each_block_paired(n, body):
    def pair(p, _):
        body(2 * p)
        body(2 * p + 1)
        return 0

    lax.fori_loop(0, n >> 1, pair, 0)

    @pl.when(n & 1 == 1)
    def _():
        body(n - 1)


def _dsa_kernel(q_ref, iq_ref, iw_ref, k_ref, v_ref, ik_ref, corr_ref, o_ref,
                key_ref, hi_ref, lo_ref, madd_ref, wt_ref, m_ref, l_ref, acc_ref, vt_ref, s_ref,
                *, topk):
    tq = q_ref.shape[0]
    tk = tq
    dh = LANES
    sub = 8
    i = pl.program_id(1)
    nkb = i + 1
    scale = dh ** -0.5

    wt_ref[...] = (iw_ref[...] * (IDX_HEADS ** -0.5 * IDX_DH ** -0.5)).T
    qpos = lax.broadcasted_iota(jnp.int32, (tk, tq), 1) + i * tq
    kpos0 = lax.broadcasted_iota(jnp.int32, (tk, tq), 0)

    def score_block(kb):
        ik = ik_ref[pl.ds(pl.multiple_of(kb * tk, tk), tk), :].astype(BF16)
        ik_even, ik_odd = ik[:, :LANES], ik[:, LANES:]
        acc = jnp.zeros((tk, tq), F32)
        for p in range(IDX_HEADS // 2):
            qp = iq_ref[:, p * LANES:(p + 1) * LANES]
            for par, ikx in ((0, ik_even), (1, ik_odd)):
                h = 2 * p + par
                d = lax.dot_general(ikx, qp, NT_DIMS, preferred_element_type=F32)
                acc = acc + jnp.maximum(d, 0.0) * wt_ref[h:h + 1, :]
        sc = jnp.where(kpos0 + kb * tk <= qpos, acc, -jnp.inf)
        bits = lax.bitcast_convert_type(sc, jnp.int32)
        key = bits ^ ((bits >> 31) & 0x7FFFFFFF)
        key_ref[kb] = key
        hi_ref[kb] = (key >> 16).astype(jnp.int16)
        lo_ref[kb] = ((key & 0xFFFF) - HALF16).astype(jnp.int16)

    _for_each_block_paired(nkb, score_block)

    n_acc = 4
    slabs = tk // PACK16
    one16, zero16 = jnp.int16(1), jnp.int16(0)

    def rows16(x):
        return jnp.broadcast_to(x, (PACK16, tq)).astype(jnp.int16)

    def count16(src_ref, cand, strict):
        cand16 = rows16(cand)

        def body(kb, cnts):
            cnts = list(cnts)
            for r in range(slabs):
                blk = src_ref[kb, r * PACK16:(r + 1) * PACK16, :]
                hit = (blk > cand16) if strict else (blk >= cand16)
                cnts[r % n_acc] = cnts[r % n_acc] + jnp.where(hit, one16, zero16)
            return tuple(cnts)

        cnts = lax.fori_loop(0, nkb, body, (jnp.zeros((PACK16, tq), jnp.int16),) * n_acc)
        cnt = ((cnts[0] + cnts[1]) + (cnts[2] + cnts[3])).astype(jnp.int32)
        return jnp.sum(cnt, axis=0, keepdims=True)

    def bisect16(src_ref, kth):
        def step(s, t):
            cand = t + lax.shift_left(jnp.int32(1), 15 - s)
            return jnp.where(count16(src_ref, cand, False) >= kth, cand, t)

        return lax.fori_loop(0, 16, step, jnp.full((1, tq), -HALF16, jnp.int32))

    t_hi = bisect16(hi_ref, topk)
    above = count16(hi_ref, t_hi, True)
    t_hi16 = rows16(t_hi)

    def keep_threshold_bucket(kb, _):
        for r in range(slabs):
            rows = slice(r * PACK16, (r + 1) * PACK16)
            lo_ref[kb, rows, :] = jnp.where(hi_ref[kb, rows, :] == t_hi16, lo_ref[kb, rows, :],
                                            jnp.int16(-HALF16))
        return 0

    lax.fori_loop(0, nkb, keep_threshold_bucket, 0)
    t_lo = bisect16(lo_ref, topk - above)
    thr = t_hi * (2 * HALF16) + (t_lo + HALF16)

    def mask_block(kb, _):
        vis = kpos0 + kb * tk <= qpos
        madd_ref[kb] = jnp.where(vis, jnp.where(key_ref[kb] >= thr, 0.0, NEG_BIG), NEG_BIG)
        return 0

    lax.fori_loop(0, nkb, mask_block, 0)

    @pl.when(i == 0)
    def _():
        def transpose_values(kb, _):
            vt_ref[kb] = v_ref[pl.ds(pl.multiple_of(kb * tk, tk), tk), :].T
            return 0

        lax.fori_loop(0, vt_ref.shape[0], transpose_values, 0)

    m_ref[...] = jnp.full(m_ref.shape, NEG_BIG, F32)
    l_ref[...] = jnp.zeros(l_ref.shape, F32)
    acc_ref[...] = jnp.zeros(acc_ref.shape, F32)

    def attend(kb, near, buf=0):
        keys = pl.ds(pl.multiple_of(kb * tk, tk), tk)
        madd = madd_ref[kb]
        for h in range(DSA_HEADS):
            hs = slice(h * dh, (h + 1) * dh)
            s_ref[buf, h] = lax.dot_general(k_ref[keys, hs], q_ref[:, hs], NT_DIMS,
                                            preferred_element_type=F32)
        for h in range(DSA_HEADS):
            hs = slice(h * dh, (h + 1) * dh)
            s = s_ref[buf, h] * (scale * LOG2E)
            if near is not None:
                s = s + corr_ref[h, near]
            s = s + madd
            m_prev = m_ref[h, 0:1, :]
            m_next = jnp.maximum(m_prev, jnp.max(s, axis=0, keepdims=True))
            alpha = jnp.exp2(m_prev - m_next)
            p = jnp.exp2(s - m_next)
            l_next = alpha * l_ref[h, 0:1, :] + jnp.sum(p, axis=0, keepdims=True)
            pv = jnp.dot(vt_ref[kb, hs, :], p.astype(BF16),
                         preferred_element_type=F32)
            acc_ref[h] = alpha * acc_ref[h] + pv
            m_ref[h] = jnp.broadcast_to(m_next, (sub, tq))
            l_ref[h] = jnp.broadcast_to(l_next, (sub, tq))

    def far_pair(p, _):
        attend(2 * p, None, 0)
        attend(2 * p + 1, None, 1)
        return 0

    n_far = jnp.maximum(i - 1, 0)
    lax.fori_loop(0, n_far >> 1, far_pair, 0)

    @pl.when(n_far & 1 == 1)
    def _():
        attend(n_far - 1, None)

    @pl.when(i >= 1)
    def _():
        attend(i - 1, 1, 0)
        attend(i, 0, 1)

    @pl.when(i == 0)
    def _():
        attend(i, 0)

    for h in range(DSA_HEADS):
        o_h = acc_ref[h] / l_ref[h, 0:1, :]
        o_ref[:, h * dh:(h + 1) * dh] = o_h.T.astype(o_ref.dtype)


def dsa(pa, pb, corr, tq):
    b, l, _ = pa.shape
    dh = LANES
    hd = DSA_HEADS * dh
    qb = (2 * GLA_HEADS * LANES + GLA_HEADS * 2 * LANES) // hd
    iwb = GLA_HEADS * 2 * LANES // LANES + 1
    ikb = (iwb + 1) * LANES // (2 * LANES)
    topk = min(TOPK_MAX, l // 4)
    assert l % tq == 0 and tq >= topk and tq >= REL_MAX_DIST
    nkb = l // tq
    assert nkb * tq // PACK16 < HALF16
    return pl.pallas_call(
        functools.partial(_dsa_kernel, topk=topk),
        grid=(b, l // tq),
        in_specs=[
            pl.BlockSpec((None, tq, hd), lambda bi, i: (bi, i, qb)),
            pl.BlockSpec((None, tq, IDX_HEADS * IDX_DH), lambda bi, i: (bi, i, qb + 3)),
            pl.BlockSpec((None, tq, LANES), lambda bi, i: (bi, i, iwb)),
            _resident((None, l, hd), lambda bi, i: (bi, 0, qb + 1)),
            _resident((None, l, hd), lambda bi, i: (bi, 0, qb + 2)),
            _resident((None, l, 2 * LANES), lambda bi, i: (bi, 0, ikb)),
            _resident((DSA_HEADS, 2, tq, tq), lambda bi, i: (0, 0, 0, 0)),
        ],
        out_specs=pl.BlockSpec((None, tq, hd), lambda bi, i: (bi, i, 0)),
        out_shape=jax.ShapeDtypeStruct((b, l, hd), BF16),
        scratch_shapes=[
            pltpu.VMEM((nkb, tq, tq), jnp.int32),
            pltpu.VMEM((nkb, tq, tq), jnp.int16),
            pltpu.VMEM((nkb, tq, tq), jnp.int16),
            pltpu.VMEM((nkb, tq, tq), F32),
            pltpu.VMEM((LANES, tq), F32),
            pltpu.VMEM((DSA_HEADS, 8, tq), F32),
            pltpu.VMEM((DSA_HEADS, 8, tq), F32),
            pltpu.VMEM((DSA_HEADS, dh, tq), F32),
            pltpu.VMEM((nkb, hd, tq), BF16),
            pltpu.VMEM((2, DSA_HEADS, tq, tq), F32),
        ],
        compiler_params=_params("parallel", "arbitrary"),
        name="dsa",
    )(pa, pa, pb, pa, pa, pb, corr)


def _xattn_kernel(x_ref, gq_ref, wq_ref, kv_ref, wo_ref, go_ref, o_ref, h_ref, q_ref, a_ref):
    d = x_ref.shape[1]
    dh = d // XA_HEADS
    scale = dh ** -0.5
    _rms_to(x_ref, gq_ref, h_ref)
    q_ref[...] = jnp.dot(h_ref[...], wq_ref[...], preferred_element_type=F32).astype(q_ref.dtype)
    for h in range(XA_HEADS):
        hs = slice(h * dh, (h + 1) * dh)
        s = lax.dot_general(q_ref[:, hs], kv_ref[:, hs], NT_DIMS, preferred_element_type=F32) * scale
        e = jnp.exp(s - jnp.max(s, axis=-1, keepdims=True))
        o = jnp.dot(e.astype(BF16), kv_ref[:, d + h * dh:d + (h + 1) * dh], preferred_element_type=F32)
        a_ref[:, hs] = (o / jnp.sum(e, axis=-1, keepdims=True)).astype(a_ref.dtype)
    y = jnp.dot(a_ref[...], wo_ref[...], preferred_element_type=F32)
    o_ref[...] = x_ref[...] + _rms(y, go_ref[...])


def xattn(x, kv, gq, wq, go, wo, layer, tq):
    b, l, d = x.shape
    nm = kv.shape[1]
    vec = lambda: _resident((1, d), lambda bi, i: (0, 0))
    mat = lambda: _resident((None, d, d), lambda bi, i: (layer, 0, 0))
    return pl.pallas_call(
        _xattn_kernel,
        grid=(b, l // tq),
        in_specs=[
            pl.BlockSpec((None, tq, d), lambda bi, i: (bi, i, 0)),
            vec(), mat(),
            pl.BlockSpec((None, nm, 2 * d), lambda bi, i: (bi, 0, 0)),
            mat(), vec(),
        ],
        out_specs=pl.BlockSpec((None, tq, d), lambda bi, i: (bi, i, 0)),
        out_shape=jax.ShapeDtypeStruct((b, l, d), F32),
        scratch_shapes=[pltpu.VMEM((tq, d), BF16)] * 3,
        compiler_params=_params("parallel", "parallel"),
        name="xattn",
    )(x, gq.reshape(1, d), wq, kv, wo, go.reshape(1, d))


def _rel_bucket(dist):
    n = jnp.maximum(dist, 0)
    max_exact = REL_BUCKETS // 2
    nf = jnp.maximum(n, 1).astype(F32)
    large = max_exact + (jnp.log(nf / max_exact) / math.log(REL_MAX_DIST / max_exact)
                         * (REL_BUCKETS - max_exact)).astype(jnp.int32)
    large = jnp.minimum(large, REL_BUCKETS - 1)
    return jnp.where(n < max_exact, n, large)


def _bias_correction_tiles(rel_table, tq):
    t = jnp.arange(tq)
    d0 = t[:, None] - t[None, :]
    buckets = jnp.stack([_rel_bucket(d0), _rel_bucket(d0 + tq)], axis=0)
    onehot = (buckets[..., None] == jnp.arange(REL_BUCKETS)).astype(F32)
    far = rel_table[_rel_bucket(jnp.int32(2 * tq))]
    return jnp.einsum("ntsk,kh->hnst", onehot, (rel_table - far) * LOG2E,
                      precision=lax.Precision.HIGHEST)


def _chunk_sum_matrices(tl):
    r = jnp.arange(tl)
    same = (r[:, None] // GLA_CHUNK) == (r[None, :] // GLA_CHUNK)
    return (same & (r[None, :] <= r[:, None])).astype(BF16)


def _split_in_proj(w):
    depth, d, _ = w.shape
    half = d // 2
    sizes = [GLA_HEADS * LANES, GLA_HEADS * LANES, half, GLA_RANK, half,
             half, half, half, IDX_HEADS * IDX_DH, IDX_DH, IDX_HEADS]
    offs = [0]
    for s in sizes:
        offs.append(offs[-1] + s)
    assert offs[-1] == w.shape[2]
    seg = lambda n: w[:, :, offs[n]:offs[n + 1]]
    zero = lambda n: jnp.zeros((depth, d, n), w.dtype)
    ik = seg(9)
    w_head = w[:, :, :offs[3]]
    w_mid = w[:, :, offs[5]:offs[9]]
    w_small = jnp.concatenate([seg(4), seg(3), zero(LANES - GLA_RANK), seg(10), zero(LANES - IDX_HEADS),
                               ik, zero(2 * IDX_DH), ik], axis=2)
    return w_head.astype(BF16), w_mid.astype(BF16), w_small.astype(BF16)


def kernel(x, mem, norm_gains, w_in, w_gate_up, b_gate, gla_head_gain, rel_table,
           w_out, w_xq, w_xkv, w_xo, w_ffn_in, w_ffn_out):
    b, l, d = x.shape
    nm = mem.shape[1]
    depth = w_in.shape[0]
    d_ff = w_ffn_out.shape[1]
    m = b * l
    assert d // 2 // GLA_HEADS == 2 * LANES and d // 2 // DSA_HEADS == LANES

    tq_dsa = 256
    tl_gla = 512
    corr = _bias_correction_tiles(rel_table, tq_dsa)
    tri = _chunk_sum_matrices(tl_gla)
    w2p = jnp.pad(w_gate_up, ((0, 0), (0, LANES - GLA_RANK), (0, 0))).astype(BF16)

    w_out, w_xq, w_xkv, w_xo, w_ffn_in, w_ffn_out = (
        w.astype(BF16) for w in (w_out, w_xq, w_xkv, w_xo, w_ffn_in, w_ffn_out))
    w_head, w_mid, w_small = _split_in_proj(w_in)

    xf = x.reshape(m, d)
    memf = mem.reshape(b * nm, d)
    for li in range(depth):
        g = norm_gains[li]
        pa, pb = in_proj(xf, g[0], w_head, w_mid, w_small, li, bm=1024, bn=1024, bs=512)
        pa, pb = pa.reshape(b, l, -1), pb.reshape(b, l, -1)
        o_gla = gla(pa, pb, w2p[li], b_gate[li].reshape(1, -1),
                    gla_head_gain[li].reshape(1, -1), tri, tl_gla, heads_per_step=2)
        o_dsa = dsa(pa, pb, corr, tq_dsa)
        xf = mm_norm_res([o_gla.reshape(m, -1), o_dsa.reshape(m, -1)], w_out, li, g[1], xf, bm=512)
        kv = norm_matmul(memf, g[6], w_xkv, li, BF16, bm=min(512, b * nm), bn=1024)
        xf = xattn(xf.reshape(b, l, d), kv.reshape(b, nm, 2 * d), g[2], w_xq, g[3], w_xo, li,
                   tq=512).reshape(m, d)
        act = ffn_in(xf, g[4], w_ffn_in, li, d_ff, bm=1024, bn=512)
        xf = mm_norm_res([act], w_ffn_out, li, g[5], xf, bm=512)
    return xf.reshape(b, l, d)
```

```python
import functools
import math

import jax
import jax.numpy as jnp
from jax import lax
from jax.experimental import pallas as pl
from jax.experimental.pallas import tpu as pltpu

F32 = jnp.float32
BF16 = jnp.bfloat16

EPS = 1e-6
GLA_HEADS = 4
GLA_RANK = 16
GLA_TAU = 16.0
GLA_CHUNK = 64
DSA_HEADS = 8
IDX_HEADS = 16
IDX_DH = 64
TOPK_MAX = 256
REL_BUCKETS = 32
REL_MAX_DIST = 128
XA_HEADS = 4

LANES = 128
VMEM_LIMIT_BYTES = 60000 * 1024

PACK16 = 16
ROW_CHUNK = 128
NORM_PARTS = 4

NEG_BIG = -1e30
HALF16 = 2 ** 15
LOG2E = 1.4426950408889634

NT_DIMS = (((1,), (1,)), ((), ()))
TN_DIMS = (((0,), (0,)), ((), ()))


def _params(*sem):
    return pltpu.CompilerParams(dimension_semantics=sem, vmem_limit_bytes=VMEM_LIMIT_BYTES)


def _resident(shape, index_map):
    return pl.BlockSpec(shape, index_map, pipeline_mode=pl.Buffered(1))


def _rms(x, g):
    return x * lax.rsqrt(jnp.mean(x * x, axis=-1, keepdims=True) + EPS) * g


def _row_parts(n_rows, parts):
    size = n_rows // parts
    return [slice(p * size, (p + 1) * size) for p in range(parts)]


def _rms_to(x_ref, g_ref, h_ref):
    g = g_ref[...]

    def body(c, _):
        rows = pl.ds(pl.multiple_of(c * ROW_CHUNK, ROW_CHUNK), ROW_CHUNK)
        h_ref[rows, :] = _rms(x_ref[rows, :], g).astype(h_ref.dtype)
        return 0

    lax.fori_loop(0, x_ref.shape[0] // ROW_CHUNK, body, 0)


def _norm_matmul_kernel(x_ref, g_ref, w_ref, o_ref, h_ref):
    @pl.when(pl.program_id(1) == 0)
    def _():
        _rms_to(x_ref, g_ref, h_ref)

    o_ref[...] = jnp.dot(h_ref[...], w_ref[...], preferred_element_type=F32).astype(o_ref.dtype)


def norm_matmul(x, g, w, layer, out_dtype, bm, bn):
    m, k = x.shape
    n = w.shape[2]
    assert m % bm == 0 and n % bn == 0
    return pl.pallas_call(
        _norm_matmul_kernel,
        grid=(m // bm, n // bn),
        in_specs=[
            pl.BlockSpec((bm, k), lambda i, j: (i, 0)),
            _resident((1, k), lambda i, j: (0, 0)),
            pl.BlockSpec((None, k, bn), lambda i, j: (layer, 0, j)),
        ],
        out_specs=pl.BlockSpec((bm, bn), lambda i, j: (i, j)),
        out_shape=jax.ShapeDtypeStruct((m, n), out_dtype),
        scratch_shapes=[pltpu.VMEM((bm, k), BF16)],
        compiler_params=_params("parallel", "arbitrary"),
        name="norm_matmul",
    )(x, g.reshape(1, k), w)


def _in_proj_slot(i, j, n_slots):
    return jnp.where(i % 2 == 0, j, n_slots - 1 - j)


def _in_proj_kernel(x_ref, g_ref, wh_ref, wm_ref, ws_ref, oa_ref, ob_ref, h_ref, *, nh, ns, nm):
    s = _in_proj_slot(pl.program_id(0), pl.program_id(1), nh + ns + nm)

    first = pl.program_id(1) == 0

    def project(w_ref, o_ref):
        o_ref[...] = jnp.dot(h_ref[...], w_ref[...], preferred_element_type=F32).astype(o_ref.dtype)

    def project_first(w_ref, o_ref):
        g = g_ref[...]
        for rows in _row_parts(x_ref.shape[0], NORM_PARTS):
            h = _rms(x_ref[rows, :], g).astype(h_ref.dtype)
            h_ref[rows, :] = h
            o_ref[rows, :] = jnp.dot(h, w_ref[...], preferred_element_type=F32).astype(o_ref.dtype)

    pl.when(first & (s < nh))(lambda: project_first(wh_ref, oa_ref))
    pl.when(first & (s >= nh + ns))(lambda: project_first(wm_ref, oa_ref))
    pl.when(jnp.logical_not(first) & (s < nh))(lambda: project(wh_ref, oa_ref))
    pl.when((s >= nh) & (s < nh + ns))(lambda: project(ws_ref, ob_ref))
    pl.when(jnp.logical_not(first) & (s >= nh + ns))(lambda: project(wm_ref, oa_ref))


def in_proj(x, g, w_head, w_mid, w_small, layer, bm, bn, bs):
    m, k = x.shape
    n_head, n_mid, n_small = w_head.shape[2], w_mid.shape[2], w_small.shape[2]
    assert m % bm == 0 and n_head % bn == 0 and n_mid % bn == 0 and n_small % bs == 0
    nh, nm, ns = n_head // bn, n_mid // bn, n_small // bs
    n_slots = nh + ns + nm

    def slot(i, j):
        return _in_proj_slot(i, j, n_slots)

    def pa_block(i, j):
        s = slot(i, j)
        hold = jnp.where(i % 2 == 0, nh - 1, nh)
        return jnp.where(s < nh, s, jnp.where(s >= nh + ns, s - ns, hold))

    return pl.pallas_call(
        functools.partial(_in_proj_kernel, nh=nh, ns=ns, nm=nm),
        grid=(m // bm, n_slots),
        in_specs=[
            pl.BlockSpec((bm, k), lambda i, j: (i, 0)),
            _resident((1, k), lambda i, j: (0, 0)),
            pl.BlockSpec((None, k, bn), lambda i, j: (layer, 0, jnp.clip(slot(i, j), 0, nh - 1))),
            pl.BlockSpec((None, k, bn), lambda i, j: (layer, 0, jnp.clip(slot(i, j) - nh - ns, 0, nm - 1))),
            pl.BlockSpec((None, k, bs), lambda i, j: (layer, 0, jnp.clip(slot(i, j) - nh, 0, ns - 1))),
        ],
        out_specs=[
            pl.BlockSpec((bm, bn), lambda i, j: (i, pa_block(i, j))),
            pl.BlockSpec((bm, bs), lambda i, j: (i, jnp.clip(slot(i, j) - nh, 0, ns - 1))),
        ],
        out_shape=[
            jax.ShapeDtypeStruct((m, n_head + n_mid), BF16),
            jax.ShapeDtypeStruct((m, n_small), F32),
        ],
        scratch_shapes=[pltpu.VMEM((bm, k), BF16)],
        compiler_params=_params("parallel", "arbitrary"),
        name="in_proj",
    )(x, g.reshape(1, k), w_head, w_mid, w_small)


def _ffn_in_kernel(x_ref, g_ref, wg_ref, wu_ref, o_ref, h_ref):
    def gated(h):
        a = jnp.dot(h, wg_ref[...], preferred_element_type=F32)
        b = jnp.dot(h, wu_ref[...], preferred_element_type=F32)
        return (a * (1.0 / (1.0 + jnp.exp(-a))) * b).astype(o_ref.dtype)

    @pl.when(pl.program_id(1) == 0)
    def _():
        g = g_ref[...]
        for rows in _row_parts(x_ref.shape[0], NORM_PARTS):
            h = _rms(x_ref[rows, :], g).astype(h_ref.dtype)
            h_ref[rows, :] = h
            o_ref[rows, :] = gated(h)

    @pl.when(pl.program_id(1) > 0)
    def _():
        o_ref[...] = gated(h_ref[...])


def ffn_in(x, g, w, layer, d_ff, bm, bn):
    m, k = x.shape
    assert m % bm == 0 and d_ff % bn == 0
    nj = d_ff // bn
    return pl.pallas_call(
        _ffn_in_kernel,
        grid=(m // bm, nj),
        in_specs=[
            pl.BlockSpec((bm, k), lambda i, j: (i, 0)),
            _resident((1, k), lambda i, j: (0, 0)),
            pl.BlockSpec((None, k, bn), lambda i, j: (layer, 0, j)),
            pl.BlockSpec((None, k, bn), lambda i, j: (layer, 0, j + nj)),
        ],
        out_specs=pl.BlockSpec((bm, bn), lambda i, j: (i, j)),
        out_shape=jax.ShapeDtypeStruct((m, d_ff), BF16),
        scratch_shapes=[pltpu.VMEM((bm, k), BF16)],
        compiler_params=_params("parallel", "arbitrary"),
        name="ffn_in",
    )(x, g.reshape(1, k), w, w)


def _mm_norm_res_kernel(*refs, n_lhs, nk):
    lhs = refs[:n_lhs]
    w_ref, g_ref, res_ref, o_ref = refs[n_lhs:n_lhs + 4]
    if nk == 1:
        g = g_ref[...]
        half = o_ref.shape[0] // 2
        for rows in (slice(0, half), slice(half, 2 * half)):
            y = None
            off = 0
            for a in lhs:
                wd = a.shape[1]
                part = jnp.dot(a[rows, :], w_ref[off:off + wd, :], preferred_element_type=F32)
                y = part if y is None else y + part
                off += wd
            o_ref[rows, :] = res_ref[rows, :] + _rms(y, g)
    else:
        k = pl.program_id(1)
        part = jnp.dot(lhs[0][...], w_ref[...], preferred_element_type=F32)

        @pl.when(k == 0)
        def _():
            o_ref[...] = part

        @pl.when(k > 0)
        def _():
            o_ref[...] += part

        @pl.when(k == nk - 1)
        def _():
            g = g_ref[...]

            def finish(c, _):
                rows = pl.ds(pl.multiple_of(c * ROW_CHUNK, ROW_CHUNK), ROW_CHUNK)
                o_ref[rows, :] = res_ref[rows, :] + _rms(o_ref[rows, :], g)
                return 0

            lax.fori_loop(0, o_ref.shape[0] // ROW_CHUNK, finish, 0)


def mm_norm_res(lhs_list, w, layer, g, res, bm, nk=1):
    m, n = res.shape
    kk = w.shape[1]
    assert m % bm == 0 and kk % nk == 0
    bk = kk // nk
    n_lhs = len(lhs_list)
    if nk == 1:
        lhs_specs = [pl.BlockSpec((bm, a.shape[1]), lambda i, k: (i, 0)) for a in lhs_list]
        w_spec = _resident((None, kk, n), lambda i, k: (layer, 0, 0))
    else:
        assert n_lhs == 1
        kblock = lambda i, k: jnp.where(i % 2 == 0, k, nk - 1 - k)
        lhs_specs = [pl.BlockSpec((bm, bk), lambda i, k: (i, kblock(i, k)))]
        w_spec = pl.BlockSpec((None, bk, n), lambda i, k: (layer, kblock(i, k), 0))
    return pl.pallas_call(
        functools.partial(_mm_norm_res_kernel, n_lhs=n_lhs, nk=nk),
        grid=(m // bm, nk),
        in_specs=lhs_specs + [
            w_spec,
            _resident((1, n), lambda i, k: (0, 0)),
            pl.BlockSpec((bm, n), lambda i, k: (i, 0)),
        ],
        out_specs=pl.BlockSpec((bm, n), lambda i, k: (i, 0)),
        out_shape=jax.ShapeDtypeStruct((m, n), F32),
        compiler_params=_params("parallel", "arbitrary"),
        name="mm_norm_res",
    )(*lhs_list, w, g.reshape(1, n), res)


def _gla_kernel(q_ref, k_ref, v_ref, z_ref, r_ref, w2_ref, bg_ref, gh_ref, tri_ref,
                o_ref, st_ref, *, dk, dv):
    tl = q_ref.shape[0]
    heads = q_ref.shape[1] // dk
    n_chunks = tl // GLA_CHUNK

    @pl.when(pl.program_id(2) == 0)
    def _():
        st_ref[...] = jnp.zeros_like(st_ref)

    pre = jnp.dot(z_ref[...].astype(BF16), w2_ref[...], preferred_element_type=F32) + bg_ref[...]
    logg = (jnp.minimum(pre, 0.0) - jnp.log1p(jnp.exp(-jnp.abs(pre)))) * (1.0 / GLA_TAU)
    hi = logg.astype(BF16)
    lo = (logg - hi.astype(F32)).astype(BF16)
    incl = tri_ref[...]
    cs = jnp.dot(incl, jnp.concatenate([hi, lo], axis=1), preferred_element_type=F32)
    bcum_all = cs[:, :heads * dk] + cs[:, heads * dk:]
    gh = gh_ref[...]

    for hh in range(heads):
        ks = slice(hh * dk, (hh + 1) * dk)
        vs = slice(hh * dv, (hh + 1) * dv)
        bcum = bcum_all[:, ks]
        b_last = [bcum[(c + 1) * GLA_CHUNK - 1:(c + 1) * GLA_CHUNK, :] for c in range(n_chunks)]
        suf = jnp.concatenate([jnp.broadcast_to(bl, (GLA_CHUNK, dk)) for bl in b_last], axis=0) - bcum

        qf = q_ref[:, ks].astype(F32) * dk ** -0.5
        kf = k_ref[:, ks].astype(F32)
        q_dec = (qf * jnp.exp(bcum)).astype(BF16)
        k_intra = (kf * jnp.exp(-bcum)).astype(BF16)
        k_state = (kf * jnp.exp(suf)).astype(BF16)
        v = v_ref[:, vs]

        a = lax.dot_general(q_dec, k_intra, NT_DIMS, preferred_element_type=F32)
        a = jnp.where(incl > 0, a, 0.0).astype(BF16)
        o_intra = jnp.dot(a, v, preferred_element_type=F32)

        st = st_ref[hh]
        outs = []
        for c in range(n_chunks):
            rows = slice(c * GLA_CHUNK, (c + 1) * GLA_CHUNK)
            o_inter = lax.dot_general(q_dec[rows], st.astype(BF16), NT_DIMS,
                                      preferred_element_type=F32)
            outs.append(o_intra[rows] + o_inter)
            decay = jnp.exp(b_last[c])
            upd = lax.dot_general(v[rows], k_state[rows], TN_DIMS, preferred_element_type=F32)
            st = decay * st + upd
        st_ref[hh] = st
        o = _rms(jnp.concatenate(outs, axis=0), gh)
        r = r_ref[:, vs]
        o_ref[:, vs] = (o * (r * (1.0 / (1.0 + jnp.exp(-r))))).astype(o_ref.dtype)


def gla(pa, pb, w2p, bgate, ghead, tri, tl, heads_per_step):
    b, l, _ = pa.shape
    dk = LANES
    dv = 2 * dk
    hp = heads_per_step
    assert GLA_HEADS % hp == 0
    groups = GLA_HEADS // hp
    wk, wv = hp * dk, hp * dv
    kb0 = GLA_HEADS * dk // wk
    vb0 = 2 * GLA_HEADS * dk // wv
    zb = GLA_HEADS * dv // LANES
    return pl.pallas_call(
        functools.partial(_gla_kernel, dk=dk, dv=dv),
        grid=(b, groups, l // tl),
        in_specs=[
            pl.BlockSpec((None, tl, wk), lambda bi, h, t: (bi, t, h)),
            pl.BlockSpec((None, tl, wk), lambda bi, h, t: (bi, t, kb0 + h)),
            pl.BlockSpec((None, tl, wv), lambda bi, h, t: (bi, t, vb0 + h)),
            pl.BlockSpec((None, tl, LANES), lambda bi, h, t: (bi, t, zb)),
            pl.BlockSpec((None, tl, wv), lambda bi, h, t: (bi, t, h)),
            pl.BlockSpec((LANES, wk), lambda bi, h, t: (0, h)),
            pl.BlockSpec((1, wk), lambda bi, h, t: (0, h)),
            _resident((1, dv), lambda bi, h, t: (0, 0)),
            _resident((tl, tl), lambda bi, h, t: (0, 0)),
        ],
        out_specs=pl.BlockSpec((None, tl, wv), lambda bi, h, t: (bi, t, h)),
        out_shape=jax.ShapeDtypeStruct((b, l, GLA_HEADS * dv), BF16),
        scratch_shapes=[pltpu.VMEM((hp, dv, dk), F32)],
        compiler_params=_params("parallel", "parallel", "arbitrary"),
        name="gla",
    )(pa, pa, pa, pb, pb, w2p, bgate, ghead, tri)


def _for_each_block_paired(n, body):
    def pair(p, _):
        body(2 * p)
        body(2 * p + 1)
        return 0

    lax.fori_loop(0, n >> 1, pair, 0)

    @pl.when(n & 1 == 1)
    def _():
        body(n - 1)


def _dsa_kernel(q_ref, iq_ref, iw_ref, k_ref, v_ref, ik_ref, corr_ref, o_ref,
                key_ref, hi_ref, lo_ref, madd_ref, wt_ref, m_ref, l_ref, acc_ref, vt_ref, s_ref,
                *, topk):
    tq = q_ref.shape[0]
    tk = tq
    dh = LANES
    sub = 8
    i = pl.program_id(1)
    nkb = i + 1
    scale = dh ** -0.5

    wt_ref[...] = (iw_ref[...] * (IDX_HEADS ** -0.5 * IDX_DH ** -0.5)).T
    qpos = lax.broadcasted_iota(jnp.int32, (tk, tq), 1) + i * tq
    kpos0 = lax.broadcasted_iota(jnp.int32, (tk, tq), 0)

    def score_block(kb):
        ik = ik_ref[pl.ds(pl.multiple_of(kb * tk, tk), tk), :].astype(BF16)
        ik_even, ik_odd = ik[:, :LANES], ik[:, LANES:]
        acc = jnp.zeros((tk, tq), F32)
        for p in range(IDX_HEADS // 2):
            qp = iq_ref[:, p * LANES:(p + 1) * LANES]
            for par, ikx in ((0, ik_even), (1, ik_odd)):
                h = 2 * p + par
                d = lax.dot_general(ikx, qp, NT_DIMS, preferred_element_type=F32)
                acc = acc + jnp.maximum(d, 0.0) * wt_ref[h:h + 1, :]
        sc = jnp.where(kpos0 + kb * tk <= qpos, acc, -jnp.inf)
        bits = lax.bitcast_convert_type(sc, jnp.int32)
        key = bits ^ ((bits >> 31) & 0x7FFFFFFF)
        key_ref[kb] = key
        hi_ref[kb] = (key >> 16).astype(jnp.int16)
        lo_ref[kb] = ((key & 0xFFFF) - HALF16).astype(jnp.int16)

    _for_each_block_paired(nkb, score_block)

    n_acc = 4
    slabs = tk // PACK16
    one16, zero16 = jnp.int16(1), jnp.int16(0)

    def rows16(x):
        return jnp.broadcast_to(x, (PACK16, tq)).astype(jnp.int16)

    def count16(src_ref, cand, strict):
        cand16 = rows16(cand)

        def body(kb, cnts):
            cnts = list(cnts)
            for r in range(slabs):
                blk = src_ref[kb, r * PACK16:(r + 1) * PACK16, :]
                hit = (blk > cand16) if strict else (blk >= cand16)
                cnts[r % n_acc] = cnts[r % n_acc] + jnp.where(hit, one16, zero16)
            return tuple(cnts)

        cnts = lax.fori_loop(0, nkb, body, (jnp.zeros((PACK16, tq), jnp.int16),) * n_acc)
        cnt = ((cnts[0] + cnts[1]) + (cnts[2] + cnts[3])).astype(jnp.int32)
        return jnp.sum(cnt, axis=0, keepdims=True)

    def bisect16(src_ref, kth):
        def step(s, t):
            cand = t + lax.shift_left(jnp.int32(1), 15 - s)
            return jnp.where(count16(src_ref, cand, False) >= kth, cand, t)

        return lax.fori_loop(0, 16, step, jnp.full((1, tq), -HALF16, jnp.int32))

    t_hi = bisect16(hi_ref, topk)
    above = count16(hi_ref, t_hi, True)
    t_hi16 = rows16(t_hi)

    def keep_threshold_bucket(kb, _):
        for r in range(slabs):
            rows = slice(r * PACK16, (r + 1) * PACK16)
            lo_ref[kb, rows, :] = jnp.where(hi_ref[kb, rows, :] == t_hi16, lo_ref[kb, rows, :],
                                            jnp.int16(-HALF16))
        return 0

    lax.fori_loop(0, nkb, keep_threshold_bucket, 0)
    t_lo = bisect16(lo_ref, topk - above)
    thr = t_hi * (2 * HALF16) + (t_lo + HALF16)

    def mask_block(kb, _):
        vis = kpos0 + kb * tk <= qpos
        madd_ref[kb] = jnp.where(vis, jnp.where(key_ref[kb] >= thr, 0.0, NEG_BIG), NEG_BIG)
        return 0

    lax.fori_loop(0, nkb, mask_block, 0)

    @pl.when(i == 0)
    def _():
        def transpose_values(kb, _):
            vt_ref[kb] = v_ref[pl.ds(pl.multiple_of(kb * tk, tk), tk), :].T
            return 0

        lax.fori_loop(0, vt_ref.shape[0], transpose_values, 0)

    m_ref[...] = jnp.full(m_ref.shape, NEG_BIG, F32)
    l_ref[...] = jnp.zeros(l_ref.shape, F32)
    acc_ref[...] = jnp.zeros(acc_ref.shape, F32)

    def attend(kb, near, buf=0):
        keys = pl.ds(pl.multiple_of(kb * tk, tk), tk)
        madd = madd_ref[kb]
        for h in range(DSA_HEADS):
            hs = slice(h * dh, (h + 1) * dh)
            s_ref[buf, h] = lax.dot_general(k_ref[keys, hs], q_ref[:, hs], NT_DIMS,
                                            preferred_element_type=F32)
        for h in range(DSA_HEADS):
            hs = slice(h * dh, (h + 1) * dh)
            s = s_ref[buf, h] * (scale * LOG2E)
            if near is not None:
                s = s + corr_ref[h, near]
            s = s + madd
            m_prev = m_ref[h, 0:1, :]
            m_next = jnp.maximum(m_prev, jnp.max(s, axis=0, keepdims=True))
            alpha = jnp.exp2(m_prev - m_next)
            p = jnp.exp2(s - m_next)
            l_next = alpha * l_ref[h, 0:1, :] + jnp.sum(p, axis=0, keepdims=True)
            pv = jnp.dot(vt_ref[kb, hs, :], p.astype(BF16),
                         preferred_element_type=F32)
            acc_ref[h] = alpha * acc_ref[h] + pv
            m_ref[h] = jnp.broadcast_to(m_next, (sub, tq))
            l_ref[h] = jnp.broadcast_to(l_next, (sub, tq))

    def far_pair(p, _):
        attend(2 * p, None, 0)
        attend(2 * p + 1, None, 1)
        return 0

    n_far = jnp.maximum(i - 1, 0)
    lax.fori_loop(0, n_far >> 1, far_pair, 0)

    @pl.when(n_far & 1 == 1)
    def _():
        attend(n_far - 1, None)

    @pl.when(i >= 1)
    def _():
        attend(i - 1, 1, 0)
        attend(i, 0, 1)

    @pl.when(i == 0)
    def _():
        attend(i, 0)

    for h in range(DSA_HEADS):
        o_h = acc_ref[h] / l_ref[h, 0:1, :]
        o_ref[:, h * dh:(h + 1) * dh] = o_h.T.astype(o_ref.dtype)


def dsa(pa, pb, corr, tq):
    b, l, _ = pa.shape
    dh = LANES
    hd = DSA_HEADS * dh
    qb = (2 * GLA_HEADS * LANES + GLA_HEADS * 2 * LANES) // hd
    iwb = GLA_HEADS * 2 * LANES // LANES + 1
    ikb = (iwb + 1) * LANES // (2 * LANES)
    topk = min(TOPK_MAX, l // 4)
    assert l % tq == 0 and tq >= topk and tq >= REL_MAX_DIST
    nkb = l // tq
    assert nkb * tq // PACK16 < HALF16
    return pl.pallas_call(
        functools.partial(_dsa_kernel, topk=topk),
        grid=(b, l // tq),
        in_specs=[
            pl.BlockSpec((None, tq, hd), lambda bi, i: (bi, i, qb)),
            pl.BlockSpec((None, tq, IDX_HEADS * IDX_DH), lambda bi, i: (bi, i, qb + 3)),
            pl.BlockSpec((None, tq, LANES), lambda bi, i: (bi, i, iwb)),
            _resident((None, l, hd), lambda bi, i: (bi, 0, qb + 1)),
            _resident((None, l, hd), lambda bi, i: (bi, 0, qb + 2)),
            _resident((None, l, 2 * LANES), lambda bi, i: (bi, 0, ikb)),
            _resident((DSA_HEADS, 2, tq, tq), lambda bi, i: (0, 0, 0, 0)),
        ],
        out_specs=pl.BlockSpec((None, tq, hd), lambda bi, i: (bi, i, 0)),
        out_shape=jax.ShapeDtypeStruct((b, l, hd), BF16),
        scratch_shapes=[
            pltpu.VMEM((nkb, tq, tq), jnp.int32),
            pltpu.VMEM((nkb, tq, tq), jnp.int16),
            pltpu.VMEM((nkb, tq, tq), jnp.int16),
            pltpu.VMEM((nkb, tq, tq), F32),
            pltpu.VMEM((LANES, tq), F32),
            pltpu.VMEM((DSA_HEADS, 8, tq), F32),
            pltpu.VMEM((DSA_HEADS, 8, tq), F32),
            pltpu.VMEM((DSA_HEADS, dh, tq), F32),
            pltpu.VMEM((nkb, hd, tq), BF16),
            pltpu.VMEM((2, DSA_HEADS, tq, tq), F32),
        ],
        compiler_params=_params("parallel", "arbitrary"),
        name="dsa",
    )(pa, pa, pb, pa, pa, pb, corr)


def _xattn_kernel(x_ref, gq_ref, wq_ref, kv_ref, wo_ref, go_ref, o_ref, q_ref, a_ref):
    d = x_ref.shape[1]
    dh = d // XA_HEADS
    scale = dh ** -0.5
    halves = _row_parts(x_ref.shape[0], 2)
    gq = gq_ref[...]
    for rows in halves:
        h = _rms(x_ref[rows, :], gq).astype(BF16)
        q_ref[rows, :] = jnp.dot(h, wq_ref[...], preferred_element_type=F32).astype(q_ref.dtype)
    for h in range(XA_HEADS):
        hs = slice(h * dh, (h + 1) * dh)
        s = lax.dot_general(q_ref[:, hs], kv_ref[:, hs], NT_DIMS, preferred_element_type=F32) * scale
        e = jnp.exp(s - jnp.max(s, axis=-1, keepdims=True))
        o = jnp.dot(e.astype(BF16), kv_ref[:, d + h * dh:d + (h + 1) * dh], preferred_element_type=F32)
        a_ref[:, hs] = (o / jnp.sum(e, axis=-1, keepdims=True)).astype(a_ref.dtype)
    go = go_ref[...]
    for rows in halves:
        y = jnp.dot(a_ref[rows, :], wo_ref[...], preferred_element_type=F32)
        o_ref[rows, :] = x_ref[rows, :] + _rms(y, go)


def xattn(x, kv, gq, wq, go, wo, layer, tq):
    b, l, d = x.shape
    nm = kv.shape[1]
    vec = lambda: _resident((1, d), lambda bi, i: (0, 0))
    mat = lambda: _resident((None, d, d), lambda bi, i: (layer, 0, 0))
    return pl.pallas_call(
        _xattn_kernel,
        grid=(b, l // tq),
        in_specs=[
            pl.BlockSpec((None, tq, d), lambda bi, i: (bi, i, 0)),
            vec(), mat(),
            pl.BlockSpec((None, nm, 2 * d), lambda bi, i: (bi, 0, 0)),
            mat(), vec(),
        ],
        out_specs=pl.BlockSpec((None, tq, d), lambda bi, i: (bi, i, 0)),
        out_shape=jax.ShapeDtypeStruct((b, l, d), F32),
        scratch_shapes=[pltpu.VMEM((tq, d), BF16)] * 2,
        compiler_params=_params("parallel", "parallel"),
        name="xattn",
    )(x, gq.reshape(1, d), wq, kv, wo, go.reshape(1, d))


def _rel_bucket(dist):
    n = jnp.maximum(dist, 0)
    max_exact = REL_BUCKETS // 2
    nf = jnp.maximum(n, 1).astype(F32)
    large = max_exact + (jnp.log(nf / max_exact) / math.log(REL_MAX_DIST / max_exact)
                         * (REL_BUCKETS - max_exact)).astype(jnp.int32)
    large = jnp.minimum(large, REL_BUCKETS - 1)
    return jnp.where(n < max_exact, n, large)


def _bias_correction_tiles(rel_table, tq):
    t = jnp.arange(tq)
    d0 = t[:, None] - t[None, :]
    buckets = jnp.stack([_rel_bucket(d0), _rel_bucket(d0 + tq)], axis=0)
    onehot = (buckets[..., None] == jnp.arange(REL_BUCKETS)).astype(F32)
    far = rel_table[_rel_bucket(jnp.int32(2 * tq))]
    return jnp.einsum("ntsk,kh->hnst", onehot, (rel_table - far) * LOG2E,
                      precision=lax.Precision.HIGHEST)


def _chunk_sum_matrices(tl):
    r = jnp.arange(tl)
    same = (r[:, None] // GLA_CHUNK) == (r[None, :] // GLA_CHUNK)
    return (same & (r[None, :] <= r[:, None])).astype(BF16)


def _split_in_proj(w):
    depth, d, _ = w.shape
    half = d // 2
    sizes = [GLA_HEADS * LANES, GLA_HEADS * LANES, half, GLA_RANK, half,
             half, half, half, IDX_HEADS * IDX_DH, IDX_DH, IDX_HEADS]
    offs = [0]
    for s in sizes:
        offs.append(offs[-1] + s)
    assert offs[-1] == w.shape[2]
    seg = lambda n: w[:, :, offs[n]:offs[n + 1]]
    zero = lambda n: jnp.zeros((depth, d, n), w.dtype)
    ik = seg(9)
    w_head = w[:, :, :offs[3]]
    w_mid = w[:, :, offs[5]:offs[9]]
    w_small = jnp.concatenate([seg(4), seg(3), zero(LANES - GLA_RANK), seg(10), zero(LANES - IDX_HEADS),
                               ik, zero(2 * IDX_DH), ik], axis=2)
    return w_head.astype(BF16), w_mid.astype(BF16), w_small.astype(BF16)


def kernel(x, mem, norm_gains, w_in, w_gate_up, b_gate, gla_head_gain, rel_table,
           w_out, w_xq, w_xkv, w_xo, w_ffn_in, w_ffn_out):
    b, l, d = x.shape
    nm = mem.shape[1]
    depth = w_in.shape[0]
    d_ff = w_ffn_out.shape[1]
    m = b * l
    assert d // 2 // GLA_HEADS == 2 * LANES and d // 2 // DSA_HEADS == LANES

    tq_dsa = 256
    tl_gla = 512
    corr = _bias_correction_tiles(rel_table, tq_dsa)
    tri = _chunk_sum_matrices(tl_gla)
    w2p = jnp.pad(w_gate_up, ((0, 0), (0, LANES - GLA_RANK), (0, 0))).astype(BF16)

    w_out, w_xq, w_xkv, w_xo, w_ffn_in, w_ffn_out = (
        w.astype(BF16) for w in (w_out, w_xq, w_xkv, w_xo, w_ffn_in, w_ffn_out))
    w_head, w_mid, w_small = _split_in_proj(w_in)

    xf = x.reshape(m, d)
    memf = mem.reshape(b * nm, d)
    for li in range(depth):
        g = norm_gains[li]
        pa, pb = in_proj(xf, g[0], w_head, w_mid, w_small, li, bm=1024, bn=1024, bs=512)
        pa, pb = pa.reshape(b, l, -1), pb.reshape(b, l, -1)
        o_gla = gla(pa, pb, w2p[li], b_gate[li].reshape(1, -1),
                    gla_head_gain[li].reshape(1, -1), tri, tl_gla, heads_per_step=2)
        o_dsa = dsa(pa, pb, corr, tq_dsa)
        xf = mm_norm_res([o_gla.reshape(m, -1), o_dsa.reshape(m, -1)], w_out, li, g[1], xf, bm=512)
        kv = norm_matmul(memf, g[6], w_xkv, li, BF16, bm=min(512, b * nm), bn=1024)
        xf = xattn(xf.reshape(b, l, d), kv.reshape(b, nm, 2 * d), g[2], w_xq, g[3], w_xo, li,
                   tq=512).reshape(m, d)
        act = ffn_in(xf, g[4], w_ffn_in, li, d_ff, bm=1024, bn=512)
        xf = mm_norm_res([act], w_ffn_out, li, g[5], xf, bm=512)
    return xf.reshape(b, l, d)
```

```python
import functools
import math

import jax
import jax.numpy as jnp
from jax import lax
from jax.experimental import pallas as pl
from jax.experimental.pallas import tpu as pltpu

F32 = jnp.float32
BF16 = jnp.bfloat16

EPS = 1e-6
GLA_HEADS = 4
GLA_RANK = 16
GLA_TAU = 16.0
GLA_CHUNK = 64
DSA_HEADS = 8
IDX_HEADS = 16
IDX_DH = 64
TOPK_MAX = 256
REL_BUCKETS = 32
REL_MAX_DIST = 128
XA_HEADS = 4

LANES = 128
VMEM_LIMIT_BYTES = 60000 * 1024

PACK16 = 16
ROW_CHUNK = 128
NORM_PARTS = 4

NEG_BIG = -1e30
HALF16 = 2 ** 15
LOG2E = 1.4426950408889634

NT_DIMS = (((1,), (1,)), ((), ()))
TN_DIMS = (((0,), (0,)), ((), ()))


def _params(*sem):
    return pltpu.CompilerParams(dimension_semantics=sem, vmem_limit_bytes=VMEM_LIMIT_BYTES)


def _resident(shape, index_map):
    return pl.BlockSpec(shape, index_map, pipeline_mode=pl.Buffered(1))


def _rms(x, g):
    return x * lax.rsqrt(jnp.mean(x * x, axis=-1, keepdims=True) + EPS) * g


def _row_parts(n_rows, parts):
    size = n_rows // parts
    return [slice(p * size, (p + 1) * size) for p in range(parts)]


def _rms_to(x_ref, g_ref, h_ref):
    g = g_ref[...]

    def body(c, _):
        rows = pl.ds(pl.multiple_of(c * ROW_CHUNK, ROW_CHUNK), ROW_CHUNK)
        h_ref[rows, :] = _rms(x_ref[rows, :], g).astype(h_ref.dtype)
        return 0

    lax.fori_loop(0, x_ref.shape[0] // ROW_CHUNK, body, 0)


def _norm_matmul_kernel(x_ref, g_ref, w_ref, o_ref, h_ref):
    @pl.when(pl.program_id(1) == 0)
    def _():
        _rms_to(x_ref, g_ref, h_ref)

    o_ref[...] = jnp.dot(h_ref[...], w_ref[...], preferred_element_type=F32).astype(o_ref.dtype)


def norm_matmul(x, g, w, layer, out_dtype, bm, bn):
    m, k = x.shape
    n = w.shape[2]
    assert m % bm == 0 and n % bn == 0
    return pl.pallas_call(
        _norm_matmul_kernel,
        grid=(m // bm, n // bn),
        in_specs=[
            pl.BlockSpec((bm, k), lambda i, j: (i, 0)),
            _resident((1, k), lambda i, j: (0, 0)),
            pl.BlockSpec((None, k, bn), lambda i, j: (layer, 0, j)),
        ],
        out_specs=pl.BlockSpec((bm, bn), lambda i, j: (i, j)),
        out_shape=jax.ShapeDtypeStruct((m, n), out_dtype),
        scratch_shapes=[pltpu.VMEM((bm, k), BF16)],
        compiler_params=_params("parallel", "arbitrary"),
        name="norm_matmul",
    )(x, g.reshape(1, k), w)


def _in_proj_slot(i, j, n_slots):
    return jnp.where(i % 2 == 0, j, n_slots - 1 - j)


def _in_proj_kernel(x_ref, g_ref, wh_ref, wm_ref, ws_ref, oa_ref, ob_ref, h_ref, *, nh, ns, nm):
    s = _in_proj_slot(pl.program_id(0), pl.program_id(1), nh + ns + nm)

    first = pl.program_id(1) == 0

    def project(w_ref, o_ref):
        o_ref[...] = jnp.dot(h_ref[...], w_ref[...], preferred_element_type=F32).astype(o_ref.dtype)

    def project_first(w_ref, o_ref):
        g = g_ref[...]
        for rows in _row_parts(x_ref.shape[0], NORM_PARTS):
            h = _rms(x_ref[rows, :], g).astype(h_ref.dtype)
            h_ref[rows, :] = h
            o_ref[rows, :] = jnp.dot(h, w_ref[...], preferred_element_type=F32).astype(o_ref.dtype)

    pl.when(first & (s < nh))(lambda: project_first(wh_ref, oa_ref))
    pl.when(first & (s >= nh + ns))(lambda: project_first(wm_ref, oa_ref))
    pl.when(jnp.logical_not(first) & (s < nh))(lambda: project(wh_ref, oa_ref))
    pl.when((s >= nh) & (s < nh + ns))(lambda: project(ws_ref, ob_ref))
    pl.when(jnp.logical_not(first) & (s >= nh + ns))(lambda: project(wm_ref, oa_ref))


def in_proj(x, g, w_head, w_mid, w_small, layer, bm, bn, bs):
    m, k = x.shape
    n_head, n_mid, n_small = w_head.shape[2], w_mid.shape[2], w_small.shape[2]
    assert m % bm == 0 and n_head % bn == 0 and n_mid % bn == 0 and n_small % bs == 0
    nh, nm, ns = n_head // bn, n_mid // bn, n_small // bs
    n_slots = nh + ns + nm

    def slot(i, j):
        return _in_proj_slot(i, j, n_slots)

    def pa_block(i, j):
        s = slot(i, j)
        hold = jnp.where(i % 2 == 0, nh - 1, nh)
        return jnp.where(s < nh, s, jnp.where(s >= nh + ns, s - ns, hold))

    return pl.pallas_call(
        functools.partial(_in_proj_kernel, nh=nh, ns=ns, nm=nm),
        grid=(m // bm, n_slots),
        in_specs=[
            pl.BlockSpec((bm, k), lambda i, j: (i, 0)),
            _resident((1, k), lambda i, j: (0, 0)),
            pl.BlockSpec((None, k, bn), lambda i, j: (layer, 0, jnp.clip(slot(i, j), 0, nh - 1))),
            pl.BlockSpec((None, k, bn), lambda i, j: (layer, 0, jnp.clip(slot(i, j) - nh - ns, 0, nm - 1))),
            pl.BlockSpec((None, k, bs), lambda i, j: (layer, 0, jnp.clip(slot(i, j) - nh, 0, ns - 1))),
        ],
        out_specs=[
            pl.BlockSpec((bm, bn), lambda i, j: (i, pa_block(i, j))),
            pl.BlockSpec((bm, bs), lambda i, j: (i, jnp.clip(slot(i, j) - nh, 0, ns - 1))),
        ],
        out_shape=[
            jax.ShapeDtypeStruct((m, n_head + n_mid), BF16),
            jax.ShapeDtypeStruct((m, n_small), F32),
        ],
        scratch_shapes=[pltpu.VMEM((bm, k), BF16)],
        compiler_params=_params("parallel", "arbitrary"),
        name="in_proj",
    )(x, g.reshape(1, k), w_head, w_mid, w_small)


def _ffn_in_kernel(x_ref, g_ref, wg_ref, wu_ref, o_ref, h_ref):
    def gated(h):
        a = jnp.dot(h, wg_ref[...], preferred_element_type=F32)
        b = jnp.dot(h, wu_ref[...], preferred_element_type=F32)
        return (a * (1.0 / (1.0 + jnp.exp(-a))) * b).astype(o_ref.dtype)

    @pl.when(pl.program_id(1) == 0)
    def _():
        g = g_ref[...]
        for rows in _row_parts(x_ref.shape[0], NORM_PARTS):
            h = _rms(x_ref[rows, :], g).astype(h_ref.dtype)
            h_ref[rows, :] = h
            o_ref[rows, :] = gated(h)

    @pl.when(pl.program_id(1) > 0)
    def _():
        o_ref[...] = gated(h_ref[...])


def ffn_in(x, g, w, layer, d_ff, bm, bn):
    m, k = x.shape
    assert m % bm == 0 and d_ff % bn == 0
    nj = d_ff // bn
    return pl.pallas_call(
        _ffn_in_kernel,
        grid=(m // bm, nj),
        in_specs=[
            pl.BlockSpec((bm, k), lambda i, j: (i, 0)),
            _resident((1, k), lambda i, j: (0, 0)),
            pl.BlockSpec((None, k, bn), lambda i, j: (layer, 0, j)),
            pl.BlockSpec((None, k, bn), lambda i, j: (layer, 0, j + nj)),
        ],
        out_specs=pl.BlockSpec((bm, bn), lambda i, j: (i, j)),
        out_shape=jax.ShapeDtypeStruct((m, d_ff), BF16),
        scratch_shapes=[pltpu.VMEM((bm, k), BF16)],
        compiler_params=_params("parallel", "arbitrary"),
        name="ffn_in",
    )(x, g.reshape(1, k), w, w)


def _mm_norm_res_kernel(*refs, n_lhs, nk):
    lhs = refs[:n_lhs]
    w_ref, g_ref, res_ref, o_ref = refs[n_lhs:n_lhs + 4]
    if nk == 1:
        g = g_ref[...]
        half = o_ref.shape[0] // 2
        for rows in (slice(0, half), slice(half, 2 * half)):
            y = None
            off = 0
            for a in lhs:
                wd = a.shape[1]
                part = jnp.dot(a[rows, :], w_ref[off:off + wd, :], preferred_element_type=F32)
                y = part if y is None else y + part
                off += wd
            o_ref[rows, :] = res_ref[rows, :] + _rms(y, g)
    else:
        k = pl.program_id(1)
        part = jnp.dot(lhs[0][...], w_ref[...], preferred_element_type=F32)

        @pl.when(k == 0)
        def _():
            o_ref[...] = part

        @pl.when(k > 0)
        def _():
            o_ref[...] += part

        @pl.when(k == nk - 1)
        def _():
            g = g_ref[...]

            def finish(c, _):
                rows = pl.ds(pl.multiple_of(c * ROW_CHUNK, ROW_CHUNK), ROW_CHUNK)
                o_ref[rows, :] = res_ref[rows, :] + _rms(o_ref[rows, :], g)
                return 0

            lax.fori_loop(0, o_ref.shape[0] // ROW_CHUNK, finish, 0)


def mm_norm_res(lhs_list, w, layer, g, res, bm, nk=1):
    m, n = res.shape
    kk = w.shape[1]
    assert m % bm == 0 and kk % nk == 0
    bk = kk // nk
    n_lhs = len(lhs_list)
    if nk == 1:
        lhs_specs = [pl.BlockSpec((bm, a.shape[1]), lambda i, k: (i, 0)) for a in lhs_list]
        w_spec = _resident((None, kk, n), lambda i, k: (layer, 0, 0))
    else:
        assert n_lhs == 1
        kblock = lambda i, k: jnp.where(i % 2 == 0, k, nk - 1 - k)
        lhs_specs = [pl.BlockSpec((bm, bk), lambda i, k: (i, kblock(i, k)))]
        w_spec = pl.BlockSpec((None, bk, n), lambda i, k: (layer, kblock(i, k), 0))
    return pl.pallas_call(
        functools.partial(_mm_norm_res_kernel, n_lhs=n_lhs, nk=nk),
        grid=(m // bm, nk),
        in_specs=lhs_specs + [
            w_spec,
            _resident((1, n), lambda i, k: (0, 0)),
            pl.BlockSpec((bm, n), lambda i, k: (i, 0)),
        ],
        out_specs=pl.BlockSpec((bm, n), lambda i, k: (i, 0)),
        out_shape=jax.ShapeDtypeStruct((m, n), F32),
        compiler_params=_params("parallel", "arbitrary"),
        name="mm_norm_res",
    )(*lhs_list, w, g.reshape(1, n), res)


def _gla_kernel(q_ref, k_ref, v_ref, z_ref, r_ref, w2_ref, bg_ref, gh_ref, tri_ref,
                o_ref, st_ref, *, dk, dv):
    tl = q_ref.shape[0]
    heads = q_ref.shape[1] // dk
    n_chunks = tl // GLA_CHUNK

    @pl.when(pl.program_id(2) == 0)
    def _():
        st_ref[...] = jnp.zeros_like(st_ref)

    pre = jnp.dot(z_ref[...].astype(BF16), w2_ref[...], preferred_element_type=F32) + bg_ref[...]
    logg = (jnp.minimum(pre, 0.0) - jnp.log1p(jnp.exp(-jnp.abs(pre)))) * (1.0 / GLA_TAU)
    hi = logg.astype(BF16)
    lo = (logg - hi.astype(F32)).astype(BF16)
    incl = tri_ref[...]
    cs = jnp.dot(incl, jnp.concatenate([hi, lo], axis=1), preferred_element_type=F32)
    bcum_all = cs[:, :heads * dk] + cs[:, heads * dk:]
    gh = gh_ref[...]

    for hh in range(heads):
        ks = slice(hh * dk, (hh + 1) * dk)
        vs = slice(hh * dv, (hh + 1) * dv)
        bcum = bcum_all[:, ks]
        b_last = [bcum[(c + 1) * GLA_CHUNK - 1:(c + 1) * GLA_CHUNK, :] for c in range(n_chunks)]
        suf = jnp.concatenate([jnp.broadcast_to(bl, (GLA_CHUNK, dk)) for bl in b_last], axis=0) - bcum

        qf = q_ref[:, ks].astype(F32) * dk ** -0.5
        kf = k_ref[:, ks].astype(F32)
        q_dec = (qf * jnp.exp(bcum)).astype(BF16)
        k_intra = (kf * jnp.exp(-bcum)).astype(BF16)
        k_state = (kf * jnp.exp(suf)).astype(BF16)
        v = v_ref[:, vs]

        a = lax.dot_general(q_dec, k_intra, NT_DIMS, preferred_element_type=F32)
        a = jnp.where(incl > 0, a, 0.0).astype(BF16)
        o_intra = jnp.dot(a, v, preferred_element_type=F32)

        st = st_ref[hh]
        outs = []
        for c in range(n_chunks):
            rows = slice(c * GLA_CHUNK, (c + 1) * GLA_CHUNK)
            o_inter = lax.dot_general(q_dec[rows], st.astype(BF16), NT_DIMS,
                                      preferred_element_type=F32)
            outs.append(o_intra[rows] + o_inter)
            decay = jnp.exp(b_last[c])
            upd = lax.dot_general(v[rows], k_state[rows], TN_DIMS, preferred_element_type=F32)
            st = decay * st + upd
        st_ref[hh] = st
        o = _rms(jnp.concatenate(outs, axis=0), gh)
        r = r_ref[:, vs]
        o_ref[:, vs] = (o * (r * (1.0 / (1.0 + jnp.exp(-r))))).astype(o_ref.dtype)


def gla(pa, pb, w2p, bgate, ghead, tri, tl, heads_per_step):
    b, l, _ = pa.shape
    dk = LANES
    dv = 2 * dk
    hp = heads_per_step
    assert GLA_HEADS % hp == 0
    groups = GLA_HEADS // hp
    wk, wv = hp * dk, hp * dv
    kb0 = GLA_HEADS * dk // wk
    vb0 = 2 * GLA_HEADS * dk // wv
    zb = GLA_HEADS * dv // LANES
    return pl.pallas_call(
        functools.partial(_gla_kernel, dk=dk, dv=dv),
        grid=(b, groups, l // tl),
        in_specs=[
            pl.BlockSpec((None, tl, wk), lambda bi, h, t: (bi, t, h)),
            pl.BlockSpec((None, tl, wk), lambda bi, h, t: (bi, t, kb0 + h)),
            pl.BlockSpec((None, tl, wv), lambda bi, h, t: (bi, t, vb0 + h)),
            pl.BlockSpec((None, tl, LANES), lambda bi, h, t: (bi, t, zb)),
            pl.BlockSpec((None, tl, wv), lambda bi, h, t: (bi, t, h)),
            pl.BlockSpec((LANES, wk), lambda bi, h, t: (0, h)),
            pl.BlockSpec((1, wk), lambda bi, h, t: (0, h)),
            _resident((1, dv), lambda bi, h, t: (0, 0)),
            _resident((tl, tl), lambda bi, h, t: (0, 0)),
        ],
        out_specs=pl.BlockSpec((None, tl, wv), lambda bi, h, t: (bi, t, h)),
        out_shape=jax.ShapeDtypeStruct((b, l, GLA_HEADS * dv), BF16),
        scratch_shapes=[pltpu.VMEM((hp, dv, dk), F32)],
        compiler_params=_params("parallel", "parallel", "arbitrary"),
        name="gla",
    )(pa, pa, pa, pb, pb, w2p, bgate, ghead, tri)


def _for_each_block_paired(n, body):
    def pair(p, _):
        body(2 * p)
        body(2 * p + 1)
        return 0

    lax.fori_loop(0, n >> 1, pair, 0)

    @pl.when(n & 1 == 1)
    def _():
        body(n - 1)


def _dsa_kernel(q_ref, iq_ref, iw_ref, k_ref, v_ref, ik_ref, corr_ref, o_ref,
                key_ref, hi_ref, lo_ref, madd_ref, wt_ref, m_ref, l_ref, acc_ref, vt_ref, s_ref,
                *, topk):
    tq = q_ref.shape[0]
    tk = tq
    dh = LANES
    sub = 8
    i = pl.program_id(1)
    nkb = i + 1
    scale = dh ** -0.5

    wt_ref[...] = (iw_ref[...] * (IDX_HEADS ** -0.5 * IDX_DH ** -0.5)).T
    qpos = lax.broadcasted_iota(jnp.int32, (tk, tq), 1) + i * tq
    kpos0 = lax.broadcasted_iota(jnp.int32, (tk, tq), 0)

    def score_block(kb):
        ik = ik_ref[pl.ds(pl.multiple_of(kb * tk, tk), tk), :].astype(BF16)
        ik_even, ik_odd = ik[:, :LANES], ik[:, LANES:]
        acc = jnp.zeros((tk, tq), F32)
        for p in range(IDX_HEADS // 2):
            qp = iq_ref[:, p * LANES:(p + 1) * LANES]
            for par, ikx in ((0, ik_even), (1, ik_odd)):
                h = 2 * p + par
                d = lax.dot_general(ikx, qp, NT_DIMS, preferred_element_type=F32)
                acc = acc + jnp.maximum(d, 0.0) * wt_ref[h:h + 1, :]
        sc = jnp.where(kpos0 + kb * tk <= qpos, acc, -jnp.inf)
        bits = lax.bitcast_convert_type(sc, jnp.int32)
        key = bits ^ ((bits >> 31) & 0x7FFFFFFF)
        key_ref[kb] = key
        hi_ref[kb] = (key >> 16).astype(jnp.int16)
        lo_ref[kb] = ((key & 0xFFFF) - HALF16).astype(jnp.int16)

    _for_each_block_paired(nkb, score_block)

    n_acc = 4
    slabs = tk // PACK16
    one16, zero16 = jnp.int16(1), jnp.int16(0)

    def rows16(x):
        return jnp.broadcast_to(x, (PACK16, tq)).astype(jnp.int16)

    def count16(src_ref, cand):
        cand16 = rows16(cand)

        def block(kb, cnts):
            cnts = list(cnts)
            for r in range(slabs):
                hit = src_ref[kb, r * PACK16:(r + 1) * PACK16, :] >= cand16
                cnts[r % n_acc] = cnts[r % n_acc] + jnp.where(hit, one16, zero16)
            return tuple(cnts)

        def pair(p, cnts):
            return block(2 * p + 1, block(2 * p, cnts))

        cnts = lax.fori_loop(0, nkb >> 1, pair, (jnp.zeros((PACK16, tq), jnp.int16),) * n_acc)
        cnts = lax.cond(nkb & 1 == 1, lambda c: block(nkb - 1, c), lambda c: c, cnts)
        cnt = ((cnts[0] + cnts[1]) + (cnts[2] + cnts[3])).astype(jnp.int32)
        return jnp.sum(cnt, axis=0, keepdims=True)

    def bisect16(src_ref, kth):
        def step(s, carry):
            t, above = carry
            cand = t + lax.shift_left(jnp.int32(1), 15 - s)
            total = count16(src_ref, cand)
            ok = total >= kth
            return jnp.where(ok, cand, t), jnp.where(ok, above, total)

        return lax.fori_loop(0, 16, step, (jnp.full((1, tq), -HALF16, jnp.int32),
                                           jnp.zeros((1, tq), jnp.int32)))

    t_hi, above = bisect16(hi_ref, topk)
    t_hi16 = rows16(t_hi)

    def keep_threshold_bucket(kb, _):
        for r in range(slabs):
            rows = slice(r * PACK16, (r + 1) * PACK16)
            lo_ref[kb, rows, :] = jnp.where(hi_ref[kb, rows, :] == t_hi16, lo_ref[kb, rows, :],
                                            jnp.int16(-HALF16))
        return 0

    lax.fori_loop(0, nkb, keep_threshold_bucket, 0)
    t_lo, _ = bisect16(lo_ref, topk - above)
    thr = t_hi * (2 * HALF16) + (t_lo + HALF16)

    def mask_block(kb, _):
        vis = kpos0 + kb * tk <= qpos
        madd_ref[kb] = jnp.where(vis, jnp.where(key_ref[kb] >= thr, 0.0, NEG_BIG), NEG_BIG)
        return 0

    lax.fori_loop(0, nkb, mask_block, 0)

    @pl.when(i == 0)
    def _():
        def transpose_values(kb, _):
            vt_ref[kb] = v_ref[pl.ds(pl.multiple_of(kb * tk, tk), tk), :].T
            return 0

        lax.fori_loop(0, vt_ref.shape[0], transpose_values, 0)

    m_ref[...] = jnp.full(m_ref.shape, NEG_BIG, F32)
    l_ref[...] = jnp.zeros(l_ref.shape, F32)
    acc_ref[...] = jnp.zeros(acc_ref.shape, F32)

    def attend(kb, near, buf=0):
        keys = pl.ds(pl.multiple_of(kb * tk, tk), tk)
        madd = madd_ref[kb]
        for h in range(DSA_HEADS):
            hs = slice(h * dh, (h + 1) * dh)
            s_ref[buf, h] = lax.dot_general(k_ref[keys, hs], q_ref[:, hs], NT_DIMS,
                                            preferred_element_type=F32)
        for h in range(DSA_HEADS):
            hs = slice(h * dh, (h + 1) * dh)
            s = s_ref[buf, h] * (scale * LOG2E)
            if near is not None:
                s = s + corr_ref[h, near]
            s = s + madd
            m_prev = m_ref[h, 0:1, :]
            m_next = jnp.maximum(m_prev, jnp.max(s, axis=0, keepdims=True))
            alpha = jnp.exp2(m_prev - m_next)
            p = jnp.exp2(s - m_next)
            l_next = alpha * l_ref[h, 0:1, :] + jnp.sum(p, axis=0, keepdims=True)
            pv = jnp.dot(vt_ref[kb, hs, :], p.astype(BF16),
                         preferred_element_type=F32)
            acc_ref[h] = alpha * acc_ref[h] + pv
            m_ref[h] = jnp.broadcast_to(m_next, (sub, tq))
            l_ref[h] = jnp.broadcast_to(l_next, (sub, tq))

    def far_pair(p, _):
        attend(2 * p, None, 0)
        attend(2 * p + 1, None, 1)
        return 0

    n_far = jnp.maximum(i - 1, 0)
    lax.fori_loop(0, n_far >> 1, far_pair, 0)

    @pl.when(n_far & 1 == 1)
    def _():
        attend(n_far - 1, None)

    @pl.when(i >= 1)
    def _():
        attend(i - 1, 1, 0)
        attend(i, 0, 1)

    @pl.when(i == 0)
    def _():
        attend(i, 0)

    for h in range(DSA_HEADS):
        o_h = acc_ref[h] / l_ref[h, 0:1, :]
        o_ref[:, h * dh:(h + 1) * dh] = o_h.T.astype(o_ref.dtype)


def dsa(pa, pb, corr, tq):
    b, l, _ = pa.shape
    dh = LANES
    hd = DSA_HEADS * dh
    qb = (2 * GLA_HEADS * LANES + GLA_HEADS * 2 * LANES) // hd
    iwb = GLA_HEADS * 2 * LANES // LANES + 1
    ikb = (iwb + 1) * LANES // (2 * LANES)
    topk = min(TOPK_MAX, l // 4)
    assert l % tq == 0 and tq >= topk and tq >= REL_MAX_DIST
    nkb = l // tq
    assert nkb * tq // PACK16 < HALF16
    return pl.pallas_call(
        functools.partial(_dsa_kernel, topk=topk),
        grid=(b, l // tq),
        in_specs=[
            pl.BlockSpec((None, tq, hd), lambda bi, i: (bi, i, qb)),
            pl.BlockSpec((None, tq, IDX_HEADS * IDX_DH), lambda bi, i: (bi, i, qb + 3)),
            pl.BlockSpec((None, tq, LANES), lambda bi, i: (bi, i, iwb)),
            _resident((None, l, hd), lambda bi, i: (bi, 0, qb + 1)),
            _resident((None, l, hd), lambda bi, i: (bi, 0, qb + 2)),
            _resident((None, l, 2 * LANES), lambda bi, i: (bi, 0, ikb)),
            _resident((DSA_HEADS, 2, tq, tq), lambda bi, i: (0, 0, 0, 0)),
        ],
        out_specs=pl.BlockSpec((None, tq, hd), lambda bi, i: (bi, i, 0)),
        out_shape=jax.ShapeDtypeStruct((b, l, hd), BF16),
        scratch_shapes=[
            pltpu.VMEM((nkb, tq, tq), jnp.int32),
            pltpu.VMEM((nkb, tq, tq), jnp.int16),
            pltpu.VMEM((nkb, tq, tq), jnp.int16),
            pltpu.VMEM((nkb, tq, tq), F32),
            pltpu.VMEM((LANES, tq), F32),
            pltpu.VMEM((DSA_HEADS, 8, tq), F32),
            pltpu.VMEM((DSA_HEADS, 8, tq), F32),
            pltpu.VMEM((DSA_HEADS, dh, tq), F32),
            pltpu.VMEM((nkb, hd, tq), BF16),
            pltpu.VMEM((2, DSA_HEADS, tq, tq), F32),
        ],
        compiler_params=_params("parallel", "arbitrary"),
        name="dsa",
    )(pa, pa, pb, pa, pa, pb, corr)


def _xattn_kernel(x_ref, gq_ref, wq_ref, kv_ref, wo_ref, go_ref, o_ref, q_ref, a_ref):
    d = x_ref.shape[1]
    dh = d // XA_HEADS
    scale = dh ** -0.5
    halves = _row_parts(x_ref.shape[0], 2)
    gq = gq_ref[...]
    for rows in halves:
        h = _rms(x_ref[rows, :], gq).astype(BF16)
        q_ref[rows, :] = jnp.dot(h, wq_ref[...], preferred_element_type=F32).astype(q_ref.dtype)
    for h in range(XA_HEADS):
        hs = slice(h * dh, (h + 1) * dh)
        s = lax.dot_general(q_ref[:, hs], kv_ref[:, hs], NT_DIMS, preferred_element_type=F32) * scale
        e = jnp.exp(s - jnp.max(s, axis=-1, keepdims=True))
        o = jnp.dot(e.astype(BF16), kv_ref[:, d + h * dh:d + (h + 1) * dh], preferred_element_type=F32)
        a_ref[:, hs] = (o / jnp.sum(e, axis=-1, keepdims=True)).astype(a_ref.dtype)
    go = go_ref[...]
    for rows in halves:
        y = jnp.dot(a_ref[rows, :], wo_ref[...], preferred_element_type=F32)
        o_ref[rows, :] = x_ref[rows, :] + _rms(y, go)


def xattn(x, kv, gq, wq, go, wo, layer, tq):
    b, l, d = x.shape
    nm = kv.shape[1]
    vec = lambda: _resident((1, d), lambda bi, i: (0, 0))
    mat = lambda: _resident((None, d, d), lambda bi, i: (layer, 0, 0))
    return pl.pallas_call(
        _xattn_kernel,
        grid=(b, l // tq),
        in_specs=[
            pl.BlockSpec((None, tq, d), lambda bi, i: (bi, i, 0)),
            vec(), mat(),
            pl.BlockSpec((None, nm, 2 * d), lambda bi, i: (bi, 0, 0)),
            mat(), vec(),
        ],
        out_specs=pl.BlockSpec((None, tq, d), lambda bi, i: (bi, i, 0)),
        out_shape=jax.ShapeDtypeStruct((b, l, d), F32),
        scratch_shapes=[pltpu.VMEM((tq, d), BF16)] * 2,
        compiler_params=_params("parallel", "parallel"),
        name="xattn",
    )(x, gq.reshape(1, d), wq, kv, wo, go.reshape(1, d))


def _rel_bucket(dist):
    n = jnp.maximum(dist, 0)
    max_exact = REL_BUCKETS // 2
    nf = jnp.maximum(n, 1).astype(F32)
    large = max_exact + (jnp.log(nf / max_exact) / math.log(REL_MAX_DIST / max_exact)
                         * (REL_BUCKETS - max_exact)).astype(jnp.int32)
    large = jnp.minimum(large, REL_BUCKETS - 1)
    return jnp.where(n < max_exact, n, large)


def _bias_correction_tiles(rel_table, tq):
    t = jnp.arange(tq)
    d0 = t[:, None] - t[None, :]
    buckets = jnp.stack([_rel_bucket(d0), _rel_bucket(d0 + tq)], axis=0)
    onehot = (buckets[..., None] == jnp.arange(REL_BUCKETS)).astype(F32)
    far = rel_table[_rel_bucket(jnp.int32(2 * tq))]
    return jnp.einsum("ntsk,kh->hnst", onehot, (rel_table - far) * LOG2E,
                      precision=lax.Precision.HIGHEST)


def _chunk_sum_matrices(tl):
    r = jnp.arange(tl)
    same = (r[:, None] // GLA_CHUNK) == (r[None, :] // GLA_CHUNK)
    return (same & (r[None, :] <= r[:, None])).astype(BF16)


def _split_in_proj(w):
    depth, d, _ = w.shape
    half = d // 2
    sizes = [GLA_HEADS * LANES, GLA_HEADS * LANES, half, GLA_RANK, half,
             half, half, half, IDX_HEADS * IDX_DH, IDX_DH, IDX_HEADS]
    offs = [0]
    for s in sizes:
        offs.append(offs[-1] + s)
    assert offs[-1] == w.shape[2]
    seg = lambda n: w[:, :, offs[n]:offs[n + 1]]
    zero = lambda n: jnp.zeros((depth, d, n), w.dtype)
    ik = seg(9)
    w_head = w[:, :, :offs[3]]
    w_mid = w[:, :, offs[5]:offs[9]]
    w_small = jnp.concatenate([seg(4), seg(3), zero(LANES - GLA_RANK), seg(10), zero(LANES - IDX_HEADS),
                               ik, zero(2 * IDX_DH), ik], axis=2)
    return w_head.astype(BF16), w_mid.astype(BF16), w_small.astype(BF16)


def kernel(x, mem, norm_gains, w_in, w_gate_up, b_gate, gla_head_gain, rel_table,
           w_out, w_xq, w_xkv, w_xo, w_ffn_in, w_ffn_out):
    b, l, d = x.shape
    nm = mem.shape[1]
    depth = w_in.shape[0]
    d_ff = w_ffn_out.shape[1]
    m = b * l
    assert d // 2 // GLA_HEADS == 2 * LANES and d // 2 // DSA_HEADS == LANES

    tq_dsa = 256
    tl_gla = 512
    corr = _bias_correction_tiles(rel_table, tq_dsa)
    tri = _chunk_sum_matrices(tl_gla)
    w2p = jnp.pad(w_gate_up, ((0, 0), (0, LANES - GLA_RANK), (0, 0))).astype(BF16)

    w_out, w_xq, w_xkv, w_xo, w_ffn_in, w_ffn_out = (
        w.astype(BF16) for w in (w_out, w_xq, w_xkv, w_xo, w_ffn_in, w_ffn_out))
    w_head, w_mid, w_small = _split_in_proj(w_in)

    xf = x.reshape(m, d)
    memf = mem.reshape(b * nm, d)
    for li in range(depth):
        g = norm_gains[li]
        pa, pb = in_proj(xf, g[0], w_head, w_mid, w_small, li, bm=1024, bn=1024, bs=512)
        pa, pb = pa.reshape(b, l, -1), pb.reshape(b, l, -1)
        o_gla = gla(pa, pb, w2p[li], b_gate[li].reshape(1, -1),
                    gla_head_gain[li].reshape(1, -1), tri, tl_gla, heads_per_step=GLA_HEADS)
        o_dsa = dsa(pa, pb, corr, tq_dsa)
        xf = mm_norm_res([o_gla.reshape(m, -1), o_dsa.reshape(m, -1)], w_out, li, g[1], xf, bm=512)
        kv = norm_matmul(memf, g[6], w_xkv, li, BF16, bm=min(512, b * nm), bn=1024)
        xf = xattn(xf.reshape(b, l, d), kv.reshape(b, nm, 2 * d), g[2], w_xq, g[3], w_xo, li,
                   tq=512).reshape(m, d)
        act = ffn_in(xf, g[4], w_ffn_in, li, d_ff, bm=1024, bn=512)
        xf = mm_norm_res([act], w_ffn_out, li, g[5], xf, bm=512)
    return xf.reshape(b, l, d)
```

```python
import functools
import math

import jax
import jax.numpy as jnp
from jax import lax
from jax.experimental import pallas as pl
from jax.experimental.pallas import tpu as pltpu

F32 = jnp.float32
BF16 = jnp.bfloat16

EPS = 1e-6
GLA_HEADS = 4
GLA_RANK = 16
GLA_TAU = 16.0
GLA_CHUNK = 64
DSA_HEADS = 8
IDX_HEADS = 16
IDX_DH = 64
TOPK_MAX = 256
REL_BUCKETS = 32
REL_MAX_DIST = 128
XA_HEADS = 4

LANES = 128
SUBLANES = 8
VMEM_LIMIT_BYTES = 60000 * 1024

PACK16 = 16
ROW_CHUNK = 128
NORM_PARTS = 4

NEG_BIG = -1e30
HALF16 = 2 ** 15
LOG2E = 1.4426950408889634

TILES = dict(
    in_proj=(1024, 1024, 512),
    gla_t=512,
    dsa_q=256,
    out_proj_m=1024,
    mem_kv=(512, 1024),
    xattn_q=512,
    ffn_in=(1024, 512),
    ffn_out_m=512,
)

NT_DIMS = (((1,), (1,)), ((), ()))
TN_DIMS = (((0,), (0,)), ((), ()))


def _params(*sem):
    return pltpu.CompilerParams(dimension_semantics=sem, vmem_limit_bytes=VMEM_LIMIT_BYTES)


def _resident(shape, index_map):
    return pl.BlockSpec(shape, index_map, pipeline_mode=pl.Buffered(1))


def _rms(x, g):
    return x * lax.rsqrt(jnp.mean(x * x, axis=-1, keepdims=True) + EPS) * g


def _row_parts(n_rows, parts):
    size = n_rows // parts
    return [slice(p * size, (p + 1) * size) for p in range(parts)]


def _rms_to(x_ref, g_ref, h_ref):
    g = g_ref[...]

    def body(c, _):
        rows = pl.ds(pl.multiple_of(c * ROW_CHUNK, ROW_CHUNK), ROW_CHUNK)
        h_ref[rows, :] = _rms(x_ref[rows, :], g).astype(h_ref.dtype)
        return 0

    lax.fori_loop(0, x_ref.shape[0] // ROW_CHUNK, body, 0)


def _norm_matmul_kernel(x_ref, g_ref, w_ref, o_ref, h_ref):
    @pl.when(pl.program_id(1) == 0)
    def _():
        _rms_to(x_ref, g_ref, h_ref)

    o_ref[...] = jnp.dot(h_ref[...], w_ref[...], preferred_element_type=F32).astype(o_ref.dtype)


def norm_matmul(x, g, w, layer, out_dtype, bm, bn):
    m, k = x.shape
    n = w.shape[2]
    assert m % bm == 0 and n % bn == 0
    return pl.pallas_call(
        _norm_matmul_kernel,
        grid=(m // bm, n // bn),
        in_specs=[
            pl.BlockSpec((bm, k), lambda i, j: (i, 0)),
            _resident((1, k), lambda i, j: (0, 0)),
            pl.BlockSpec((None, k, bn), lambda i, j: (layer, 0, j)),
        ],
        out_specs=pl.BlockSpec((bm, bn), lambda i, j: (i, j)),
        out_shape=jax.ShapeDtypeStruct((m, n), out_dtype),
        scratch_shapes=[pltpu.VMEM((bm, k), BF16)],
        compiler_params=_params("parallel", "arbitrary"),
        name="norm_matmul",
    )(x, g.reshape(1, k), w)


def _in_proj_slot(i, j, n_slots):
    return jnp.where(i % 2 == 0, j, n_slots - 1 - j)


def _in_proj_kernel(x_ref, g_ref, wh_ref, wm_ref, ws_ref, oa_ref, ob_ref, h_ref, *, nh, ns, nm):
    s = _in_proj_slot(pl.program_id(0), pl.program_id(1), nh + ns + nm)

    first = pl.program_id(1) == 0

    def project(w_ref, o_ref):
        o_ref[...] = jnp.dot(h_ref[...], w_ref[...], preferred_element_type=F32).astype(o_ref.dtype)

    def project_first(w_ref, o_ref):
        g = g_ref[...]
        for rows in _row_parts(x_ref.shape[0], NORM_PARTS):
            h = _rms(x_ref[rows, :], g).astype(h_ref.dtype)
            h_ref[rows, :] = h
            o_ref[rows, :] = jnp.dot(h, w_ref[...], preferred_element_type=F32).astype(o_ref.dtype)

    pl.when(first & (s < nh))(lambda: project_first(wh_ref, oa_ref))
    pl.when(first & (s >= nh + ns))(lambda: project_first(wm_ref, oa_ref))
    pl.when(jnp.logical_not(first) & (s < nh))(lambda: project(wh_ref, oa_ref))
    pl.when((s >= nh) & (s < nh + ns))(lambda: project(ws_ref, ob_ref))
    pl.when(jnp.logical_not(first) & (s >= nh + ns))(lambda: project(wm_ref, oa_ref))


def in_proj(x, g, w_head, w_mid, w_small, layer, bm, bn, bs):
    m, k = x.shape
    n_head, n_mid, n_small = w_head.shape[2], w_mid.shape[2], w_small.shape[2]
    assert m % bm == 0 and n_head % bn == 0 and n_mid % bn == 0 and n_small % bs == 0
    nh, nm, ns = n_head // bn, n_mid // bn, n_small // bs
    n_slots = nh + ns + nm

    def slot(i, j):
        return _in_proj_slot(i, j, n_slots)

    def pa_block(i, j):
        s = slot(i, j)
        hold = jnp.where(i % 2 == 0, nh - 1, nh)
        return jnp.where(s < nh, s, jnp.where(s >= nh + ns, s - ns, hold))

    return pl.pallas_call(
        functools.partial(_in_proj_kernel, nh=nh, ns=ns, nm=nm),
        grid=(m // bm, n_slots),
        in_specs=[
            pl.BlockSpec((bm, k), lambda i, j: (i, 0)),
            _resident((1, k), lambda i, j: (0, 0)),
            pl.BlockSpec((None, k, bn), lambda i, j: (layer, 0, jnp.clip(slot(i, j), 0, nh - 1))),
            pl.BlockSpec((None, k, bn), lambda i, j: (layer, 0, jnp.clip(slot(i, j) - nh - ns, 0, nm - 1))),
            pl.BlockSpec((None, k, bs), lambda i, j: (layer, 0, jnp.clip(slot(i, j) - nh, 0, ns - 1))),
        ],
        out_specs=[
            pl.BlockSpec((bm, bn), lambda i, j: (i, pa_block(i, j))),
            pl.BlockSpec((bm, bs), lambda i, j: (i, jnp.clip(slot(i, j) - nh, 0, ns - 1))),
        ],
        out_shape=[
            jax.ShapeDtypeStruct((m, n_head + n_mid), BF16),
            jax.ShapeDtypeStruct((m, n_small), F32),
        ],
        scratch_shapes=[pltpu.VMEM((bm, k), BF16)],
        compiler_params=_params("parallel", "arbitrary"),
        name="in_proj",
    )(x, g.reshape(1, k), w_head, w_mid, w_small)


def _ffn_in_kernel(x_ref, g_ref, wg_ref, wu_ref, o_ref, h_ref):
    def gated(h):
        a = jnp.dot(h, wg_ref[...], preferred_element_type=F32)
        b = jnp.dot(h, wu_ref[...], preferred_element_type=F32)
        return (a * (1.0 / (1.0 + jnp.exp(-a))) * b).astype(o_ref.dtype)

    @pl.when(pl.program_id(1) == 0)
    def _():
        g = g_ref[...]
        for rows in _row_parts(x_ref.shape[0], NORM_PARTS):
            h = _rms(x_ref[rows, :], g).astype(h_ref.dtype)
            h_ref[rows, :] = h
            o_ref[rows, :] = gated(h)

    @pl.when(pl.program_id(1) > 0)
    def _():
        o_ref[...] = gated(h_ref[...])


def ffn_in(x, g, w, layer, d_ff, bm, bn):
    m, k = x.shape
    assert m % bm == 0 and d_ff % bn == 0
    nj = d_ff // bn
    return pl.pallas_call(
        _ffn_in_kernel,
        grid=(m // bm, nj),
        in_specs=[
            pl.BlockSpec((bm, k), lambda i, j: (i, 0)),
            _resident((1, k), lambda i, j: (0, 0)),
            pl.BlockSpec((None, k, bn), lambda i, j: (layer, 0, j)),
            pl.BlockSpec((None, k, bn), lambda i, j: (layer, 0, j + nj)),
        ],
        out_specs=pl.BlockSpec((bm, bn), lambda i, j: (i, j)),
        out_shape=jax.ShapeDtypeStruct((m, d_ff), BF16),
        scratch_shapes=[pltpu.VMEM((bm, k), BF16)],
        compiler_params=_params("parallel", "arbitrary"),
        name="ffn_in",
    )(x, g.reshape(1, k), w, w)


def _mm_norm_res_kernel(*refs, n_lhs, nk):
    lhs = refs[:n_lhs]
    w_ref, g_ref, res_ref, o_ref = refs[n_lhs:n_lhs + 4]
    if nk == 1:
        g = g_ref[...]
        half = o_ref.shape[0] // 2
        for rows in (slice(0, half), slice(half, 2 * half)):
            y = None
            off = 0
            for a in lhs:
                wd = a.shape[1]
                part = jnp.dot(a[rows, :], w_ref[off:off + wd, :], preferred_element_type=F32)
                y = part if y is None else y + part
                off += wd
            o_ref[rows, :] = res_ref[rows, :] + _rms(y, g)
    else:
        k = pl.program_id(1)
        part = jnp.dot(lhs[0][...], w_ref[...], preferred_element_type=F32)

        @pl.when(k == 0)
        def _():
            o_ref[...] = part

        @pl.when(k > 0)
        def _():
            o_ref[...] += part

        @pl.when(k == nk - 1)
        def _():
            g = g_ref[...]

            def finish(c, _):
                rows = pl.ds(pl.multiple_of(c * ROW_CHUNK, ROW_CHUNK), ROW_CHUNK)
                o_ref[rows, :] = res_ref[rows, :] + _rms(o_ref[rows, :], g)
                return 0

            lax.fori_loop(0, o_ref.shape[0] // ROW_CHUNK, finish, 0)


def mm_norm_res(lhs_list, w, layer, g, res, bm, nk=1):
    m, n = res.shape
    kk = w.shape[1]
    assert m % bm == 0 and kk % nk == 0
    bk = kk // nk
    n_lhs = len(lhs_list)
    if nk == 1:
        lhs_specs = [pl.BlockSpec((bm, a.shape[1]), lambda i, k: (i, 0)) for a in lhs_list]
        w_spec = _resident((None, kk, n), lambda i, k: (layer, 0, 0))
    else:
        assert n_lhs == 1
        kblock = lambda i, k: jnp.where(i % 2 == 0, k, nk - 1 - k)
        lhs_specs = [pl.BlockSpec((bm, bk), lambda i, k: (i, kblock(i, k)))]
        w_spec = pl.BlockSpec((None, bk, n), lambda i, k: (layer, kblock(i, k), 0))
    return pl.pallas_call(
        functools.partial(_mm_norm_res_kernel, n_lhs=n_lhs, nk=nk),
        grid=(m // bm, nk),
        in_specs=lhs_specs + [
            w_spec,
            _resident((1, n), lambda i, k: (0, 0)),
            pl.BlockSpec((bm, n), lambda i, k: (i, 0)),
        ],
        out_specs=pl.BlockSpec((bm, n), lambda i, k: (i, 0)),
        out_shape=jax.ShapeDtypeStruct((m, n), F32),
        compiler_params=_params("parallel", "arbitrary"),
        name="mm_norm_res",
    )(*lhs_list, w, g.reshape(1, n), res)


def _gla_kernel(q_ref, k_ref, v_ref, z_ref, r_ref, w2_ref, bg_ref, gh_ref, tri_ref,
                o_ref, st_ref, *, dk, dv):
    tl = q_ref.shape[0]
    heads = q_ref.shape[1] // dk
    n_chunks = tl // GLA_CHUNK

    @pl.when(pl.program_id(2) == 0)
    def _():
        st_ref[...] = jnp.zeros_like(st_ref)

    pre = jnp.dot(z_ref[...].astype(BF16), w2_ref[...], preferred_element_type=F32) + bg_ref[...]
    logg = (jnp.minimum(pre, 0.0) - jnp.log1p(jnp.exp(-jnp.abs(pre)))) * (1.0 / GLA_TAU)
    hi = logg.astype(BF16)
    lo = (logg - hi.astype(F32)).astype(BF16)
    incl = tri_ref[...]
    cs = jnp.dot(incl, jnp.concatenate([hi, lo], axis=1), preferred_element_type=F32)
    bcum_all = cs[:, :heads * dk] + cs[:, heads * dk:]
    gh = gh_ref[...]

    for hh in range(heads):
        ks = slice(hh * dk, (hh + 1) * dk)
        vs = slice(hh * dv, (hh + 1) * dv)
        bcum = bcum_all[:, ks]
        b_last = [bcum[(c + 1) * GLA_CHUNK - 1:(c + 1) * GLA_CHUNK, :] for c in range(n_chunks)]
        suf = jnp.concatenate([jnp.broadcast_to(bl, (GLA_CHUNK, dk)) for bl in b_last], axis=0) - bcum

        qf = q_ref[:, ks].astype(F32) * dk ** -0.5
        kf = k_ref[:, ks].astype(F32)
        q_dec = (qf * jnp.exp(bcum)).astype(BF16)
        k_intra = (kf * jnp.exp(-bcum)).astype(BF16)
        k_state = (kf * jnp.exp(suf)).astype(BF16)
        v = v_ref[:, vs]

        a = lax.dot_general(q_dec, k_intra, NT_DIMS, preferred_element_type=F32)
        a = jnp.where(incl > 0, a, 0.0).astype(BF16)
        o_intra = jnp.dot(a, v, preferred_element_type=F32)

        st = st_ref[hh]
        outs = []
        for c in range(n_chunks):
            rows = slice(c * GLA_CHUNK, (c + 1) * GLA_CHUNK)
            o_inter = lax.dot_general(q_dec[rows], st.astype(BF16), NT_DIMS,
                                      preferred_element_type=F32)
            outs.append(o_intra[rows] + o_inter)
            decay = jnp.exp(b_last[c])
            upd = lax.dot_general(v[rows], k_state[rows], TN_DIMS, preferred_element_type=F32)
            st = decay * st + upd
        st_ref[hh] = st
        o = _rms(jnp.concatenate(outs, axis=0), gh)
        r = r_ref[:, vs]
        o_ref[:, vs] = (o * (r * (1.0 / (1.0 + jnp.exp(-r))))).astype(o_ref.dtype)


def gla(pa, pb, w2p, bgate, ghead, tri, tl, heads_per_step):
    b, l, _ = pa.shape
    dk = LANES
    dv = 2 * dk
    hp = heads_per_step
    assert GLA_HEADS % hp == 0
    groups = GLA_HEADS // hp
    wk, wv = hp * dk, hp * dv
    kb0 = GLA_HEADS * dk // wk
    vb0 = 2 * GLA_HEADS * dk // wv
    zb = GLA_HEADS * dv // LANES
    return pl.pallas_call(
        functools.partial(_gla_kernel, dk=dk, dv=dv),
        grid=(b, groups, l // tl),
        in_specs=[
            pl.BlockSpec((None, tl, wk), lambda bi, h, t: (bi, t, h)),
            pl.BlockSpec((None, tl, wk), lambda bi, h, t: (bi, t, kb0 + h)),
            pl.BlockSpec((None, tl, wv), lambda bi, h, t: (bi, t, vb0 + h)),
            pl.BlockSpec((None, tl, LANES), lambda bi, h, t: (bi, t, zb)),
            pl.BlockSpec((None, tl, wv), lambda bi, h, t: (bi, t, h)),
            pl.BlockSpec((LANES, wk), lambda bi, h, t: (0, h)),
            pl.BlockSpec((1, wk), lambda bi, h, t: (0, h)),
            _resident((1, dv), lambda bi, h, t: (0, 0)),
            _resident((tl, tl), lambda bi, h, t: (0, 0)),
        ],
        out_specs=pl.BlockSpec((None, tl, wv), lambda bi, h, t: (bi, t, h)),
        out_shape=jax.ShapeDtypeStruct((b, l, GLA_HEADS * dv), BF16),
        scratch_shapes=[pltpu.VMEM((hp, dv, dk), F32)],
        compiler_params=_params("parallel", "parallel", "arbitrary"),
        name="gla",
    )(pa, pa, pa, pb, pb, w2p, bgate, ghead, tri)


def _for_each_block_paired(n, body):
    def pair(p, _):
        body(2 * p)
        body(2 * p + 1)
        return 0

    lax.fori_loop(0, n >> 1, pair, 0)

    @pl.when(n & 1 == 1)
    def _():
        body(n - 1)


def _dsa_kernel(q_ref, iq_ref, iw_ref, k_ref, v_ref, ik_ref, corr_ref, o_ref,
                key_ref, hi_ref, lo_ref, madd_ref, wt_ref, m_ref, l_ref, acc_ref, vt_ref, s_ref,
                *, topk):
    tq = q_ref.shape[0]
    tk = tq
    dh = LANES
    i = pl.program_id(1)
    nkb = i + 1
    scale = dh ** -0.5

    wt_ref[...] = (iw_ref[...] * (IDX_HEADS ** -0.5 * IDX_DH ** -0.5)).T
    qpos = lax.broadcasted_iota(jnp.int32, (tk, tq), 1) + i * tq
    kpos0 = lax.broadcasted_iota(jnp.int32, (tk, tq), 0)

    def score_block(kb):
        ik = ik_ref[pl.ds(pl.multiple_of(kb * tk, tk), tk), :].astype(BF16)
        ik_even, ik_odd = ik[:, :LANES], ik[:, LANES:]
        acc = jnp.zeros((tk, tq), F32)
        for p in range(IDX_HEADS // 2):
            qp = iq_ref[:, p * LANES:(p + 1) * LANES]
            for par, ikx in ((0, ik_even), (1, ik_odd)):
                h = 2 * p + par
                d = lax.dot_general(ikx, qp, NT_DIMS, preferred_element_type=F32)
                acc = acc + jnp.maximum(d, 0.0) * wt_ref[h:h + 1, :]
        sc = jnp.where(kpos0 + kb * tk <= qpos, acc, -jnp.inf)
        bits = lax.bitcast_convert_type(sc, jnp.int32)
        key = bits ^ ((bits >> 31) & 0x7FFFFFFF)
        key_ref[kb] = key
        hi_ref[kb] = (key >> 16).astype(jnp.int16)
        lo_ref[kb] = ((key & 0xFFFF) - HALF16).astype(jnp.int16)

    _for_each_block_paired(nkb, score_block)

    n_acc = 4
    slabs = tk // PACK16
    one16, zero16 = jnp.int16(1), jnp.int16(0)

    def rows16(x):
        return jnp.broadcast_to(x, (PACK16, tq)).astype(jnp.int16)

    def count16(src_ref, cand):
        cand16 = rows16(cand)

        def block(kb, cnts):
            cnts = list(cnts)
            for r in range(slabs):
                hit = src_ref[kb, r * PACK16:(r + 1) * PACK16, :] >= cand16
                cnts[r % n_acc] = cnts[r % n_acc] + jnp.where(hit, one16, zero16)
            return tuple(cnts)

        def pair(p, cnts):
            return block(2 * p + 1, block(2 * p, cnts))

        cnts = lax.fori_loop(0, nkb >> 1, pair, (jnp.zeros((PACK16, tq), jnp.int16),) * n_acc)
        cnts = lax.cond(nkb & 1 == 1, lambda c: block(nkb - 1, c), lambda c: c, cnts)
        cnt = ((cnts[0] + cnts[1]) + (cnts[2] + cnts[3])).astype(jnp.int32)
        return jnp.sum(cnt, axis=0, keepdims=True)

    def bisect16(src_ref, kth):
        def step(s, carry):
            t, above = carry
            cand = t + lax.shift_left(jnp.int32(1), 15 - s)
            total = count16(src_ref, cand)
            ok = total >= kth
            return jnp.where(ok, cand, t), jnp.where(ok, above, total)

        return lax.fori_loop(0, 16, step, (jnp.full((1, tq), -HALF16, jnp.int32),
                                           jnp.zeros((1, tq), jnp.int32)))

    t_hi, above = bisect16(hi_ref, topk)
    t_hi16 = rows16(t_hi)

    def keep_threshold_bucket(kb, _):
        for r in range(slabs):
            rows = slice(r * PACK16, (r + 1) * PACK16)
            lo_ref[kb, rows, :] = jnp.where(hi_ref[kb, rows, :] == t_hi16, lo_ref[kb, rows, :],
                                            jnp.int16(-HALF16))
        return 0

    lax.fori_loop(0, nkb, keep_threshold_bucket, 0)
    t_lo, _ = bisect16(lo_ref, topk - above)
    thr = t_hi * (2 * HALF16) + (t_lo + HALF16)

    def mask_block(kb, _):
        vis = kpos0 + kb * tk <= qpos
        madd_ref[kb] = jnp.where(vis, jnp.where(key_ref[kb] >= thr, 0.0, NEG_BIG), NEG_BIG)
        return 0

    lax.fori_loop(0, nkb, mask_block, 0)

    @pl.when(i == 0)
    def _():
        def transpose_values(kb, _):
            vt_ref[kb] = v_ref[pl.ds(pl.multiple_of(kb * tk, tk), tk), :].T
            return 0

        lax.fori_loop(0, vt_ref.shape[0], transpose_values, 0)

    m_ref[...] = jnp.full(m_ref.shape, NEG_BIG, F32)
    l_ref[...] = jnp.zeros(l_ref.shape, F32)
    acc_ref[...] = jnp.zeros(acc_ref.shape, F32)

    def attend(kb, near, buf=0):
        keys = pl.ds(pl.multiple_of(kb * tk, tk), tk)
        madd = madd_ref[kb]
        for h in range(DSA_HEADS):
            hs = slice(h * dh, (h + 1) * dh)
            s_ref[buf, h] = lax.dot_general(k_ref[keys, hs], q_ref[:, hs], NT_DIMS,
                                            preferred_element_type=F32)
        for h in range(DSA_HEADS):
            hs = slice(h * dh, (h + 1) * dh)
            s = s_ref[buf, h] * (scale * LOG2E)
            if near is not None:
                s = s + corr_ref[h, near]
            s = s + madd
            m_prev = m_ref[h, 0:1, :]
            m_next = jnp.maximum(m_prev, jnp.max(s, axis=0, keepdims=True))
            alpha = jnp.exp2(m_prev - m_next)
            p = jnp.exp2(s - m_next)
            l_next = alpha * l_ref[h, 0:1, :] + jnp.sum(p, axis=0, keepdims=True)
            pv = jnp.dot(vt_ref[kb, hs, :], p.astype(BF16),
                         preferred_element_type=F32)
            acc_ref[h] = alpha * acc_ref[h] + pv
            m_ref[h] = jnp.broadcast_to(m_next, (SUBLANES, tq))
            l_ref[h] = jnp.broadcast_to(l_next, (SUBLANES, tq))

    def far_pair(p, _):
        attend(2 * p, None, 0)
        attend(2 * p + 1, None, 1)
        return 0

    n_far = jnp.maximum(i - 1, 0)
    lax.fori_loop(0, n_far >> 1, far_pair, 0)

    @pl.when(n_far & 1 == 1)
    def _():
        attend(n_far - 1, None)

    @pl.when(i >= 1)
    def _():
        attend(i - 1, 1, 0)
        attend(i, 0, 1)

    @pl.when(i == 0)
    def _():
        attend(i, 0)

    for h in range(DSA_HEADS):
        o_h = acc_ref[h] / l_ref[h, 0:1, :]
        o_ref[:, h * dh:(h + 1) * dh] = o_h.T.astype(o_ref.dtype)


def dsa(pa, pb, corr, tq):
    b, l, _ = pa.shape
    dh = LANES
    hd = DSA_HEADS * dh
    qb = (2 * GLA_HEADS * LANES + GLA_HEADS * 2 * LANES) // hd
    iwb = GLA_HEADS * 2 * LANES // LANES + 1
    ikb = (iwb + 1) * LANES // (2 * LANES)
    topk = min(TOPK_MAX, l // 4)
    assert l % tq == 0 and tq >= topk and tq >= REL_MAX_DIST
    nkb = l // tq
    assert nkb * tq // PACK16 < HALF16
    return pl.pallas_call(
        functools.partial(_dsa_kernel, topk=topk),
        grid=(b, l // tq),
        in_specs=[
            pl.BlockSpec((None, tq, hd), lambda bi, i: (bi, i, qb)),
            pl.BlockSpec((None, tq, IDX_HEADS * IDX_DH), lambda bi, i: (bi, i, qb + 3)),
            pl.BlockSpec((None, tq, LANES), lambda bi, i: (bi, i, iwb)),
            _resident((None, l, hd), lambda bi, i: (bi, 0, qb + 1)),
            _resident((None, l, hd), lambda bi, i: (bi, 0, qb + 2)),
            _resident((None, l, 2 * LANES), lambda bi, i: (bi, 0, ikb)),
            _resident((DSA_HEADS, 2, tq, tq), lambda bi, i: (0, 0, 0, 0)),
        ],
        out_specs=pl.BlockSpec((None, tq, hd), lambda bi, i: (bi, i, 0)),
        out_shape=jax.ShapeDtypeStruct((b, l, hd), BF16),
        scratch_shapes=[
            pltpu.VMEM((nkb, tq, tq), jnp.int32),
            pltpu.VMEM((nkb, tq, tq), jnp.int16),
            pltpu.VMEM((nkb, tq, tq), jnp.int16),
            pltpu.VMEM((nkb, tq, tq), F32),
            pltpu.VMEM((LANES, tq), F32),
            pltpu.VMEM((DSA_HEADS, SUBLANES, tq), F32),
            pltpu.VMEM((DSA_HEADS, SUBLANES, tq), F32),
            pltpu.VMEM((DSA_HEADS, dh, tq), F32),
            pltpu.VMEM((nkb, hd, tq), BF16),
            pltpu.VMEM((2, DSA_HEADS, tq, tq), F32),
        ],
        compiler_params=_params("parallel", "arbitrary"),
        name="dsa",
    )(pa, pa, pb, pa, pa, pb, corr)


def _xattn_kernel(x_ref, gq_ref, wq_ref, kv_ref, wo_ref, go_ref, o_ref, q_ref, a_ref):
    d = x_ref.shape[1]
    dh = d // XA_HEADS
    scale = dh ** -0.5
    halves = _row_parts(x_ref.shape[0], 2)
    gq = gq_ref[...]
    for rows in halves:
        h = _rms(x_ref[rows, :], gq).astype(BF16)
        q_ref[rows, :] = jnp.dot(h, wq_ref[...], preferred_element_type=F32).astype(q_ref.dtype)
    for h in range(XA_HEADS):
        hs = slice(h * dh, (h + 1) * dh)
        s = lax.dot_general(q_ref[:, hs], kv_ref[:, hs], NT_DIMS, preferred_element_type=F32) * scale
        e = jnp.exp(s - jnp.max(s, axis=-1, keepdims=True))
        o = jnp.dot(e.astype(BF16), kv_ref[:, d + h * dh:d + (h + 1) * dh], preferred_element_type=F32)
        a_ref[:, hs] = (o / jnp.sum(e, axis=-1, keepdims=True)).astype(a_ref.dtype)
    go = go_ref[...]
    for rows in halves:
        y = jnp.dot(a_ref[rows, :], wo_ref[...], preferred_element_type=F32)
        o_ref[rows, :] = x_ref[rows, :] + _rms(y, go)


def xattn(x, kv, gq, wq, go, wo, layer, tq):
    b, l, d = x.shape
    nm = kv.shape[1]
    vec = lambda: _resident((1, d), lambda bi, i: (0, 0))
    mat = lambda: _resident((None, d, d), lambda bi, i: (layer, 0, 0))
    return pl.pallas_call(
        _xattn_kernel,
        grid=(b, l // tq),
        in_specs=[
            pl.BlockSpec((None, tq, d), lambda bi, i: (bi, i, 0)),
            vec(), mat(),
            pl.BlockSpec((None, nm, 2 * d), lambda bi, i: (bi, 0, 0)),
            mat(), vec(),
        ],
        out_specs=pl.BlockSpec((None, tq, d), lambda bi, i: (bi, i, 0)),
        out_shape=jax.ShapeDtypeStruct((b, l, d), F32),
        scratch_shapes=[pltpu.VMEM((tq, d), BF16)] * 2,
        compiler_params=_params("parallel", "parallel"),
        name="xattn",
    )(x, gq.reshape(1, d), wq, kv, wo, go.reshape(1, d))


def _rel_bucket(dist):
    n = jnp.maximum(dist, 0)
    max_exact = REL_BUCKETS // 2
    nf = jnp.maximum(n, 1).astype(F32)
    large = max_exact + (jnp.log(nf / max_exact) / math.log(REL_MAX_DIST / max_exact)
                         * (REL_BUCKETS - max_exact)).astype(jnp.int32)
    large = jnp.minimum(large, REL_BUCKETS - 1)
    return jnp.where(n < max_exact, n, large)


def _bias_correction_tiles(rel_table, tq):
    t = jnp.arange(tq)
    d0 = t[:, None] - t[None, :]
    buckets = jnp.stack([_rel_bucket(d0), _rel_bucket(d0 + tq)], axis=0)
    onehot = (buckets[..., None] == jnp.arange(REL_BUCKETS)).astype(F32)
    far = rel_table[_rel_bucket(jnp.int32(2 * tq))]
    return jnp.einsum("ntsk,kh->hnst", onehot, (rel_table - far) * LOG2E,
                      precision=lax.Precision.HIGHEST)


def _chunk_sum_matrices(tl):
    r = jnp.arange(tl)
    same = (r[:, None] // GLA_CHUNK) == (r[None, :] // GLA_CHUNK)
    return (same & (r[None, :] <= r[:, None])).astype(BF16)


def _split_in_proj(w):
    depth, d, _ = w.shape
    half = d // 2
    sizes = [GLA_HEADS * LANES, GLA_HEADS * LANES, half, GLA_RANK, half,
             half, half, half, IDX_HEADS * IDX_DH, IDX_DH, IDX_HEADS]
    offs = [0]
    for s in sizes:
        offs.append(offs[-1] + s)
    assert offs[-1] == w.shape[2]
    seg = lambda n: w[:, :, offs[n]:offs[n + 1]]
    zero = lambda n: jnp.zeros((depth, d, n), w.dtype)
    ik = seg(9)
    w_head = w[:, :, :offs[3]]
    w_mid = w[:, :, offs[5]:offs[9]]
    w_small = jnp.concatenate([seg(4), seg(3), zero(LANES - GLA_RANK), seg(10), zero(LANES - IDX_HEADS),
                               ik, zero(2 * IDX_DH), ik], axis=2)
    return w_head.astype(BF16), w_mid.astype(BF16), w_small.astype(BF16)


def kernel(x, mem, norm_gains, w_in, w_gate_up, b_gate, gla_head_gain, rel_table,
           w_out, w_xq, w_xkv, w_xo, w_ffn_in, w_ffn_out):
    b, l, d = x.shape
    nm = mem.shape[1]
    depth = w_in.shape[0]
    d_ff = w_ffn_out.shape[1]
    m = b * l
    assert d // 2 // GLA_HEADS == 2 * LANES and d // 2 // DSA_HEADS == LANES

    t = TILES
    corr = _bias_correction_tiles(rel_table, t["dsa_q"])
    tri = _chunk_sum_matrices(t["gla_t"])
    w2p = jnp.pad(w_gate_up, ((0, 0), (0, LANES - GLA_RANK), (0, 0))).astype(BF16)

    w_out, w_xq, w_xkv, w_xo, w_ffn_in, w_ffn_out = (
        w.astype(BF16) for w in (w_out, w_xq, w_xkv, w_xo, w_ffn_in, w_ffn_out))
    w_head, w_mid, w_small = _split_in_proj(w_in)

    xf = x.reshape(m, d)
    memf = mem.reshape(b * nm, d)
    for li in range(depth):
        g = norm_gains[li]
        pa, pb = in_proj(xf, g[0], w_head, w_mid, w_small, li, *t["in_proj"])
        pa, pb = pa.reshape(b, l, -1), pb.reshape(b, l, -1)
        o_gla = gla(pa, pb, w2p[li], b_gate[li].reshape(1, -1),
                    gla_head_gain[li].reshape(1, -1), tri, t["gla_t"], heads_per_step=GLA_HEADS)
        o_dsa = dsa(pa, pb, corr, t["dsa_q"])
        xf = mm_norm_res([o_gla.reshape(m, -1), o_dsa.reshape(m, -1)], w_out, li, g[1], xf,
                         bm=t["out_proj_m"])
        kv = norm_matmul(memf, g[6], w_xkv, li, BF16, bm=min(t["mem_kv"][0], b * nm), bn=t["mem_kv"][1])
        xf = xattn(xf.reshape(b, l, d), kv.reshape(b, nm, 2 * d), g[2], w_xq, g[3], w_xo, li,
                   tq=t["xattn_q"]).reshape(m, d)
        act = ffn_in(xf, g[4], w_ffn_in, li, d_ff, *t["ffn_in"])
        xf = mm_norm_res([act], w_ffn_out, li, g[5], xf, bm=t["ffn_out_m"])
    return xf.reshape(b, l, d)
```

```python
import functools
import math

import jax
import jax.numpy as jnp
from jax import lax
from jax.experimental import pallas as pl
from jax.experimental.pallas import tpu as pltpu

F32 = jnp.float32
BF16 = jnp.bfloat16

EPS = 1e-6
GLA_HEADS = 4
GLA_RANK = 16
GLA_TAU = 16.0
GLA_CHUNK = 64
DSA_HEADS = 8
IDX_HEADS = 16
IDX_DH = 64
TOPK_MAX = 256
REL_BUCKETS = 32
REL_MAX_DIST = 128
XA_HEADS = 4

LANES = 128
SUBLANES = 8
VMEM_LIMIT_BYTES = 60000 * 1024

PACK16 = 16
ROW_CHUNK = 128
NORM_PARTS = 4

NEG_BIG = -1e30
HALF16 = 2 ** 15
LOG2E = 1.4426950408889634

TILES = dict(
    in_proj=(1024, 1024, 768),
    gla_t=512,
    dsa_q=256,
    out_proj_m=1024,
    mem_kv=(512, 1024),
    xattn_q=512,
    ffn_in=(1024, 512),
    ffn_out_m=512,
)

NT_DIMS = (((1,), (1,)), ((), ()))
TN_DIMS = (((0,), (0,)), ((), ()))


def _params(*sem):
    return pltpu.CompilerParams(dimension_semantics=sem, vmem_limit_bytes=VMEM_LIMIT_BYTES)


def _resident(shape, index_map):
    return pl.BlockSpec(shape, index_map, pipeline_mode=pl.Buffered(1))


def _rms(x, g):
    return x * lax.rsqrt(jnp.mean(x * x, axis=-1, keepdims=True) + EPS) * g


def _row_parts(n_rows, parts):
    size = n_rows // parts
    return [slice(p * size, (p + 1) * size) for p in range(parts)]


def _rms_to(x_ref, g_ref, h_ref):
    g = g_ref[...]

    def body(c, _):
        rows = pl.ds(pl.multiple_of(c * ROW_CHUNK, ROW_CHUNK), ROW_CHUNK)
        h_ref[rows, :] = _rms(x_ref[rows, :], g).astype(h_ref.dtype)
        return 0

    lax.fori_loop(0, x_ref.shape[0] // ROW_CHUNK, body, 0)


def _norm_matmul_kernel(x_ref, g_ref, w_ref, o_ref, h_ref):
    @pl.when(pl.program_id(1) == 0)
    def _():
        _rms_to(x_ref, g_ref, h_ref)

    o_ref[...] = jnp.dot(h_ref[...], w_ref[...], preferred_element_type=F32).astype(o_ref.dtype)


def norm_matmul(x, g, w, layer, out_dtype, bm, bn):
    m, k = x.shape
    n = w.shape[2]
    assert m % bm == 0 and n % bn == 0
    return pl.pallas_call(
        _norm_matmul_kernel,
        grid=(m // bm, n // bn),
        in_specs=[
            pl.BlockSpec((bm, k), lambda i, j: (i, 0)),
            _resident((1, k), lambda i, j: (0, 0)),
            pl.BlockSpec((None, k, bn), lambda i, j: (layer, 0, j)),
        ],
        out_specs=pl.BlockSpec((bm, bn), lambda i, j: (i, j)),
        out_shape=jax.ShapeDtypeStruct((m, n), out_dtype),
        scratch_shapes=[pltpu.VMEM((bm, k), BF16)],
        compiler_params=_params("parallel", "arbitrary"),
        name="norm_matmul",
    )(x, g.reshape(1, k), w)


def _in_proj_slot(i, j, n_slots):
    return jnp.where(i % 2 == 0, j, n_slots - 1 - j)


def _in_proj_kernel(x_ref, g_ref, wh_ref, wm_ref, ws_ref, oa_ref, ob_ref, h_ref, *, nh, ns, nm):
    s = _in_proj_slot(pl.program_id(0), pl.program_id(1), nh + ns + nm)

    first = pl.program_id(1) == 0

    def project(w_ref, o_ref):
        o_ref[...] = jnp.dot(h_ref[...], w_ref[...], preferred_element_type=F32).astype(o_ref.dtype)

    def project_first(w_ref, o_ref):
        g = g_ref[...]
        for rows in _row_parts(x_ref.shape[0], NORM_PARTS):
            h = _rms(x_ref[rows, :], g).astype(h_ref.dtype)
            h_ref[rows, :] = h
            o_ref[rows, :] = jnp.dot(h, w_ref[...], preferred_element_type=F32).astype(o_ref.dtype)

    pl.when(first & (s < nh))(lambda: project_first(wh_ref, oa_ref))
    pl.when(first & (s >= nh + ns))(lambda: project_first(wm_ref, oa_ref))
    pl.when(jnp.logical_not(first) & (s < nh))(lambda: project(wh_ref, oa_ref))
    pl.when((s >= nh) & (s < nh + ns))(lambda: project(ws_ref, ob_ref))
    pl.when(jnp.logical_not(first) & (s >= nh + ns))(lambda: project(wm_ref, oa_ref))


def in_proj(x, g, w_head, w_mid, w_small, layer, bm, bn, bs):
    m, k = x.shape
    n_head, n_mid, n_small = w_head.shape[2], w_mid.shape[2], w_small.shape[2]
    assert m % bm == 0 and n_head % bn == 0 and n_mid % bn == 0 and n_small % bs == 0
    nh, nm, ns = n_head // bn, n_mid // bn, n_small // bs
    n_slots = nh + ns + nm

    def slot(i, j):
        return _in_proj_slot(i, j, n_slots)

    def pa_block(i, j):
        s = slot(i, j)
        hold = jnp.where(i % 2 == 0, nh - 1, nh)
        return jnp.where(s < nh, s, jnp.where(s >= nh + ns, s - ns, hold))

    return pl.pallas_call(
        functools.partial(_in_proj_kernel, nh=nh, ns=ns, nm=nm),
        grid=(m // bm, n_slots),
        in_specs=[
            pl.BlockSpec((bm, k), lambda i, j: (i, 0)),
            _resident((1, k), lambda i, j: (0, 0)),
            pl.BlockSpec((None, k, bn), lambda i, j: (layer, 0, jnp.clip(slot(i, j), 0, nh - 1))),
            pl.BlockSpec((None, k, bn), lambda i, j: (layer, 0, jnp.clip(slot(i, j) - nh - ns, 0, nm - 1))),
            pl.BlockSpec((None, k, bs), lambda i, j: (layer, 0, jnp.clip(slot(i, j) - nh, 0, ns - 1))),
        ],
        out_specs=[
            pl.BlockSpec((bm, bn), lambda i, j: (i, pa_block(i, j))),
            pl.BlockSpec((bm, bs), lambda i, j: (i, jnp.clip(slot(i, j) - nh, 0, ns - 1))),
        ],
        out_shape=[
            jax.ShapeDtypeStruct((m, n_head + n_mid), BF16),
            jax.ShapeDtypeStruct((m, n_small), F32),
        ],
        scratch_shapes=[pltpu.VMEM((bm, k), BF16)],
        compiler_params=_params("parallel", "arbitrary"),
        name="in_proj",
    )(x, g.reshape(1, k), w_head, w_mid, w_small)


def _ffn_in_kernel(x_ref, g_ref, wg_ref, wu_ref, o_ref, h_ref):
    def gated(h):
        a = jnp.dot(h, wg_ref[...], preferred_element_type=F32)
        b = jnp.dot(h, wu_ref[...], preferred_element_type=F32)
        return (a * (1.0 / (1.0 + jnp.exp(-a))) * b).astype(o_ref.dtype)

    @pl.when(pl.program_id(1) == 0)
    def _():
        g = g_ref[...]
        for rows in _row_parts(x_ref.shape[0], NORM_PARTS):
            h = _rms(x_ref[rows, :], g).astype(h_ref.dtype)
            h_ref[rows, :] = h
            o_ref[rows, :] = gated(h)

    @pl.when(pl.program_id(1) > 0)
    def _():
        o_ref[...] = gated(h_ref[...])


def ffn_in(x, g, w, layer, d_ff, bm, bn):
    m, k = x.shape
    assert m % bm == 0 and d_ff % bn == 0
    nj = d_ff // bn
    return pl.pallas_call(
        _ffn_in_kernel,
        grid=(m // bm, nj),
        in_specs=[
            pl.BlockSpec((bm, k), lambda i, j: (i, 0)),
            _resident((1, k), lambda i, j: (0, 0)),
            pl.BlockSpec((None, k, bn), lambda i, j: (layer, 0, j)),
            pl.BlockSpec((None, k, bn), lambda i, j: (layer, 0, j + nj)),
        ],
        out_specs=pl.BlockSpec((bm, bn), lambda i, j: (i, j)),
        out_shape=jax.ShapeDtypeStruct((m, d_ff), BF16),
        scratch_shapes=[pltpu.VMEM((bm, k), BF16)],
        compiler_params=_params("parallel", "arbitrary"),
        name="ffn_in",
    )(x, g.reshape(1, k), w, w)


def _mm_norm_res_kernel(*refs, n_lhs, nk):
    lhs = refs[:n_lhs]
    w_ref, g_ref, res_ref, o_ref = refs[n_lhs:n_lhs + 4]
    if nk == 1:
        g = g_ref[...]
        half = o_ref.shape[0] // 2
        for rows in (slice(0, half), slice(half, 2 * half)):
            y = None
            off = 0
            for a in lhs:
                wd = a.shape[1]
                part = jnp.dot(a[rows, :], w_ref[off:off + wd, :], preferred_element_type=F32)
                y = part if y is None else y + part
                off += wd
            o_ref[rows, :] = res_ref[rows, :] + _rms(y, g)
    else:
        k = pl.program_id(1)
        part = jnp.dot(lhs[0][...], w_ref[...], preferred_element_type=F32)

        @pl.when(k == 0)
        def _():
            o_ref[...] = part

        @pl.when(k > 0)
        def _():
            o_ref[...] += part

        @pl.when(k == nk - 1)
        def _():
            g = g_ref[...]

            def finish(c, _):
                rows = pl.ds(pl.multiple_of(c * ROW_CHUNK, ROW_CHUNK), ROW_CHUNK)
                o_ref[rows, :] = res_ref[rows, :] + _rms(o_ref[rows, :], g)
                return 0

            lax.fori_loop(0, o_ref.shape[0] // ROW_CHUNK, finish, 0)


def mm_norm_res(lhs_list, w, layer, g, res, bm, nk=1):
    m, n = res.shape
    kk = w.shape[1]
    assert m % bm == 0 and kk % nk == 0
    bk = kk // nk
    n_lhs = len(lhs_list)
    if nk == 1:
        lhs_specs = [pl.BlockSpec((bm, a.shape[1]), lambda i, k: (i, 0)) for a in lhs_list]
        w_spec = _resident((None, kk, n), lambda i, k: (layer, 0, 0))
    else:
        assert n_lhs == 1
        kblock = lambda i, k: jnp.where(i % 2 == 0, k, nk - 1 - k)
        lhs_specs = [pl.BlockSpec((bm, bk), lambda i, k: (i, kblock(i, k)))]
        w_spec = pl.BlockSpec((None, bk, n), lambda i, k: (layer, kblock(i, k), 0))
    return pl.pallas_call(
        functools.partial(_mm_norm_res_kernel, n_lhs=n_lhs, nk=nk),
        grid=(m // bm, nk),
        in_specs=lhs_specs + [
            w_spec,
            _resident((1, n), lambda i, k: (0, 0)),
            pl.BlockSpec((bm, n), lambda i, k: (i, 0)),
        ],
        out_specs=pl.BlockSpec((bm, n), lambda i, k: (i, 0)),
        out_shape=jax.ShapeDtypeStruct((m, n), F32),
        compiler_params=_params("parallel", "arbitrary"),
        name="mm_norm_res",
    )(*lhs_list, w, g.reshape(1, n), res)


def _gla_kernel(q_ref, k_ref, v_ref, z_ref, r_ref, w2_ref, bg_ref, gh_ref, tri_ref,
                o_ref, st_ref, *, dk, dv):
    tl = q_ref.shape[0]
    heads = q_ref.shape[1] // dk
    n_chunks = tl // GLA_CHUNK

    @pl.when(pl.program_id(2) == 0)
    def _():
        st_ref[...] = jnp.zeros_like(st_ref)

    pre = jnp.dot(z_ref[...].astype(BF16), w2_ref[...], preferred_element_type=F32) + bg_ref[...]
    logg = (jnp.minimum(pre, 0.0) - jnp.log1p(jnp.exp(-jnp.abs(pre)))) * (1.0 / GLA_TAU)
    hi = logg.astype(BF16)
    lo = (logg - hi.astype(F32)).astype(BF16)
    incl = tri_ref[...]
    cs = jnp.dot(incl, jnp.concatenate([hi, lo], axis=1), preferred_element_type=F32)
    bcum_all = cs[:, :heads * dk] + cs[:, heads * dk:]
    gh = gh_ref[...]

    for hh in range(heads):
        ks = slice(hh * dk, (hh + 1) * dk)
        vs = slice(hh * dv, (hh + 1) * dv)
        bcum = bcum_all[:, ks]
        b_last = [bcum[(c + 1) * GLA_CHUNK - 1:(c + 1) * GLA_CHUNK, :] for c in range(n_chunks)]
        suf = jnp.concatenate([jnp.broadcast_to(bl, (GLA_CHUNK, dk)) for bl in b_last], axis=0) - bcum

        qf = q_ref[:, ks].astype(F32) * dk ** -0.5
        kf = k_ref[:, ks].astype(F32)
        q_dec = (qf * jnp.exp(bcum)).astype(BF16)
        k_intra = (kf * jnp.exp(-bcum)).astype(BF16)
        k_state = (kf * jnp.exp(suf)).astype(BF16)
        v = v_ref[:, vs]

        a = lax.dot_general(q_dec, k_intra, NT_DIMS, preferred_element_type=F32)
        a = jnp.where(incl > 0, a, 0.0).astype(BF16)
        o_intra = jnp.dot(a, v, preferred_element_type=F32)

        st = st_ref[hh]
        outs = []
        for c in range(n_chunks):
            rows = slice(c * GLA_CHUNK, (c + 1) * GLA_CHUNK)
            o_inter = lax.dot_general(q_dec[rows], st.astype(BF16), NT_DIMS,
                                      preferred_element_type=F32)
            outs.append(o_intra[rows] + o_inter)
            decay = jnp.exp(b_last[c])
            upd = lax.dot_general(v[rows], k_state[rows], TN_DIMS, preferred_element_type=F32)
            st = decay * st + upd
        st_ref[hh] = st
        o = _rms(jnp.concatenate(outs, axis=0), gh)
        r = r_ref[:, vs]
        o_ref[:, vs] = (o * (r * (1.0 / (1.0 + jnp.exp(-r))))).astype(o_ref.dtype)


def gla(pa, pb, w2p, bgate, ghead, tri, tl, heads_per_step):
    b, l, _ = pa.shape
    dk = LANES
    dv = 2 * dk
    hp = heads_per_step
    assert GLA_HEADS % hp == 0
    groups = GLA_HEADS // hp
    wk, wv = hp * dk, hp * dv
    kb0 = GLA_HEADS * dk // wk
    vb0 = 2 * GLA_HEADS * dk // wv
    zb = GLA_HEADS * dv // LANES
    return pl.pallas_call(
        functools.partial(_gla_kernel, dk=dk, dv=dv),
        grid=(b, groups, l // tl),
        in_specs=[
            pl.BlockSpec((None, tl, wk), lambda bi, h, t: (bi, t, h)),
            pl.BlockSpec((None, tl, wk), lambda bi, h, t: (bi, t, kb0 + h)),
            pl.BlockSpec((None, tl, wv), lambda bi, h, t: (bi, t, vb0 + h)),
            pl.BlockSpec((None, tl, LANES), lambda bi, h, t: (bi, t, zb)),
            pl.BlockSpec((None, tl, wv), lambda bi, h, t: (bi, t, h)),
            pl.BlockSpec((LANES, wk), lambda bi, h, t: (0, h)),
            pl.BlockSpec((1, wk), lambda bi, h, t: (0, h)),
            _resident((1, dv), lambda bi, h, t: (0, 0)),
            _resident((tl, tl), lambda bi, h, t: (0, 0)),
        ],
        out_specs=pl.BlockSpec((None, tl, wv), lambda bi, h, t: (bi, t, h)),
        out_shape=jax.ShapeDtypeStruct((b, l, GLA_HEADS * dv), BF16),
        scratch_shapes=[pltpu.VMEM((hp, dv, dk), F32)],
        compiler_params=_params("parallel", "parallel", "arbitrary"),
        name="gla",
    )(pa, pa, pa, pb, pb, w2p, bgate, ghead, tri)


def _for_each_block_paired(n, body):
    def pair(p, _):
        body(2 * p)
        body(2 * p + 1)
        return 0

    lax.fori_loop(0, n >> 1, pair, 0)

    @pl.when(n & 1 == 1)
    def _():
        body(n - 1)


def _dsa_kernel(q_ref, iq_ref, iw_ref, k_ref, v_ref, ik_ref, corr_ref, o_ref,
                key_ref, hi_ref, lo_ref, madd_ref, wt_ref, m_ref, l_ref, acc_ref, vt_ref, s_ref,
                *, topk):
    tq = q_ref.shape[0]
    tk = tq
    dh = LANES
    i = pl.program_id(1)
    nkb = i + 1
    scale = dh ** -0.5

    wt_ref[...] = (iw_ref[...] * (IDX_HEADS ** -0.5 * IDX_DH ** -0.5)).T
    qpos = lax.broadcasted_iota(jnp.int32, (tk, tq), 1) + i * tq
    kpos0 = lax.broadcasted_iota(jnp.int32, (tk, tq), 0)

    def score_block(kb):
        ik = ik_ref[pl.ds(pl.multiple_of(kb * tk, tk), tk), :].astype(BF16)
        ik_even, ik_odd = ik[:, :LANES], ik[:, LANES:]
        acc = jnp.zeros((tk, tq), F32)
        for p in range(IDX_HEADS // 2):
            qp = iq_ref[:, p * LANES:(p + 1) * LANES]
            for par, ikx in ((0, ik_even), (1, ik_odd)):
                h = 2 * p + par
                d = lax.dot_general(ikx, qp, NT_DIMS, preferred_element_type=F32)
                acc = acc + jnp.maximum(d, 0.0) * wt_ref[h:h + 1, :]
        sc = jnp.where(kpos0 + kb * tk <= qpos, acc, -jnp.inf)
        bits = lax.bitcast_convert_type(sc, jnp.int32)
        key = bits ^ ((bits >> 31) & 0x7FFFFFFF)
        key_ref[kb] = key
        hi_ref[kb] = (key >> 16).astype(jnp.int16)
        lo_ref[kb] = ((key & 0xFFFF) - HALF16).astype(jnp.int16)

    _for_each_block_paired(nkb, score_block)

    n_acc = 4
    slabs = tk // PACK16
    one16, zero16 = jnp.int16(1), jnp.int16(0)

    def rows16(x):
        return jnp.broadcast_to(x, (PACK16, tq)).astype(jnp.int16)

    def count16(src_ref, cand):
        cand16 = rows16(cand)

        def block(kb, cnts):
            cnts = list(cnts)
            for r in range(slabs):
                hit = src_ref[kb, r * PACK16:(r + 1) * PACK16, :] >= cand16
                cnts[r % n_acc] = cnts[r % n_acc] + jnp.where(hit, one16, zero16)
            return tuple(cnts)

        def pair(p, cnts):
            return block(2 * p + 1, block(2 * p, cnts))

        cnts = lax.fori_loop(0, nkb >> 1, pair, (jnp.zeros((PACK16, tq), jnp.int16),) * n_acc)
        cnts = lax.cond(nkb & 1 == 1, lambda c: block(nkb - 1, c), lambda c: c, cnts)
        cnt = ((cnts[0] + cnts[1]) + (cnts[2] + cnts[3])).astype(jnp.int32)
        return jnp.sum(cnt, axis=0, keepdims=True)

    def bisect16(src_ref, kth):
        def step(s, carry):
            t, above = carry
            cand = t + lax.shift_left(jnp.int32(1), 15 - s)
            total = count16(src_ref, cand)
            ok = total >= kth
            return jnp.where(ok, cand, t), jnp.where(ok, above, total)

        return lax.fori_loop(0, 16, step, (jnp.full((1, tq), -HALF16, jnp.int32),
                                           jnp.zeros((1, tq), jnp.int32)))

    t_hi, above = bisect16(hi_ref, topk)
    t_hi16 = rows16(t_hi)

    def keep_threshold_bucket(kb, _):
        for r in range(slabs):
            rows = slice(r * PACK16, (r + 1) * PACK16)
            lo_ref[kb, rows, :] = jnp.where(hi_ref[kb, rows, :] == t_hi16, lo_ref[kb, rows, :],
                                            jnp.int16(-HALF16))
        return 0

    lax.fori_loop(0, nkb, keep_threshold_bucket, 0)
    t_lo, _ = bisect16(lo_ref, topk - above)
    thr = t_hi * (2 * HALF16) + (t_lo + HALF16)

    def mask_block(kb, _):
        vis = kpos0 + kb * tk <= qpos
        madd_ref[kb] = jnp.where(vis, jnp.where(key_ref[kb] >= thr, 0.0, NEG_BIG), NEG_BIG)
        return 0

    lax.fori_loop(0, nkb, mask_block, 0)

    @pl.when(i == 0)
    def _():
        def transpose_values(kb, _):
            vt_ref[kb] = v_ref[pl.ds(pl.multiple_of(kb * tk, tk), tk), :].T
            return 0

        lax.fori_loop(0, vt_ref.shape[0], transpose_values, 0)

    m_ref[...] = jnp.full(m_ref.shape, NEG_BIG, F32)
    l_ref[...] = jnp.zeros(l_ref.shape, F32)
    acc_ref[...] = jnp.zeros(acc_ref.shape, F32)

    def attend(kb, near, buf=0):
        keys = pl.ds(pl.multiple_of(kb * tk, tk), tk)
        madd = madd_ref[kb]
        for h in range(DSA_HEADS):
            hs = slice(h * dh, (h + 1) * dh)
            s_ref[buf, h] = lax.dot_general(k_ref[keys, hs], q_ref[:, hs], NT_DIMS,
                                            preferred_element_type=F32)
        for h in range(DSA_HEADS):
            hs = slice(h * dh, (h + 1) * dh)
            s = s_ref[buf, h] * (scale * LOG2E)
            if near is not None:
                s = s + corr_ref[h, near]
            s = s + madd
            m_prev = m_ref[h, 0:1, :]
            m_next = jnp.maximum(m_prev, jnp.max(s, axis=0, keepdims=True))
            alpha = jnp.exp2(m_prev - m_next)
            p = jnp.exp2(s - m_next)
            l_next = alpha * l_ref[h, 0:1, :] + jnp.sum(p, axis=0, keepdims=True)
            pv = jnp.dot(vt_ref[kb, hs, :], p.astype(BF16),
                         preferred_element_type=F32)
            acc_ref[h] = alpha * acc_ref[h] + pv
            m_ref[h] = jnp.broadcast_to(m_next, (SUBLANES, tq))
            l_ref[h] = jnp.broadcast_to(l_next, (SUBLANES, tq))

    def far_pair(p, _):
        attend(2 * p, None, 0)
        attend(2 * p + 1, None, 1)
        return 0

    n_far = jnp.maximum(i - 1, 0)
    lax.fori_loop(0, n_far >> 1, far_pair, 0)

    @pl.when(n_far & 1 == 1)
    def _():
        attend(n_far - 1, None)

    @pl.when(i >= 1)
    def _():
        attend(i - 1, 1, 0)
        attend(i, 0, 1)

    @pl.when(i == 0)
    def _():
        attend(i, 0)

    for h in range(DSA_HEADS):
        o_h = acc_ref[h] / l_ref[h, 0:1, :]
        o_ref[:, h * dh:(h + 1) * dh] = o_h.T.astype(o_ref.dtype)


def dsa(pa, pb, corr, tq):
    b, l, _ = pa.shape
    dh = LANES
    hd = DSA_HEADS * dh
    qb = (2 * GLA_HEADS * LANES + GLA_HEADS * 2 * LANES) // hd
    iwb = GLA_HEADS * 2 * LANES // LANES + 1
    ikb = (iwb + 1) * LANES // (2 * LANES)
    topk = min(TOPK_MAX, l // 4)
    assert l % tq == 0 and tq >= topk and tq >= REL_MAX_DIST
    nkb = l // tq
    assert nkb * tq // PACK16 < HALF16
    return pl.pallas_call(
        functools.partial(_dsa_kernel, topk=topk),
        grid=(b, l // tq),
        in_specs=[
            pl.BlockSpec((None, tq, hd), lambda bi, i: (bi, i, qb)),
            pl.BlockSpec((None, tq, IDX_HEADS * IDX_DH), lambda bi, i: (bi, i, qb + 3)),
            pl.BlockSpec((None, tq, LANES), lambda bi, i: (bi, i, iwb)),
            _resident((None, l, hd), lambda bi, i: (bi, 0, qb + 1)),
            _resident((None, l, hd), lambda bi, i: (bi, 0, qb + 2)),
            _resident((None, l, 2 * LANES), lambda bi, i: (bi, 0, ikb)),
            _resident((DSA_HEADS, 2, tq, tq), lambda bi, i: (0, 0, 0, 0)),
        ],
        out_specs=pl.BlockSpec((None, tq, hd), lambda bi, i: (bi, i, 0)),
        out_shape=jax.ShapeDtypeStruct((b, l, hd), BF16),
        scratch_shapes=[
            pltpu.VMEM((nkb, tq, tq), jnp.int32),
            pltpu.VMEM((nkb, tq, tq), jnp.int16),
            pltpu.VMEM((nkb, tq, tq), jnp.int16),
            pltpu.VMEM((nkb, tq, tq), F32),
            pltpu.VMEM((LANES, tq), F32),
            pltpu.VMEM((DSA_HEADS, SUBLANES, tq), F32),
            pltpu.VMEM((DSA_HEADS, SUBLANES, tq), F32),
            pltpu.VMEM((DSA_HEADS, dh, tq), F32),
            pltpu.VMEM((nkb, hd, tq), BF16),
            pltpu.VMEM((2, DSA_HEADS, tq, tq), F32),
        ],
        compiler_params=_params("parallel", "arbitrary"),
        name="dsa",
    )(pa, pa, pb, pa, pa, pb, corr)


def _xattn_kernel(x_ref, gq_ref, wq_ref, kv_ref, wo_ref, go_ref, o_ref, q_ref, a_ref):
    d = x_ref.shape[1]
    dh = d // XA_HEADS
    scale = dh ** -0.5
    halves = _row_parts(x_ref.shape[0], 2)
    gq = gq_ref[...]
    for rows in halves:
        h = _rms(x_ref[rows, :], gq).astype(BF16)
        q_ref[rows, :] = jnp.dot(h, wq_ref[...], preferred_element_type=F32).astype(q_ref.dtype)
    heads = [slice(h * dh, (h + 1) * dh) for h in range(XA_HEADS)]
    scores = [lax.dot_general(q_ref[:, hs], kv_ref[:, hs], NT_DIMS, preferred_element_type=F32)
              for hs in heads]
    for h, hs in enumerate(heads):
        s = scores[h] * scale
        e = jnp.exp(s - jnp.max(s, axis=-1, keepdims=True))
        o = jnp.dot(e.astype(BF16), kv_ref[:, d + h * dh:d + (h + 1) * dh], preferred_element_type=F32)
        a_ref[:, hs] = (o / jnp.sum(e, axis=-1, keepdims=True)).astype(a_ref.dtype)
    go = go_ref[...]
    for rows in halves:
        y = jnp.dot(a_ref[rows, :], wo_ref[...], preferred_element_type=F32)
        o_ref[rows, :] = x_ref[rows, :] + _rms(y, go)


def xattn(x, kv, gq, wq, go, wo, layer, tq):
    b, l, d = x.shape
    nm = kv.shape[1]
    vec = lambda: _resident((1, d), lambda bi, i: (0, 0))
    mat = lambda: _resident((None, d, d), lambda bi, i: (layer, 0, 0))
    return pl.pallas_call(
        _xattn_kernel,
        grid=(b, l // tq),
        in_specs=[
            pl.BlockSpec((None, tq, d), lambda bi, i: (bi, i, 0)),
            vec(), mat(),
            pl.BlockSpec((None, nm, 2 * d), lambda bi, i: (bi, 0, 0)),
            mat(), vec(),
        ],
        out_specs=pl.BlockSpec((None, tq, d), lambda bi, i: (bi, i, 0)),
        out_shape=jax.ShapeDtypeStruct((b, l, d), F32),
        scratch_shapes=[pltpu.VMEM((tq, d), BF16)] * 2,
        compiler_params=_params("parallel", "parallel"),
        name="xattn",
    )(x, gq.reshape(1, d), wq, kv, wo, go.reshape(1, d))


def _rel_bucket(dist):
    n = jnp.maximum(dist, 0)
    max_exact = REL_BUCKETS // 2
    nf = jnp.maximum(n, 1).astype(F32)
    large = max_exact + (jnp.log(nf / max_exact) / math.log(REL_MAX_DIST / max_exact)
                         * (REL_BUCKETS - max_exact)).astype(jnp.int32)
    large = jnp.minimum(large, REL_BUCKETS - 1)
    return jnp.where(n < max_exact, n, large)


def _bias_correction_tiles(rel_table, tq):
    t = jnp.arange(tq)
    d0 = t[:, None] - t[None, :]
    buckets = jnp.stack([_rel_bucket(d0), _rel_bucket(d0 + tq)], axis=0)
    onehot = (buckets[..., None] == jnp.arange(REL_BUCKETS)).astype(F32)
    far = rel_table[_rel_bucket(jnp.int32(2 * tq))]
    return jnp.einsum("ntsk,kh->hnst", onehot, (rel_table - far) * LOG2E,
                      precision=lax.Precision.HIGHEST)


def _chunk_sum_matrices(tl):
    r = jnp.arange(tl)
    same = (r[:, None] // GLA_CHUNK) == (r[None, :] // GLA_CHUNK)
    return (same & (r[None, :] <= r[:, None])).astype(BF16)


def _split_in_proj(w):
    depth, d, _ = w.shape
    half = d // 2
    sizes = [GLA_HEADS * LANES, GLA_HEADS * LANES, half, GLA_RANK, half,
             half, half, half, IDX_HEADS * IDX_DH, IDX_DH, IDX_HEADS]
    offs = [0]
    for s in sizes:
        offs.append(offs[-1] + s)
    assert offs[-1] == w.shape[2]
    seg = lambda n: w[:, :, offs[n]:offs[n + 1]]
    zero = lambda n: jnp.zeros((depth, d, n), w.dtype)
    ik = seg(9)
    w_head = w[:, :, :offs[3]]
    w_mid = w[:, :, offs[5]:offs[9]]
    w_small = jnp.concatenate([seg(4), seg(3), zero(LANES - GLA_RANK), seg(10), zero(LANES - IDX_HEADS),
                               ik, zero(2 * IDX_DH), ik], axis=2)
    return w_head.astype(BF16), w_mid.astype(BF16), w_small.astype(BF16)


def kernel(x, mem, norm_gains, w_in, w_gate_up, b_gate, gla_head_gain, rel_table,
           w_out, w_xq, w_xkv, w_xo, w_ffn_in, w_ffn_out):
    b, l, d = x.shape
    nm = mem.shape[1]
    depth = w_in.shape[0]
    d_ff = w_ffn_out.shape[1]
    m = b * l
    assert d // 2 // GLA_HEADS == 2 * LANES and d // 2 // DSA_HEADS == LANES

    t = TILES
    corr = _bias_correction_tiles(rel_table, t["dsa_q"])
    tri = _chunk_sum_matrices(t["gla_t"])
    w2p = jnp.pad(w_gate_up, ((0, 0), (0, LANES - GLA_RANK), (0, 0))).astype(BF16)

    w_out, w_xq, w_xkv, w_xo, w_ffn_in, w_ffn_out = (
        w.astype(BF16) for w in (w_out, w_xq, w_xkv, w_xo, w_ffn_in, w_ffn_out))
    w_head, w_mid, w_small = _split_in_proj(w_in)

    xf = x.reshape(m, d)
    memf = mem.reshape(b * nm, d)
    for li in range(depth):
        g = norm_gains[li]
        pa, pb = in_proj(xf, g[0], w_head, w_mid, w_small, li, *t["in_proj"])
        pa, pb = pa.reshape(b, l, -1), pb.reshape(b, l, -1)
        o_gla = gla(pa, pb, w2p[li], b_gate[li].reshape(1, -1),
                    gla_head_gain[li].reshape(1, -1), tri, t["gla_t"], heads_per_step=GLA_HEADS)
        o_dsa = dsa(pa, pb, corr, t["dsa_q"])
        xf = mm_norm_res([o_gla.reshape(m, -1), o_dsa.reshape(m, -1)], w_out, li, g[1], xf,
                         bm=t["out_proj_m"])
        kv = norm_matmul(memf, g[6], w_xkv, li, BF16, bm=min(t["mem_kv"][0], b * nm), bn=t["mem_kv"][1])
        xf = xattn(xf.reshape(b, l, d), kv.reshape(b, nm, 2 * d), g[2], w_xq, g[3], w_xo, li,
                   tq=t["xattn_q"]).reshape(m, d)
        act = ffn_in(xf, g[4], w_ffn_in, li, d_ff, *t["ffn_in"])
        xf = mm_norm_res([act], w_ffn_out, li, g[5], xf, bm=t["ffn_out_m"])
    return xf.reshape(b, l, d)
```

```python
import functools
import math

import jax
import jax.numpy as jnp
from jax import lax
from jax.experimental import pallas as pl
from jax.experimental.pallas import tpu as pltpu

F32 = jnp.float32
BF16 = jnp.bfloat16

EPS = 1e-6
GLA_HEADS = 4
GLA_RANK = 16
GLA_TAU = 16.0
GLA_CHUNK = 64
DSA_HEADS = 8
IDX_HEADS = 16
IDX_DH = 64
TOPK_MAX = 256
REL_BUCKETS = 32
REL_MAX_DIST = 128
XA_HEADS = 4

LANES = 128
SUBLANES = 8
VMEM_LIMIT_BYTES = 60000 * 1024

PACK16 = 16
ROW_CHUNK = 128
NORM_PARTS = 4

NEG_BIG = -1e30
HALF16 = 2 ** 15
LOG2E = 1.4426950408889634

TILES = dict(
    in_proj=(1024, 1024, 768),
    gla_t=512,
    dsa_q=256,
    out_proj_m=1024,
    mem_kv=(512, 1024),
    xattn_q=512,
    ffn_in=(1024, 512),
    ffn_out_m=512,
)

NT_DIMS = (((1,), (1,)), ((), ()))
TN_DIMS = (((0,), (0,)), ((), ()))


def _params(*sem):
    return pltpu.CompilerParams(dimension_semantics=sem, vmem_limit_bytes=VMEM_LIMIT_BYTES)


def _resident(shape, index_map):
    return pl.BlockSpec(shape, index_map, pipeline_mode=pl.Buffered(1))


def _rms(x, g):
    return x * lax.rsqrt(jnp.mean(x * x, axis=-1, keepdims=True) + EPS) * g


def _row_parts(n_rows, parts):
    size = n_rows // parts
    return [slice(p * size, (p + 1) * size) for p in range(parts)]


def _rms_to(x_ref, g_ref, h_ref):
    g = g_ref[...]

    def body(c, _):
        rows = pl.ds(pl.multiple_of(c * ROW_CHUNK, ROW_CHUNK), ROW_CHUNK)
        h_ref[rows, :] = _rms(x_ref[rows, :], g).astype(h_ref.dtype)
        return 0

    lax.fori_loop(0, x_ref.shape[0] // ROW_CHUNK, body, 0)


def _norm_matmul_kernel(x_ref, g_ref, w_ref, o_ref, h_ref):
    @pl.when(pl.program_id(1) == 0)
    def _():
        _rms_to(x_ref, g_ref, h_ref)

    o_ref[...] = jnp.dot(h_ref[...], w_ref[...], preferred_element_type=F32).astype(o_ref.dtype)


def norm_matmul(x, g, w, layer, out_dtype, bm, bn):
    m, k = x.shape
    n = w.shape[2]
    assert m % bm == 0 and n % bn == 0
    return pl.pallas_call(
        _norm_matmul_kernel,
        grid=(m // bm, n // bn),
        in_specs=[
            pl.BlockSpec((bm, k), lambda i, j: (i, 0)),
            _resident((1, k), lambda i, j: (0, 0)),
            pl.BlockSpec((None, k, bn), lambda i, j: (layer, 0, j)),
        ],
        out_specs=pl.BlockSpec((bm, bn), lambda i, j: (i, j)),
        out_shape=jax.ShapeDtypeStruct((m, n), out_dtype),
        scratch_shapes=[pltpu.VMEM((bm, k), BF16)],
        compiler_params=_params("parallel", "arbitrary"),
        name="norm_matmul",
    )(x, g.reshape(1, k), w)


def _in_proj_slot(i, j, n_slots):
    return jnp.where(i % 2 == 0, j, n_slots - 1 - j)


def _in_proj_kernel(x_ref, g_ref, wh_ref, wm_ref, ws_ref, oa_ref, ob_ref, h_ref, *, nh, ns, nm):
    s = _in_proj_slot(pl.program_id(0), pl.program_id(1), nh + ns + nm)

    first = pl.program_id(1) == 0

    def project(w_ref, o_ref):
        o_ref[...] = jnp.dot(h_ref[...], w_ref[...], preferred_element_type=F32).astype(o_ref.dtype)

    def project_first(w_ref, o_ref):
        g = g_ref[...]
        for rows in _row_parts(x_ref.shape[0], NORM_PARTS):
            h = _rms(x_ref[rows, :], g).astype(h_ref.dtype)
            h_ref[rows, :] = h
            o_ref[rows, :] = jnp.dot(h, w_ref[...], preferred_element_type=F32).astype(o_ref.dtype)

    pl.when(first & (s < nh))(lambda: project_first(wh_ref, oa_ref))
    pl.when(first & (s >= nh + ns))(lambda: project_first(wm_ref, oa_ref))
    pl.when(jnp.logical_not(first) & (s < nh))(lambda: project(wh_ref, oa_ref))
    pl.when((s >= nh) & (s < nh + ns))(lambda: project(ws_ref, ob_ref))
    pl.when(jnp.logical_not(first) & (s >= nh + ns))(lambda: project(wm_ref, oa_ref))


def in_proj(x, g, w_head, w_mid, w_small, layer, bm, bn, bs):
    m, k = x.shape
    n_head, n_mid, n_small = w_head.shape[2], w_mid.shape[2], w_small.shape[2]
    assert m % bm == 0 and n_head % bn == 0 and n_mid % bn == 0 and n_small % bs == 0
    nh, nm, ns = n_head // bn, n_mid // bn, n_small // bs
    n_slots = nh + ns + nm

    def slot(i, j):
        return _in_proj_slot(i, j, n_slots)

    def pa_block(i, j):
        s = slot(i, j)
        hold = jnp.where(i % 2 == 0, nh - 1, nh)
        return jnp.where(s < nh, s, jnp.where(s >= nh + ns, s - ns, hold))

    return pl.pallas_call(
        functools.partial(_in_proj_kernel, nh=nh, ns=ns, nm=nm),
        grid=(m // bm, n_slots),
        in_specs=[
            pl.BlockSpec((bm, k), lambda i, j: (i, 0)),
            _resident((1, k), lambda i, j: (0, 0)),
            pl.BlockSpec((None, k, bn), lambda i, j: (layer, 0, jnp.clip(slot(i, j), 0, nh - 1))),
            pl.BlockSpec((None, k, bn), lambda i, j: (layer, 0, jnp.clip(slot(i, j) - nh - ns, 0, nm - 1))),
            pl.BlockSpec((None, k, bs), lambda i, j: (layer, 0, jnp.clip(slot(i, j) - nh, 0, ns - 1))),
        ],
        out_specs=[
            pl.BlockSpec((bm, bn), lambda i, j: (i, pa_block(i, j))),
            pl.BlockSpec((bm, bs), lambda i, j: (i, jnp.clip(slot(i, j) - nh, 0, ns - 1))),
        ],
        out_shape=[
            jax.ShapeDtypeStruct((m, n_head + n_mid), BF16),
            jax.ShapeDtypeStruct((m, n_small), F32),
        ],
        scratch_shapes=[pltpu.VMEM((bm, k), BF16)],
        compiler_params=_params("parallel", "arbitrary"),
        name="in_proj",
    )(x, g.reshape(1, k), w_head, w_mid, w_small)


def _ffn_in_kernel(x_ref, g_ref, wg_ref, wu_ref, o_ref, h_ref):
    def gated(h):
        a = jnp.dot(h, wg_ref[...], preferred_element_type=F32)
        b = jnp.dot(h, wu_ref[...], preferred_element_type=F32)
        return (a * (1.0 / (1.0 + jnp.exp(-a))) * b).astype(o_ref.dtype)

    @pl.when(pl.program_id(1) == 0)
    def _():
        g = g_ref[...]
        for rows in _row_parts(x_ref.shape[0], NORM_PARTS):
            h = _rms(x_ref[rows, :], g).astype(h_ref.dtype)
            h_ref[rows, :] = h
            o_ref[rows, :] = gated(h)

    @pl.when(pl.program_id(1) > 0)
    def _():
        o_ref[...] = gated(h_ref[...])


def ffn_in(x, g, w, layer, d_ff, bm, bn):
    m, k = x.shape
    assert m % bm == 0 and d_ff % bn == 0
    nj = d_ff // bn
    return pl.pallas_call(
        _ffn_in_kernel,
        grid=(m // bm, nj),
        in_specs=[
            pl.BlockSpec((bm, k), lambda i, j: (i, 0)),
            _resident((1, k), lambda i, j: (0, 0)),
            pl.BlockSpec((None, k, bn), lambda i, j: (layer, 0, j)),
            pl.BlockSpec((None, k, bn), lambda i, j: (layer, 0, j + nj)),
        ],
        out_specs=pl.BlockSpec((bm, bn), lambda i, j: (i, j)),
        out_shape=jax.ShapeDtypeStruct((m, d_ff), BF16),
        scratch_shapes=[pltpu.VMEM((bm, k), BF16)],
        compiler_params=_params("parallel", "arbitrary"),
        name="ffn_in",
    )(x, g.reshape(1, k), w, w)


def _mm_norm_res_kernel(*refs, n_lhs, nk):
    lhs = refs[:n_lhs]
    w_ref, g_ref, res_ref, o_ref = refs[n_lhs:n_lhs + 4]
    if nk == 1:
        g = g_ref[...]
        half = o_ref.shape[0] // 2
        for rows in (slice(0, half), slice(half, 2 * half)):
            y = None
            off = 0
            for a in lhs:
                wd = a.shape[1]
                part = jnp.dot(a[rows, :], w_ref[off:off + wd, :], preferred_element_type=F32)
                y = part if y is None else y + part
                off += wd
            o_ref[rows, :] = res_ref[rows, :] + _rms(y, g)
    else:
        k = pl.program_id(1)
        part = jnp.dot(lhs[0][...], w_ref[...], preferred_element_type=F32)

        @pl.when(k == 0)
        def _():
            o_ref[...] = part

        @pl.when(k > 0)
        def _():
            o_ref[...] += part

        @pl.when(k == nk - 1)
        def _():
            g = g_ref[...]

            def finish(c, _):
                rows = pl.ds(pl.multiple_of(c * ROW_CHUNK, ROW_CHUNK), ROW_CHUNK)
                o_ref[rows, :] = res_ref[rows, :] + _rms(o_ref[rows, :], g)
                return 0

            lax.fori_loop(0, o_ref.shape[0] // ROW_CHUNK, finish, 0)


def mm_norm_res(lhs_list, w, layer, g, res, bm, nk=1):
    m, n = res.shape
    kk = w.shape[1]
    assert m % bm == 0 and kk % nk == 0
    bk = kk // nk
    n_lhs = len(lhs_list)
    if nk == 1:
        lhs_specs = [pl.BlockSpec((bm, a.shape[1]), lambda i, k: (i, 0)) for a in lhs_list]
        w_spec = _resident((None, kk, n), lambda i, k: (layer, 0, 0))
    else:
        assert n_lhs == 1
        kblock = lambda i, k: jnp.where(i % 2 == 0, k, nk - 1 - k)
        lhs_specs = [pl.BlockSpec((bm, bk), lambda i, k: (i, kblock(i, k)))]
        w_spec = pl.BlockSpec((None, bk, n), lambda i, k: (layer, kblock(i, k), 0))
    return pl.pallas_call(
        functools.partial(_mm_norm_res_kernel, n_lhs=n_lhs, nk=nk),
        grid=(m // bm, nk),
        in_specs=lhs_specs + [
            w_spec,
            _resident((1, n), lambda i, k: (0, 0)),
            pl.BlockSpec((bm, n), lambda i, k: (i, 0)),
        ],
        out_specs=pl.BlockSpec((bm, n), lambda i, k: (i, 0)),
        out_shape=jax.ShapeDtypeStruct((m, n), F32),
        compiler_params=_params("parallel", "arbitrary"),
        name="mm_norm_res",
    )(*lhs_list, w, g.reshape(1, n), res)


def _gla_kernel(q_ref, k_ref, v_ref, z_ref, r_ref, w2_ref, bg_ref, gh_ref, tri_ref,
                o_ref, st_ref, *, dk, dv):
    tl = q_ref.shape[0]
    heads = q_ref.shape[1] // dk
    n_chunks = tl // GLA_CHUNK

    @pl.when(pl.program_id(2) == 0)
    def _():
        st_ref[...] = jnp.zeros_like(st_ref)

    pre = jnp.dot(z_ref[...].astype(BF16), w2_ref[...], preferred_element_type=F32) + bg_ref[...]
    logg = (jnp.minimum(pre, 0.0) - jnp.log1p(jnp.exp(-jnp.abs(pre)))) * (1.0 / GLA_TAU)
    hi = logg.astype(BF16)
    lo = (logg - hi.astype(F32)).astype(BF16)
    incl = tri_ref[...]
    cs = jnp.dot(incl, jnp.concatenate([hi, lo], axis=1), preferred_element_type=F32)
    bcum_all = cs[:, :heads * dk] + cs[:, heads * dk:]
    gh = gh_ref[...]

    for hh in range(heads):
        ks = slice(hh * dk, (hh + 1) * dk)
        vs = slice(hh * dv, (hh + 1) * dv)
        bcum = bcum_all[:, ks]
        b_last = [bcum[(c + 1) * GLA_CHUNK - 1:(c + 1) * GLA_CHUNK, :] for c in range(n_chunks)]
        suf = jnp.concatenate([jnp.broadcast_to(bl, (GLA_CHUNK, dk)) for bl in b_last], axis=0) - bcum

        qf = q_ref[:, ks].astype(F32) * dk ** -0.5
        kf = k_ref[:, ks].astype(F32)
        q_dec = (qf * jnp.exp(bcum)).astype(BF16)
        k_intra = (kf * jnp.exp(-bcum)).astype(BF16)
        k_state = (kf * jnp.exp(suf)).astype(BF16)
        v = v_ref[:, vs]

        a = lax.dot_general(q_dec, k_intra, NT_DIMS, preferred_element_type=F32)
        a = jnp.where(incl > 0, a, 0.0).astype(BF16)
        o_intra = jnp.dot(a, v, preferred_element_type=F32)

        st = st_ref[hh]
        outs = []
        for c in range(n_chunks):
            rows = slice(c * GLA_CHUNK, (c + 1) * GLA_CHUNK)
            o_inter = lax.dot_general(q_dec[rows], st.astype(BF16), NT_DIMS,
                                      preferred_element_type=F32)
            outs.append(o_intra[rows] + o_inter)
            decay = jnp.exp(b_last[c])
            upd = lax.dot_general(v[rows], k_state[rows], TN_DIMS, preferred_element_type=F32)
            st = decay * st + upd
        st_ref[hh] = st
        o = _rms(jnp.concatenate(outs, axis=0), gh)
        r = r_ref[:, vs]
        o_ref[:, vs] = (o * (r * (1.0 / (1.0 + jnp.exp(-r))))).astype(o_ref.dtype)


def gla(pa, pb, w2p, bgate, ghead, tri, tl, heads_per_step):
    b, l, _ = pa.shape
    dk = LANES
    dv = 2 * dk
    hp = heads_per_step
    assert GLA_HEADS % hp == 0
    groups = GLA_HEADS // hp
    wk, wv = hp * dk, hp * dv
    kb0 = GLA_HEADS * dk // wk
    vb0 = 2 * GLA_HEADS * dk // wv
    zb = GLA_HEADS * dv // LANES
    return pl.pallas_call(
        functools.partial(_gla_kernel, dk=dk, dv=dv),
        grid=(b, groups, l // tl),
        in_specs=[
            pl.BlockSpec((None, tl, wk), lambda bi, h, t: (bi, t, h)),
            pl.BlockSpec((None, tl, wk), lambda bi, h, t: (bi, t, kb0 + h)),
            pl.BlockSpec((None, tl, wv), lambda bi, h, t: (bi, t, vb0 + h)),
            pl.BlockSpec((None, tl, LANES), lambda bi, h, t: (bi, t, zb)),
            pl.BlockSpec((None, tl, wv), lambda bi, h, t: (bi, t, h)),
            pl.BlockSpec((LANES, wk), lambda bi, h, t: (0, h)),
            pl.BlockSpec((1, wk), lambda bi, h, t: (0, h)),
            _resident((1, dv), lambda bi, h, t: (0, 0)),
            _resident((tl, tl), lambda bi, h, t: (0, 0)),
        ],
        out_specs=pl.BlockSpec((None, tl, wv), lambda bi, h, t: (bi, t, h)),
        out_shape=jax.ShapeDtypeStruct((b, l, GLA_HEADS * dv), BF16),
        scratch_shapes=[pltpu.VMEM((hp, dv, dk), F32)],
        compiler_params=_params("parallel", "parallel", "arbitrary"),
        name="gla",
    )(pa, pa, pa, pb, pb, w2p, bgate, ghead, tri)


def _for_each_block_paired(n, body):
    def pair(p, _):
        body(2 * p)
        body(2 * p + 1)
        return 0

    lax.fori_loop(0, n >> 1, pair, 0)

    @pl.when(n & 1 == 1)
    def _():
        body(n - 1)


def _dsa_kernel(q_ref, iq_ref, iw_ref, k_ref, v_ref, ik_ref, corr_ref, o_ref,
                key_ref, hi_ref, lo_ref, madd_ref, wt_ref, m_ref, l_ref, acc_ref, vt_ref, s_ref,
                *, topk):
    tq = q_ref.shape[0]
    tk = tq
    dh = LANES
    i = pl.program_id(1)
    nkb = i + 1
    scale = dh ** -0.5

    wt_ref[...] = (iw_ref[...] * (IDX_HEADS ** -0.5 * IDX_DH ** -0.5)).T
    qpos = lax.broadcasted_iota(jnp.int32, (tk, tq), 1) + i * tq
    kpos0 = lax.broadcasted_iota(jnp.int32, (tk, tq), 0)

    def score_block(kb):
        ik = ik_ref[pl.ds(pl.multiple_of(kb * tk, tk), tk), :].astype(BF16)
        ik_even, ik_odd = ik[:, :LANES], ik[:, LANES:]
        acc = jnp.zeros((tk, tq), F32)
        for p in range(IDX_HEADS // 2):
            qp = iq_ref[:, p * LANES:(p + 1) * LANES]
            for par, ikx in ((0, ik_even), (1, ik_odd)):
                h = 2 * p + par
                d = lax.dot_general(ikx, qp, NT_DIMS, preferred_element_type=F32)
                acc = acc + jnp.maximum(d, 0.0) * wt_ref[h:h + 1, :]
        sc = jnp.where(kpos0 + kb * tk <= qpos, acc, -jnp.inf)
        bits = lax.bitcast_convert_type(sc, jnp.int32)
        key = bits ^ ((bits >> 31) & 0x7FFFFFFF)
        key_ref[kb] = key
        hi_ref[kb] = (key >> 16).astype(jnp.int16)
        lo_ref[kb] = ((key & 0xFFFF) - HALF16).astype(jnp.int16)

    _for_each_block_paired(nkb, score_block)

    n_acc = 4
    slabs = tk // PACK16
    one16, zero16 = jnp.int16(1), jnp.int16(0)

    def rows16(x):
        return jnp.broadcast_to(x, (PACK16, tq)).astype(jnp.int16)

    def count16(src_ref, cand):
        cand16 = rows16(cand)

        def block(kb, cnts):
            cnts = list(cnts)
            for r in range(slabs):
                hit = src_ref[kb, r * PACK16:(r + 1) * PACK16, :] >= cand16
                cnts[r % n_acc] = cnts[r % n_acc] + jnp.where(hit, one16, zero16)
            return tuple(cnts)

        def pair(p, cnts):
            return block(2 * p + 1, block(2 * p, cnts))

        cnts = lax.fori_loop(0, nkb >> 1, pair, (jnp.zeros((PACK16, tq), jnp.int16),) * n_acc)
        cnts = lax.cond(nkb & 1 == 1, lambda c: block(nkb - 1, c), lambda c: c, cnts)
        cnt = ((cnts[0] + cnts[1]) + (cnts[2] + cnts[3])).astype(jnp.int32)
        return jnp.sum(cnt, axis=0, keepdims=True)

    def bisect16(src_ref, kth):
        def step(s, carry):
            t, above = carry
            cand = t + lax.shift_left(jnp.int32(1), 15 - s)
            total = count16(src_ref, cand)
            ok = total >= kth
            return jnp.where(ok, cand, t), jnp.where(ok, above, total)

        return lax.fori_loop(0, 16, step, (jnp.full((1, tq), -HALF16, jnp.int32),
                                           jnp.zeros((1, tq), jnp.int32)))

    t_hi, above = bisect16(hi_ref, topk)
    t_hi16 = rows16(t_hi)

    def keep_threshold_bucket(kb, _):
        for r in range(slabs):
            rows = slice(r * PACK16, (r + 1) * PACK16)
            lo_ref[kb, rows, :] = jnp.where(hi_ref[kb, rows, :] == t_hi16, lo_ref[kb, rows, :],
                                            jnp.int16(-HALF16))
        return 0

    lax.fori_loop(0, nkb, keep_threshold_bucket, 0)
    t_lo, _ = bisect16(lo_ref, topk - above)
    thr = t_hi * (2 * HALF16) + (t_lo + HALF16)

    def mask_block(kb, _):
        vis = kpos0 + kb * tk <= qpos
        madd_ref[kb] = jnp.where(vis, jnp.where(key_ref[kb] >= thr, 0.0, NEG_BIG), NEG_BIG)
        return 0

    lax.fori_loop(0, nkb, mask_block, 0)

    @pl.when(i == 0)
    def _():
        def transpose_values(kb, _):
            vt_ref[kb] = v_ref[pl.ds(pl.multiple_of(kb * tk, tk), tk), :].T
            return 0

        lax.fori_loop(0, vt_ref.shape[0], transpose_values, 0)

    m_ref[...] = jnp.full(m_ref.shape, NEG_BIG, F32)
    l_ref[...] = jnp.zeros(l_ref.shape, F32)
    acc_ref[...] = jnp.zeros(acc_ref.shape, F32)

    ones_rows = jnp.ones((PACK16, tk), BF16)

    def attend(kb, near, buf=0):
        keys = pl.ds(pl.multiple_of(kb * tk, tk), tk)
        madd = madd_ref[kb]
        for h in range(DSA_HEADS):
            hs = slice(h * dh, (h + 1) * dh)
            s_ref[buf, h] = lax.dot_general(k_ref[keys, hs], q_ref[:, hs], NT_DIMS,
                                            preferred_element_type=F32)
        for h in range(DSA_HEADS):
            hs = slice(h * dh, (h + 1) * dh)
            s = s_ref[buf, h] * (scale * LOG2E)
            if near is not None:
                s = s + corr_ref[h, near]
            s = s + madd
            m_prev = m_ref[h, 0:1, :]
            m_next = jnp.maximum(m_prev, jnp.max(s, axis=0, keepdims=True))
            alpha = jnp.exp2(m_prev - m_next)
            p = jnp.exp2(s - m_next)
            pv = jnp.dot(jnp.concatenate([vt_ref[kb, hs, :], ones_rows], axis=0), p.astype(BF16),
                         preferred_element_type=F32)
            l_next = alpha * l_ref[h, 0:1, :] + pv[dh:dh + 1, :]
            acc_ref[h] = alpha * acc_ref[h] + pv[:dh, :]
            m_ref[h] = jnp.broadcast_to(m_next, (SUBLANES, tq))
            l_ref[h] = jnp.broadcast_to(l_next, (SUBLANES, tq))

    def far_pair(p, _):
        attend(2 * p, None, 0)
        attend(2 * p + 1, None, 1)
        return 0

    n_far = jnp.maximum(i - 1, 0)
    lax.fori_loop(0, n_far >> 1, far_pair, 0)

    @pl.when(n_far & 1 == 1)
    def _():
        attend(n_far - 1, None)

    @pl.when(i >= 1)
    def _():
        attend(i - 1, 1, 0)
        attend(i, 0, 1)

    @pl.when(i == 0)
    def _():
        attend(i, 0)

    for h in range(DSA_HEADS):
        o_h = acc_ref[h] / l_ref[h, 0:1, :]
        o_ref[:, h * dh:(h + 1) * dh] = o_h.T.astype(o_ref.dtype)


def dsa(pa, pb, corr, tq):
    b, l, _ = pa.shape
    dh = LANES
    hd = DSA_HEADS * dh
    qb = (2 * GLA_HEADS * LANES + GLA_HEADS * 2 * LANES) // hd
    iwb = GLA_HEADS * 2 * LANES // LANES + 1
    ikb = (iwb + 1) * LANES // (2 * LANES)
    topk = min(TOPK_MAX, l // 4)
    assert l % tq == 0 and tq >= topk and tq >= REL_MAX_DIST
    nkb = l // tq
    assert nkb * tq // PACK16 < HALF16
    return pl.pallas_call(
        functools.partial(_dsa_kernel, topk=topk),
        grid=(b, l // tq),
        in_specs=[
            pl.BlockSpec((None, tq, hd), lambda bi, i: (bi, i, qb)),
            pl.BlockSpec((None, tq, IDX_HEADS * IDX_DH), lambda bi, i: (bi, i, qb + 3)),
            pl.BlockSpec((None, tq, LANES), lambda bi, i: (bi, i, iwb)),
            _resident((None, l, hd), lambda bi, i: (bi, 0, qb + 1)),
            _resident((None, l, hd), lambda bi, i: (bi, 0, qb + 2)),
            _resident((None, l, 2 * LANES), lambda bi, i: (bi, 0, ikb)),
            _resident((DSA_HEADS, 2, tq, tq), lambda bi, i: (0, 0, 0, 0)),
        ],
        out_specs=pl.BlockSpec((None, tq, hd), lambda bi, i: (bi, i, 0)),
        out_shape=jax.ShapeDtypeStruct((b, l, hd), BF16),
        scratch_shapes=[
            pltpu.VMEM((nkb, tq, tq), jnp.int32),
            pltpu.VMEM((nkb, tq, tq), jnp.int16),
            pltpu.VMEM((nkb, tq, tq), jnp.int16),
            pltpu.VMEM((nkb, tq, tq), F32),
            pltpu.VMEM((LANES, tq), F32),
            pltpu.VMEM((DSA_HEADS, SUBLANES, tq), F32),
            pltpu.VMEM((DSA_HEADS, SUBLANES, tq), F32),
            pltpu.VMEM((DSA_HEADS, dh, tq), F32),
            pltpu.VMEM((nkb, hd, tq), BF16),
            pltpu.VMEM((2, DSA_HEADS, tq, tq), F32),
        ],
        compiler_params=_params("parallel", "arbitrary"),
        name="dsa",
    )(pa, pa, pb, pa, pa, pb, corr)


def _xattn_kernel(x_ref, gq_ref, wq_ref, kv_ref, wo_ref, go_ref, o_ref, q_ref, a_ref):
    d = x_ref.shape[1]
    dh = d // XA_HEADS
    scale = dh ** -0.5
    halves = _row_parts(x_ref.shape[0], 2)
    gq = gq_ref[...]
    for rows in halves:
        h = _rms(x_ref[rows, :], gq).astype(BF16)
        q_ref[rows, :] = jnp.dot(h, wq_ref[...], preferred_element_type=F32).astype(q_ref.dtype)
    heads = [slice(h * dh, (h + 1) * dh) for h in range(XA_HEADS)]
    scores = [lax.dot_general(q_ref[:, hs], kv_ref[:, hs], NT_DIMS, preferred_element_type=F32)
              for hs in heads]
    for h, hs in enumerate(heads):
        s = scores[h] * scale
        e = jnp.exp(s - jnp.max(s, axis=-1, keepdims=True))
        o = jnp.dot(e.astype(BF16), kv_ref[:, d + h * dh:d + (h + 1) * dh], preferred_element_type=F32)
        a_ref[:, hs] = (o / jnp.sum(e, axis=-1, keepdims=True)).astype(a_ref.dtype)
    go = go_ref[...]
    for rows in halves:
        y = jnp.dot(a_ref[rows, :], wo_ref[...], preferred_element_type=F32)
        o_ref[rows, :] = x_ref[rows, :] + _rms(y, go)


def xattn(x, kv, gq, wq, go, wo, layer, tq):
    b, l, d = x.shape
    nm = kv.shape[1]
    vec = lambda: _resident((1, d), lambda bi, i: (0, 0))
    mat = lambda: _resident((None, d, d), lambda bi, i: (layer, 0, 0))
    return pl.pallas_call(
        _xattn_kernel,
        grid=(b, l // tq),
        in_specs=[
            pl.BlockSpec((None, tq, d), lambda bi, i: (bi, i, 0)),
            vec(), mat(),
            pl.BlockSpec((None, nm, 2 * d), lambda bi, i: (bi, 0, 0)),
            mat(), vec(),
        ],
        out_specs=pl.BlockSpec((None, tq, d), lambda bi, i: (bi, i, 0)),
        out_shape=jax.ShapeDtypeStruct((b, l, d), F32),
        scratch_shapes=[pltpu.VMEM((tq, d), BF16)] * 2,
        compiler_params=_params("parallel", "parallel"),
        name="xattn",
    )(x, gq.reshape(1, d), wq, kv, wo, go.reshape(1, d))


def _rel_bucket(dist):
    n = jnp.maximum(dist, 0)
    max_exact = REL_BUCKETS // 2
    nf = jnp.maximum(n, 1).astype(F32)
    large = max_exact + (jnp.log(nf / max_exact) / math.log(REL_MAX_DIST / max_exact)
                         * (REL_BUCKETS - max_exact)).astype(jnp.int32)
    large = jnp.minimum(large, REL_BUCKETS - 1)
    return jnp.where(n < max_exact, n, large)


def _bias_correction_tiles(rel_table, tq):
    t = jnp.arange(tq)
    d0 = t[:, None] - t[None, :]
    buckets = jnp.stack([_rel_bucket(d0), _rel_bucket(d0 + tq)], axis=0)
    onehot = (buckets[..., None] == jnp.arange(REL_BUCKETS)).astype(F32)
    far = rel_table[_rel_bucket(jnp.int32(2 * tq))]
    return jnp.einsum("ntsk,kh->hnst", onehot, (rel_table - far) * LOG2E,
                      precision=lax.Precision.HIGHEST)


def _chunk_sum_matrices(tl):
    r = jnp.arange(tl)
    same = (r[:, None] // GLA_CHUNK) == (r[None, :] // GLA_CHUNK)
    return (same & (r[None, :] <= r[:, None])).astype(BF16)


def _split_in_proj(w):
    depth, d, _ = w.shape
    half = d // 2
    sizes = [GLA_HEADS * LANES, GLA_HEADS * LANES, half, GLA_RANK, half,
             half, half, half, IDX_HEADS * IDX_DH, IDX_DH, IDX_HEADS]
    offs = [0]
    for s in sizes:
        offs.append(offs[-1] + s)
    assert offs[-1] == w.shape[2]
    seg = lambda n: w[:, :, offs[n]:offs[n + 1]]
    zero = lambda n: jnp.zeros((depth, d, n), w.dtype)
    ik = seg(9)
    w_head = w[:, :, :offs[3]]
    w_mid = w[:, :, offs[5]:offs[9]]
    w_small = jnp.concatenate([seg(4), seg(3), zero(LANES - GLA_RANK), seg(10), zero(LANES - IDX_HEADS),
                               ik, zero(2 * IDX_DH), ik], axis=2)
    return w_head.astype(BF16), w_mid.astype(BF16), w_small.astype(BF16)


def kernel(x, mem, norm_gains, w_in, w_gate_up, b_gate, gla_head_gain, rel_table,
           w_out, w_xq, w_xkv, w_xo, w_ffn_in, w_ffn_out):
    b, l, d = x.shape
    nm = mem.shape[1]
    depth = w_in.shape[0]
    d_ff = w_ffn_out.shape[1]
    m = b * l
    assert d // 2 // GLA_HEADS == 2 * LANES and d // 2 // DSA_HEADS == LANES

    t = TILES
    corr = _bias_correction_tiles(rel_table, t["dsa_q"])
    tri = _chunk_sum_matrices(t["gla_t"])
    w2p = jnp.pad(w_gate_up, ((0, 0), (0, LANES - GLA_RANK), (0, 0))).astype(BF16)

    w_out, w_xq, w_xkv, w_xo, w_ffn_in, w_ffn_out = (
        w.astype(BF16) for w in (w_out, w_xq, w_xkv, w_xo, w_ffn_in, w_ffn_out))
    w_head, w_mid, w_small = _split_in_proj(w_in)

    xf = x.reshape(m, d)
    memf = mem.reshape(b * nm, d)
    for li in range(depth):
        g = norm_gains[li]
        pa, pb = in_proj(xf, g[0], w_head, w_mid, w_small, li, *t["in_proj"])
        pa, pb = pa.reshape(b, l, -1), pb.reshape(b, l, -1)
        o_gla = gla(pa, pb, w2p[li], b_gate[li].reshape(1, -1),
                    gla_head_gain[li].reshape(1, -1), tri, t["gla_t"], heads_per_step=GLA_HEADS)
        o_dsa = dsa(pa, pb, corr, t["dsa_q"])
        xf = mm_norm_res([o_gla.reshape(m, -1), o_dsa.reshape(m, -1)], w_out, li, g[1], xf,
                         bm=t["out_proj_m"])
        kv = norm_matmul(memf, g[6], w_xkv, li, BF16, bm=min(t["mem_kv"][0], b * nm), bn=t["mem_kv"][1])
        xf = xattn(xf.reshape(b, l, d), kv.reshape(b, nm, 2 * d), g[2], w_xq, g[3], w_xo, li,
                   tq=t["xattn_q"]).reshape(m, d)
        act = ffn_in(xf, g[4], w_ffn_in, li, d_ff, *t["ffn_in"])
        xf = mm_norm_res([act], w_ffn_out, li, g[5], xf, bm=t["ffn_out_m"])
    return xf.reshape(b, l, d)
```

```python
import functools
import math

import jax
import jax.numpy as jnp
from jax import lax
from jax.experimental import pallas as pl
from jax.experimental.pallas import tpu as pltpu

F32 = jnp.float32
BF16 = jnp.bfloat16

EPS = 1e-6
GLA_HEADS = 4
GLA_RANK = 16
GLA_TAU = 16.0
GLA_CHUNK = 64
DSA_HEADS = 8
IDX_HEADS = 16
IDX_DH = 64
TOPK_MAX = 256
REL_BUCKETS = 32
REL_MAX_DIST = 128
XA_HEADS = 4

LANES = 128
SUBLANES = 8
VMEM_LIMIT_BYTES = 60000 * 1024

PACK16 = 16
ROW_CHUNK = 128
NORM_PARTS = 4

NEG_BIG = -1e30
HALF16 = 2 ** 15
LOG2E = 1.4426950408889634
DSA_LOGIT_SCALE = LANES ** -0.5 * LOG2E

TILES = dict(
    in_proj=(1024, 1024, 768),
    gla_t=512,
    dsa_q=256,
    out_proj_m=1024,
    mem_kv=(512, 1024),
    xattn_q=512,
    ffn_in=(1024, 512),
    ffn_out_m=512,
)

NT_DIMS = (((1,), (1,)), ((), ()))
TN_DIMS = (((0,), (0,)), ((), ()))


def _params(*sem):
    return pltpu.CompilerParams(dimension_semantics=sem, vmem_limit_bytes=VMEM_LIMIT_BYTES)


def _resident(shape, index_map):
    return pl.BlockSpec(shape, index_map, pipeline_mode=pl.Buffered(1))


def _rms(x, g):
    return x * lax.rsqrt(jnp.mean(x * x, axis=-1, keepdims=True) + EPS) * g


def _row_parts(n_rows, parts):
    size = n_rows // parts
    return [slice(p * size, (p + 1) * size) for p in range(parts)]


def _rms_to(x_ref, g_ref, h_ref):
    g = g_ref[...]

    def body(c, _):
        rows = pl.ds(pl.multiple_of(c * ROW_CHUNK, ROW_CHUNK), ROW_CHUNK)
        h_ref[rows, :] = _rms(x_ref[rows, :], g).astype(h_ref.dtype)
        return 0

    lax.fori_loop(0, x_ref.shape[0] // ROW_CHUNK, body, 0)


def _norm_matmul_kernel(x_ref, g_ref, w_ref, o_ref, h_ref):
    @pl.when(pl.program_id(1) == 0)
    def _():
        _rms_to(x_ref, g_ref, h_ref)

    o_ref[...] = jnp.dot(h_ref[...], w_ref[...], preferred_element_type=F32).astype(o_ref.dtype)


def norm_matmul(x, g, w, layer, out_dtype, bm, bn):
    m, k = x.shape
    n = w.shape[2]
    assert m % bm == 0 and n % bn == 0
    return pl.pallas_call(
        _norm_matmul_kernel,
        grid=(m // bm, n // bn),
        in_specs=[
            pl.BlockSpec((bm, k), lambda i, j: (i, 0)),
            _resident((1, k), lambda i, j: (0, 0)),
            pl.BlockSpec((None, k, bn), lambda i, j: (layer, 0, j)),
        ],
        out_specs=pl.BlockSpec((bm, bn), lambda i, j: (i, j)),
        out_shape=jax.ShapeDtypeStruct((m, n), out_dtype),
        scratch_shapes=[pltpu.VMEM((bm, k), BF16)],
        compiler_params=_params("parallel", "arbitrary"),
        name="norm_matmul",
    )(x, g.reshape(1, k), w)


def _in_proj_slot(i, j, n_slots):
    return jnp.where(i % 2 == 0, j, n_slots - 1 - j)


def _in_proj_kernel(x_ref, g_ref, wh_ref, wm_ref, ws_ref, oa_ref, ob_ref, h_ref, *, nh, ns, nm,
                    mid0_scale):
    s = _in_proj_slot(pl.program_id(0), pl.program_id(1), nh + ns + nm)

    first = pl.program_id(1) == 0

    def project(w_ref, o_ref, scale=None):
        y = jnp.dot(h_ref[...], w_ref[...], preferred_element_type=F32)
        o_ref[...] = (y if scale is None else y * scale).astype(o_ref.dtype)

    def project_first(w_ref, o_ref):
        g = g_ref[...]
        for rows in _row_parts(x_ref.shape[0], NORM_PARTS):
            h = _rms(x_ref[rows, :], g).astype(h_ref.dtype)
            h_ref[rows, :] = h
            o_ref[rows, :] = jnp.dot(h, w_ref[...], preferred_element_type=F32).astype(o_ref.dtype)

    pl.when(first & (s < nh))(lambda: project_first(wh_ref, oa_ref))
    pl.when(first & (s >= nh + ns))(lambda: project_first(wm_ref, oa_ref))
    later = jnp.logical_not(first)
    pl.when(later & (s < nh))(lambda: project(wh_ref, oa_ref))
    pl.when((s >= nh) & (s < nh + ns))(lambda: project(ws_ref, ob_ref))
    pl.when(later & (s == nh + ns))(lambda: project(wm_ref, oa_ref, mid0_scale))
    pl.when(later & (s > nh + ns))(lambda: project(wm_ref, oa_ref))


def in_proj(x, g, w_head, w_mid, w_small, layer, mid0_scale, bm, bn, bs):
    m, k = x.shape
    n_head, n_mid, n_small = w_head.shape[2], w_mid.shape[2], w_small.shape[2]
    assert m % bm == 0 and n_head % bn == 0 and n_mid % bn == 0 and n_small % bs == 0
    nh, nm, ns = n_head // bn, n_mid // bn, n_small // bs
    n_slots = nh + ns + nm
    assert nm > 1

    def slot(i, j):
        return _in_proj_slot(i, j, n_slots)

    def pa_block(i, j):
        s = slot(i, j)
        hold = jnp.where(i % 2 == 0, nh - 1, nh)
        return jnp.where(s < nh, s, jnp.where(s >= nh + ns, s - ns, hold))

    return pl.pallas_call(
        functools.partial(_in_proj_kernel, nh=nh, ns=ns, nm=nm, mid0_scale=mid0_scale),
        grid=(m // bm, n_slots),
        in_specs=[
            pl.BlockSpec((bm, k), lambda i, j: (i, 0)),
            _resident((1, k), lambda i, j: (0, 0)),
            pl.BlockSpec((None, k, bn), lambda i, j: (layer, 0, jnp.clip(slot(i, j), 0, nh - 1))),
            pl.BlockSpec((None, k, bn), lambda i, j: (layer, 0, jnp.clip(slot(i, j) - nh - ns, 0, nm - 1))),
            pl.BlockSpec((None, k, bs), lambda i, j: (layer, 0, jnp.clip(slot(i, j) - nh, 0, ns - 1))),
        ],
        out_specs=[
            pl.BlockSpec((bm, bn), lambda i, j: (i, pa_block(i, j))),
            pl.BlockSpec((bm, bs), lambda i, j: (i, jnp.clip(slot(i, j) - nh, 0, ns - 1))),
        ],
        out_shape=[
            jax.ShapeDtypeStruct((m, n_head + n_mid), BF16),
            jax.ShapeDtypeStruct((m, n_small), F32),
        ],
        scratch_shapes=[pltpu.VMEM((bm, k), BF16)],
        compiler_params=_params("parallel", "arbitrary"),
        name="in_proj",
    )(x, g.reshape(1, k), w_head, w_mid, w_small)


def _ffn_in_kernel(x_ref, g_ref, wg_ref, wu_ref, o_ref, h_ref):
    def gated(h):
        a = jnp.dot(h, wg_ref[...], preferred_element_type=F32)
        b = jnp.dot(h, wu_ref[...], preferred_element_type=F32)
        return (a * (1.0 / (1.0 + jnp.exp(-a))) * b).astype(o_ref.dtype)

    @pl.when(pl.program_id(1) == 0)
    def _():
        g = g_ref[...]
        for rows in _row_parts(x_ref.shape[0], NORM_PARTS):
            h = _rms(x_ref[rows, :], g).astype(h_ref.dtype)
            h_ref[rows, :] = h
            o_ref[rows, :] = gated(h)

    @pl.when(pl.program_id(1) > 0)
    def _():
        o_ref[...] = gated(h_ref[...])


def ffn_in(x, g, w, layer, d_ff, bm, bn):
    m, k = x.shape
    assert m % bm == 0 and d_ff % bn == 0
    nj = d_ff // bn
    return pl.pallas_call(
        _ffn_in_kernel,
        grid=(m // bm, nj),
        in_specs=[
            pl.BlockSpec((bm, k), lambda i, j: (i, 0)),
            _resident((1, k), lambda i, j: (0, 0)),
            pl.BlockSpec((None, k, bn), lambda i, j: (layer, 0, j)),
            pl.BlockSpec((None, k, bn), lambda i, j: (layer, 0, j + nj)),
        ],
        out_specs=pl.BlockSpec((bm, bn), lambda i, j: (i, j)),
        out_shape=jax.ShapeDtypeStruct((m, d_ff), BF16),
        scratch_shapes=[pltpu.VMEM((bm, k), BF16)],
        compiler_params=_params("parallel", "arbitrary"),
        name="ffn_in",
    )(x, g.reshape(1, k), w, w)


def _mm_norm_res_kernel(*refs, n_lhs, nk):
    lhs = refs[:n_lhs]
    w_ref, g_ref, res_ref, o_ref = refs[n_lhs:n_lhs + 4]
    if nk == 1:
        g = g_ref[...]
        half = o_ref.shape[0] // 2
        for rows in (slice(0, half), slice(half, 2 * half)):
            y = None
            off = 0
            for a in lhs:
                wd = a.shape[1]
                part = jnp.dot(a[rows, :], w_ref[off:off + wd, :], preferred_element_type=F32)
                y = part if y is None else y + part
                off += wd
            o_ref[rows, :] = res_ref[rows, :] + _rms(y, g)
    else:
        k = pl.program_id(1)
        part = jnp.dot(lhs[0][...], w_ref[...], preferred_element_type=F32)

        @pl.when(k == 0)
        def _():
            o_ref[...] = part

        @pl.when(k > 0)
        def _():
            o_ref[...] += part

        @pl.when(k == nk - 1)
        def _():
            g = g_ref[...]

            def finish(c, _):
                rows = pl.ds(pl.multiple_of(c * ROW_CHUNK, ROW_CHUNK), ROW_CHUNK)
                o_ref[rows, :] = res_ref[rows, :] + _rms(o_ref[rows, :], g)
                return 0

            lax.fori_loop(0, o_ref.shape[0] // ROW_CHUNK, finish, 0)


def mm_norm_res(lhs_list, w, layer, g, res, bm, nk=1):
    m, n = res.shape
    kk = w.shape[1]
    assert m % bm == 0 and kk % nk == 0
    bk = kk // nk
    n_lhs = len(lhs_list)
    if nk == 1:
        lhs_specs = [pl.BlockSpec((bm, a.shape[1]), lambda i, k: (i, 0)) for a in lhs_list]
        w_spec = _resident((None, kk, n), lambda i, k: (layer, 0, 0))
    else:
        assert n_lhs == 1
        kblock = lambda i, k: jnp.where(i % 2 == 0, k, nk - 1 - k)
        lhs_specs = [pl.BlockSpec((bm, bk), lambda i, k: (i, kblock(i, k)))]
        w_spec = pl.BlockSpec((None, bk, n), lambda i, k: (layer, kblock(i, k), 0))
    return pl.pallas_call(
        functools.partial(_mm_norm_res_kernel, n_lhs=n_lhs, nk=nk),
        grid=(m // bm, nk),
        in_specs=lhs_specs + [
            w_spec,
            _resident((1, n), lambda i, k: (0, 0)),
            pl.BlockSpec((bm, n), lambda i, k: (i, 0)),
        ],
        out_specs=pl.BlockSpec((bm, n), lambda i, k: (i, 0)),
        out_shape=jax.ShapeDtypeStruct((m, n), F32),
        compiler_params=_params("parallel", "arbitrary"),
        name="mm_norm_res",
    )(*lhs_list, w, g.reshape(1, n), res)


def _gla_kernel(q_ref, k_ref, v_ref, z_ref, r_ref, w2_ref, bg_ref, gh_ref, tri_ref,
                o_ref, st_ref, *, dk, dv):
    tl = q_ref.shape[0]
    heads = q_ref.shape[1] // dk
    n_chunks = tl // GLA_CHUNK

    @pl.when(pl.program_id(2) == 0)
    def _():
        st_ref[...] = jnp.zeros_like(st_ref)

    pre = jnp.dot(z_ref[...].astype(BF16), w2_ref[...], preferred_element_type=F32) + bg_ref[...]
    logg = (jnp.minimum(pre, 0.0) - jnp.log1p(jnp.exp(-jnp.abs(pre)))) * (1.0 / GLA_TAU)
    hi = logg.astype(BF16)
    lo = (logg - hi.astype(F32)).astype(BF16)
    incl = tri_ref[...]
    cs = jnp.dot(incl, jnp.concatenate([hi, lo], axis=1), preferred_element_type=F32)
    bcum_all = cs[:, :heads * dk] + cs[:, heads * dk:]
    gh = gh_ref[...]

    for hh in range(heads):
        ks = slice(hh * dk, (hh + 1) * dk)
        vs = slice(hh * dv, (hh + 1) * dv)
        bcum = bcum_all[:, ks]
        b_last = [bcum[(c + 1) * GLA_CHUNK - 1:(c + 1) * GLA_CHUNK, :] for c in range(n_chunks)]
        suf = jnp.concatenate([jnp.broadcast_to(bl, (GLA_CHUNK, dk)) for bl in b_last], axis=0) - bcum

        qf = q_ref[:, ks].astype(F32) * dk ** -0.5
        kf = k_ref[:, ks].astype(F32)
        q_dec = (qf * jnp.exp(bcum)).astype(BF16)
        k_intra = (kf * jnp.exp(-bcum)).astype(BF16)
        k_state = (kf * jnp.exp(suf)).astype(BF16)
        v = v_ref[:, vs]

        a = lax.dot_general(q_dec, k_intra, NT_DIMS, preferred_element_type=F32)
        a = jnp.where(incl > 0, a, 0.0).astype(BF16)
        o_intra = jnp.dot(a, v, preferred_element_type=F32)

        st = st_ref[hh]
        outs = []
        for c in range(n_chunks):
            rows = slice(c * GLA_CHUNK, (c + 1) * GLA_CHUNK)
            o_inter = lax.dot_general(q_dec[rows], st.astype(BF16), NT_DIMS,
                                      preferred_element_type=F32)
            outs.append(o_intra[rows] + o_inter)
            decay = jnp.exp(b_last[c])
            upd = lax.dot_general(v[rows], k_state[rows], TN_DIMS, preferred_element_type=F32)
            st = decay * st + upd
        st_ref[hh] = st
        o = _rms(jnp.concatenate(outs, axis=0), gh)
        r = r_ref[:, vs]
        o_ref[:, vs] = (o * (r * (1.0 / (1.0 + jnp.exp(-r))))).astype(o_ref.dtype)


def gla(pa, pb, w2p, bgate, ghead, tri, tl, heads_per_step):
    b, l, _ = pa.shape
    dk = LANES
    dv = 2 * dk
    hp = heads_per_step
    assert GLA_HEADS % hp == 0
    groups = GLA_HEADS // hp
    wk, wv = hp * dk, hp * dv
    kb0 = GLA_HEADS * dk // wk
    vb0 = 2 * GLA_HEADS * dk // wv
    zb = GLA_HEADS * dv // LANES
    return pl.pallas_call(
        functools.partial(_gla_kernel, dk=dk, dv=dv),
        grid=(b, groups, l // tl),
        in_specs=[
            pl.BlockSpec((None, tl, wk), lambda bi, h, t: (bi, t, h)),
            pl.BlockSpec((None, tl, wk), lambda bi, h, t: (bi, t, kb0 + h)),
            pl.BlockSpec((None, tl, wv), lambda bi, h, t: (bi, t, vb0 + h)),
            pl.BlockSpec((None, tl, LANES), lambda bi, h, t: (bi, t, zb)),
            pl.BlockSpec((None, tl, wv), lambda bi, h, t: (bi, t, h)),
            pl.BlockSpec((LANES, wk), lambda bi, h, t: (0, h)),
            pl.BlockSpec((1, wk), lambda bi, h, t: (0, h)),
            _resident((1, dv), lambda bi, h, t: (0, 0)),
            _resident((tl, tl), lambda bi, h, t: (0, 0)),
        ],
        out_specs=pl.BlockSpec((None, tl, wv), lambda bi, h, t: (bi, t, h)),
        out_shape=jax.ShapeDtypeStruct((b, l, GLA_HEADS * dv), BF16),
        scratch_shapes=[pltpu.VMEM((hp, dv, dk), F32)],
        compiler_params=_params("parallel", "parallel", "arbitrary"),
        name="gla",
    )(pa, pa, pa, pb, pb, w2p, bgate, ghead, tri)


def _for_each_block_paired(n, body):
    def pair(p, _):
        body(2 * p)
        body(2 * p + 1)
        return 0

    lax.fori_loop(0, n >> 1, pair, 0)

    @pl.when(n & 1 == 1)
    def _():
        body(n - 1)


def _dsa_kernel(q_ref, iq_ref, iw_ref, k_ref, v_ref, ik_ref, corr_ref, o_ref,
                key_ref, hi_ref, lo_ref, madd_ref, wt_ref, m_ref, l_ref, acc_ref, vt_ref, s_ref,
                *, topk):
    tq = q_ref.shape[0]
    tk = tq
    dh = LANES
    i = pl.program_id(1)
    nkb = i + 1

    wt_ref[...] = (iw_ref[...] * (IDX_HEADS ** -0.5 * IDX_DH ** -0.5)).T
    qpos = lax.broadcasted_iota(jnp.int32, (tk, tq), 1) + i * tq
    kpos0 = lax.broadcasted_iota(jnp.int32, (tk, tq), 0)

    def score_block(kb):
        ik = ik_ref[pl.ds(pl.multiple_of(kb * tk, tk), tk), :].astype(BF16)
        ik_even, ik_odd = ik[:, :LANES], ik[:, LANES:]
        acc = jnp.zeros((tk, tq), F32)
        for p in range(IDX_HEADS // 2):
            qp = iq_ref[:, p * LANES:(p + 1) * LANES]
            for par, ikx in ((0, ik_even), (1, ik_odd)):
                h = 2 * p + par
                d = lax.dot_general(ikx, qp, NT_DIMS, preferred_element_type=F32)
                acc = acc + jnp.maximum(d, 0.0) * wt_ref[h:h + 1, :]
        sc = jnp.where(kpos0 + kb * tk <= qpos, acc, -jnp.inf)
        bits = lax.bitcast_convert_type(sc, jnp.int32)
        key = bits ^ ((bits >> 31) & 0x7FFFFFFF)
        key_ref[kb] = key
        hi_ref[kb] = (key >> 16).astype(jnp.int16)
        lo_ref[kb] = ((key & 0xFFFF) - HALF16).astype(jnp.int16)

    _for_each_block_paired(nkb, score_block)

    n_acc = 4
    slabs = tk // PACK16
    one16, zero16 = jnp.int16(1), jnp.int16(0)

    def rows16(x):
        return jnp.broadcast_to(x, (PACK16, tq)).astype(jnp.int16)

    def count16(src_ref, cand):
        cand16 = rows16(cand)

        def block(kb, cnts):
            cnts = list(cnts)
            for r in range(slabs):
                hit = src_ref[kb, r * PACK16:(r + 1) * PACK16, :] >= cand16
                cnts[r % n_acc] = cnts[r % n_acc] + jnp.where(hit, one16, zero16)
            return tuple(cnts)

        def pair(p, cnts):
            return block(2 * p + 1, block(2 * p, cnts))

        cnts = lax.fori_loop(0, nkb >> 1, pair, (jnp.zeros((PACK16, tq), jnp.int16),) * n_acc)
        cnts = lax.cond(nkb & 1 == 1, lambda c: block(nkb - 1, c), lambda c: c, cnts)
        cnt = ((cnts[0] + cnts[1]) + (cnts[2] + cnts[3])).astype(jnp.int32)
        return jnp.sum(cnt, axis=0, keepdims=True)

    def bisect16(src_ref, kth):
        def step(s, carry):
            t, above = carry
            cand = t + lax.shift_left(jnp.int32(1), 15 - s)
            total = count16(src_ref, cand)
            ok = total >= kth
            return jnp.where(ok, cand, t), jnp.where(ok, above, total)

        return lax.fori_loop(0, 16, step, (jnp.full((1, tq), -HALF16, jnp.int32),
                                           jnp.zeros((1, tq), jnp.int32)))

    t_hi, above = bisect16(hi_ref, topk)
    t_hi16 = rows16(t_hi)

    def keep_threshold_bucket(kb, _):
        for r in range(slabs):
            rows = slice(r * PACK16, (r + 1) * PACK16)
            lo_ref[kb, rows, :] = jnp.where(hi_ref[kb, rows, :] == t_hi16, lo_ref[kb, rows, :],
                                            jnp.int16(-HALF16))
        return 0

    lax.fori_loop(0, nkb, keep_threshold_bucket, 0)
    t_lo, _ = bisect16(lo_ref, topk - above)
    thr = t_hi * (2 * HALF16) + (t_lo + HALF16)

    def mask_block(kb, _):
        vis = kpos0 + kb * tk <= qpos
        madd_ref[kb] = jnp.where(vis, jnp.where(key_ref[kb] >= thr, 0.0, NEG_BIG), NEG_BIG)
        return 0

    lax.fori_loop(0, nkb, mask_block, 0)

    @pl.when(i == 0)
    def _():
        def transpose_values(kb, _):
            vt_ref[kb] = v_ref[pl.ds(pl.multiple_of(kb * tk, tk), tk), :].T
            return 0

        lax.fori_loop(0, vt_ref.shape[0], transpose_values, 0)

    m_ref[...] = jnp.full(m_ref.shape, NEG_BIG, F32)
    l_ref[...] = jnp.zeros(l_ref.shape, F32)
    acc_ref[...] = jnp.zeros(acc_ref.shape, F32)

    ones_rows = jnp.ones((PACK16, tk), BF16)

    def attend(kb, near, buf=0):
        keys = pl.ds(pl.multiple_of(kb * tk, tk), tk)
        madd = madd_ref[kb]
        for h in range(DSA_HEADS):
            hs = slice(h * dh, (h + 1) * dh)
            s_ref[buf, h] = lax.dot_general(k_ref[keys, hs], q_ref[:, hs], NT_DIMS,
                                            preferred_element_type=F32)
        for h in range(DSA_HEADS):
            hs = slice(h * dh, (h + 1) * dh)
            s = s_ref[buf, h]
            if near is not None:
                s = s + corr_ref[h, near]
            s = s + madd
            m_prev = m_ref[h, 0:1, :]
            m_next = jnp.maximum(m_prev, jnp.max(s, axis=0, keepdims=True))
            alpha = jnp.exp2(m_prev - m_next)
            p = jnp.exp2(s - m_next)
            pv = jnp.dot(jnp.concatenate([vt_ref[kb, hs, :], ones_rows], axis=0), p.astype(BF16),
                         preferred_element_type=F32)
            l_next = alpha * l_ref[h, 0:1, :] + pv[dh:dh + 1, :]
            acc_ref[h] = alpha * acc_ref[h] + pv[:dh, :]
            m_ref[h] = jnp.broadcast_to(m_next, (SUBLANES, tq))
            l_ref[h] = jnp.broadcast_to(l_next, (SUBLANES, tq))

    def far_pair(p, _):
        attend(2 * p, None, 0)
        attend(2 * p + 1, None, 1)
        return 0

    n_far = jnp.maximum(i - 1, 0)
    lax.fori_loop(0, n_far >> 1, far_pair, 0)

    @pl.when(n_far & 1 == 1)
    def _():
        attend(n_far - 1, None)

    @pl.when(i >= 1)
    def _():
        attend(i - 1, 1, 0)
        attend(i, 0, 1)

    @pl.when(i == 0)
    def _():
        attend(i, 0)

    for h in range(DSA_HEADS):
        o_h = acc_ref[h] / l_ref[h, 0:1, :]
        o_ref[:, h * dh:(h + 1) * dh] = o_h.T.astype(o_ref.dtype)


def dsa(pa, pb, corr, tq):
    b, l, _ = pa.shape
    dh = LANES
    hd = DSA_HEADS * dh
    qb = (2 * GLA_HEADS * LANES + GLA_HEADS * 2 * LANES) // hd
    iwb = GLA_HEADS * 2 * LANES // LANES + 1
    ikb = (iwb + 1) * LANES // (2 * LANES)
    topk = min(TOPK_MAX, l // 4)
    assert l % tq == 0 and tq >= topk and tq >= REL_MAX_DIST
    nkb = l // tq
    assert nkb * tq // PACK16 < HALF16
    return pl.pallas_call(
        functools.partial(_dsa_kernel, topk=topk),
        grid=(b, l // tq),
        in_specs=[
            pl.BlockSpec((None, tq, hd), lambda bi, i: (bi, i, qb)),
            pl.BlockSpec((None, tq, IDX_HEADS * IDX_DH), lambda bi, i: (bi, i, qb + 3)),
            pl.BlockSpec((None, tq, LANES), lambda bi, i: (bi, i, iwb)),
            _resident((None, l, hd), lambda bi, i: (bi, 0, qb + 1)),
            _resident((None, l, hd), lambda bi, i: (bi, 0, qb + 2)),
            _resident((None, l, 2 * LANES), lambda bi, i: (bi, 0, ikb)),
            _resident((DSA_HEADS, 2, tq, tq), lambda bi, i: (0, 0, 0, 0)),
        ],
        out_specs=pl.BlockSpec((None, tq, hd), lambda bi, i: (bi, i, 0)),
        out_shape=jax.ShapeDtypeStruct((b, l, hd), BF16),
        scratch_shapes=[
            pltpu.VMEM((nkb, tq, tq), jnp.int32),
            pltpu.VMEM((nkb, tq, tq), jnp.int16),
            pltpu.VMEM((nkb, tq, tq), jnp.int16),
            pltpu.VMEM((nkb, tq, tq), F32),
            pltpu.VMEM((LANES, tq), F32),
            pltpu.VMEM((DSA_HEADS, SUBLANES, tq), F32),
            pltpu.VMEM((DSA_HEADS, SUBLANES, tq), F32),
            pltpu.VMEM((DSA_HEADS, dh, tq), F32),
            pltpu.VMEM((nkb, hd, tq), BF16),
            pltpu.VMEM((2, DSA_HEADS, tq, tq), F32),
        ],
        compiler_params=_params("parallel", "arbitrary"),
        name="dsa",
    )(pa, pa, pb, pa, pa, pb, corr)


def _xattn_kernel(x_ref, gq_ref, wq_ref, kv_ref, wo_ref, go_ref, o_ref, q_ref, a_ref):
    d = x_ref.shape[1]
    dh = d // XA_HEADS
    scale = dh ** -0.5
    halves = _row_parts(x_ref.shape[0], 2)
    gq = gq_ref[...]
    for rows in halves:
        h = _rms(x_ref[rows, :], gq).astype(BF16)
        q_ref[rows, :] = jnp.dot(h, wq_ref[...], preferred_element_type=F32).astype(q_ref.dtype)
    heads = [slice(h * dh, (h + 1) * dh) for h in range(XA_HEADS)]
    scores = [lax.dot_general(q_ref[:, hs], kv_ref[:, hs], NT_DIMS, preferred_element_type=F32)
              for hs in heads]
    for h, hs in enumerate(heads):
        s = scores[h] * scale
        e = jnp.exp(s - jnp.max(s, axis=-1, keepdims=True))
        o = jnp.dot(e.astype(BF16), kv_ref[:, d + h * dh:d + (h + 1) * dh], preferred_element_type=F32)
        a_ref[:, hs] = (o / jnp.sum(e, axis=-1, keepdims=True)).astype(a_ref.dtype)
    go = go_ref[...]
    for rows in halves:
        y = jnp.dot(a_ref[rows, :], wo_ref[...], preferred_element_type=F32)
        o_ref[rows, :] = x_ref[rows, :] + _rms(y, go)


def xattn(x, kv, gq, wq, go, wo, layer, tq):
    b, l, d = x.shape
    nm = kv.shape[1]
    vec = lambda: _resident((1, d), lambda bi, i: (0, 0))
    mat = lambda: _resident((None, d, d), lambda bi, i: (layer, 0, 0))
    return pl.pallas_call(
        _xattn_kernel,
        grid=(b, l // tq),
        in_specs=[
            pl.BlockSpec((None, tq, d), lambda bi, i: (bi, i, 0)),
            vec(), mat(),
            pl.BlockSpec((None, nm, 2 * d), lambda bi, i: (bi, 0, 0)),
            mat(), vec(),
        ],
        out_specs=pl.BlockSpec((None, tq, d), lambda bi, i: (bi, i, 0)),
        out_shape=jax.ShapeDtypeStruct((b, l, d), F32),
        scratch_shapes=[pltpu.VMEM((tq, d), BF16)] * 2,
        compiler_params=_params("parallel", "parallel"),
        name="xattn",
    )(x, gq.reshape(1, d), wq, kv, wo, go.reshape(1, d))


def _rel_bucket(dist):
    n = jnp.maximum(dist, 0)
    max_exact = REL_BUCKETS // 2
    nf = jnp.maximum(n, 1).astype(F32)
    large = max_exact + (jnp.log(nf / max_exact) / math.log(REL_MAX_DIST / max_exact)
                         * (REL_BUCKETS - max_exact)).astype(jnp.int32)
    large = jnp.minimum(large, REL_BUCKETS - 1)
    return jnp.where(n < max_exact, n, large)


def _bias_correction_tiles(rel_table, tq):
    t = jnp.arange(tq)
    d0 = t[:, None] - t[None, :]
    buckets = jnp.stack([_rel_bucket(d0), _rel_bucket(d0 + tq)], axis=0)
    onehot = (buckets[..., None] == jnp.arange(REL_BUCKETS)).astype(F32)
    far = rel_table[_rel_bucket(jnp.int32(2 * tq))]
    return jnp.einsum("ntsk,kh->hnst", onehot, (rel_table - far) * LOG2E,
                      precision=lax.Precision.HIGHEST)


def _chunk_sum_matrices(tl):
    r = jnp.arange(tl)
    same = (r[:, None] // GLA_CHUNK) == (r[None, :] // GLA_CHUNK)
    return (same & (r[None, :] <= r[:, None])).astype(BF16)


def _split_in_proj(w):
    depth, d, _ = w.shape
    half = d // 2
    sizes = [GLA_HEADS * LANES, GLA_HEADS * LANES, half, GLA_RANK, half,
             half, half, half, IDX_HEADS * IDX_DH, IDX_DH, IDX_HEADS]
    offs = [0]
    for s in sizes:
        offs.append(offs[-1] + s)
    assert offs[-1] == w.shape[2]
    seg = lambda n: w[:, :, offs[n]:offs[n + 1]]
    zero = lambda n: jnp.zeros((depth, d, n), w.dtype)
    ik = seg(9)
    w_head = w[:, :, :offs[3]]
    w_mid = w[:, :, offs[5]:offs[9]]
    w_small = jnp.concatenate([seg(4), seg(3), zero(LANES - GLA_RANK), seg(10), zero(LANES - IDX_HEADS),
                               ik, zero(2 * IDX_DH), ik], axis=2)
    return w_head.astype(BF16), w_mid.astype(BF16), w_small.astype(BF16)


def kernel(x, mem, norm_gains, w_in, w_gate_up, b_gate, gla_head_gain, rel_table,
           w_out, w_xq, w_xkv, w_xo, w_ffn_in, w_ffn_out):
    b, l, d = x.shape
    nm = mem.shape[1]
    depth = w_in.shape[0]
    d_ff = w_ffn_out.shape[1]
    m = b * l
    assert d // 2 // GLA_HEADS == 2 * LANES and d // 2 // DSA_HEADS == LANES

    t = TILES
    corr = _bias_correction_tiles(rel_table, t["dsa_q"])
    tri = _chunk_sum_matrices(t["gla_t"])
    w2p = jnp.pad(w_gate_up, ((0, 0), (0, LANES - GLA_RANK), (0, 0))).astype(BF16)

    w_out, w_xq, w_xkv, w_xo, w_ffn_in, w_ffn_out = (
        w.astype(BF16) for w in (w_out, w_xq, w_xkv, w_xo, w_ffn_in, w_ffn_out))
    w_head, w_mid, w_small = _split_in_proj(w_in)

    xf = x.reshape(m, d)
    memf = mem.reshape(b * nm, d)
    for li in range(depth):
        g = norm_gains[li]
        assert t["in_proj"][1] == DSA_HEADS * LANES
        pa, pb = in_proj(xf, g[0], w_head, w_mid, w_small, li, DSA_LOGIT_SCALE, *t["in_proj"])
        pa, pb = pa.reshape(b, l, -1), pb.reshape(b, l, -1)
        o_gla = gla(pa, pb, w2p[li], b_gate[li].reshape(1, -1),
                    gla_head_gain[li].reshape(1, -1), tri, t["gla_t"], heads_per_step=GLA_HEADS)
        o_dsa = dsa(pa, pb, corr, t["dsa_q"])
        xf = mm_norm_res([o_gla.reshape(m, -1), o_dsa.reshape(m, -1)], w_out, li, g[1], xf,
                         bm=t["out_proj_m"])
        kv = norm_matmul(memf, g[6], w_xkv, li, BF16, bm=min(t["mem_kv"][0], b * nm), bn=t["mem_kv"][1])
        xf = xattn(xf.reshape(b, l, d), kv.reshape(b, nm, 2 * d), g[2], w_xq, g[3], w_xo, li,
                   tq=t["xattn_q"]).reshape(m, d)
        act = ffn_in(xf, g[4], w_ffn_in, li, d_ff, *t["ffn_in"])
        xf = mm_norm_res([act], w_ffn_out, li, g[5], xf, bm=t["ffn_out_m"])
    return xf.reshape(b, l, d)
```

```python
import functools
import math

import jax
import jax.numpy as jnp
from jax import lax
from jax.experimental import pallas as pl
from jax.experimental.pallas import tpu as pltpu

F32 = jnp.float32
BF16 = jnp.bfloat16

EPS = 1e-6
GLA_HEADS = 4
GLA_RANK = 16
GLA_TAU = 16.0
GLA_CHUNK = 64
DSA_HEADS = 8
IDX_HEADS = 16
IDX_DH = 64
TOPK_MAX = 256
REL_BUCKETS = 32
REL_MAX_DIST = 128
XA_HEADS = 4

LANES = 128
SUBLANES = 8
VMEM_LIMIT_BYTES = 60000 * 1024

PACK16 = 16
ROW_CHUNK = 128
NORM_PARTS = 4

NEG_BIG = -1e30
HALF16 = 2 ** 15
LOG2E = 1.4426950408889634
DSA_LOGIT_SCALE = LANES ** -0.5 * LOG2E

TILES = dict(
    in_proj=(1024, 1024, 768),
    gla_t=512,
    dsa_q=256,
    out_proj_m=1024,
    mem_kv=(512, 1024),
    xattn_q=512,
    ffn_in=(1024, 512),
    ffn_out_m=512,
)

NT_DIMS = (((1,), (1,)), ((), ()))
TN_DIMS = (((0,), (0,)), ((), ()))


def _params(*sem):
    return pltpu.CompilerParams(dimension_semantics=sem, vmem_limit_bytes=VMEM_LIMIT_BYTES)


def _resident(shape, index_map):
    return pl.BlockSpec(shape, index_map, pipeline_mode=pl.Buffered(1))


def _rms(x, g):
    return x * lax.rsqrt(jnp.mean(x * x, axis=-1, keepdims=True) + EPS) * g


def _row_parts(n_rows, parts):
    size = n_rows // parts
    return [slice(p * size, (p + 1) * size) for p in range(parts)]


def _rms_to(x_ref, g_ref, h_ref):
    g = g_ref[...]

    def body(c, _):
        rows = pl.ds(pl.multiple_of(c * ROW_CHUNK, ROW_CHUNK), ROW_CHUNK)
        h_ref[rows, :] = _rms(x_ref[rows, :], g).astype(h_ref.dtype)
        return 0

    lax.fori_loop(0, x_ref.shape[0] // ROW_CHUNK, body, 0)


def _norm_matmul_kernel(x_ref, g_ref, w_ref, o_ref, h_ref):
    @pl.when(pl.program_id(1) == 0)
    def _():
        _rms_to(x_ref, g_ref, h_ref)

    o_ref[...] = jnp.dot(h_ref[...], w_ref[...], preferred_element_type=F32).astype(o_ref.dtype)


def norm_matmul(x, g, w, layer, out_dtype, bm, bn):
    m, k = x.shape
    n = w.shape[2]
    assert m % bm == 0 and n % bn == 0
    return pl.pallas_call(
        _norm_matmul_kernel,
        grid=(m // bm, n // bn),
        in_specs=[
            pl.BlockSpec((bm, k), lambda i, j: (i, 0)),
            _resident((1, k), lambda i, j: (0, 0)),
            pl.BlockSpec((None, k, bn), lambda i, j: (layer, 0, j)),
        ],
        out_specs=pl.BlockSpec((bm, bn), lambda i, j: (i, j)),
        out_shape=jax.ShapeDtypeStruct((m, n), out_dtype),
        scratch_shapes=[pltpu.VMEM((bm, k), BF16)],
        compiler_params=_params("parallel", "arbitrary"),
        name="norm_matmul",
    )(x, g.reshape(1, k), w)


def _in_proj_slot(i, j, n_slots):
    return jnp.where(i % 2 == 0, j, n_slots - 1 - j)


def _in_proj_kernel(x_ref, g_ref, wh_ref, wm_ref, ws_ref, oa_ref, ob_ref, h_ref, *, nh, ns, nm,
                    mid0_scale):
    s = _in_proj_slot(pl.program_id(0), pl.program_id(1), nh + ns + nm)

    first = pl.program_id(1) == 0

    def project(w_ref, o_ref, scale=None):
        y = jnp.dot(h_ref[...], w_ref[...], preferred_element_type=F32)
        o_ref[...] = (y if scale is None else y * scale).astype(o_ref.dtype)

    def project_first(w_ref, o_ref):
        g = g_ref[...]
        for rows in _row_parts(x_ref.shape[0], NORM_PARTS):
            h = _rms(x_ref[rows, :], g).astype(h_ref.dtype)
            h_ref[rows, :] = h
            o_ref[rows, :] = jnp.dot(h, w_ref[...], preferred_element_type=F32).astype(o_ref.dtype)

    pl.when(first & (s < nh))(lambda: project_first(wh_ref, oa_ref))
    pl.when(first & (s >= nh + ns))(lambda: project_first(wm_ref, oa_ref))
    later = jnp.logical_not(first)
    pl.when(later & (s < nh))(lambda: project(wh_ref, oa_ref))
    pl.when((s >= nh) & (s < nh + ns))(lambda: project(ws_ref, ob_ref))
    pl.when(later & (s == nh + ns))(lambda: project(wm_ref, oa_ref, mid0_scale))
    pl.when(later & (s > nh + ns))(lambda: project(wm_ref, oa_ref))


def in_proj(x, g, w_head, w_mid, w_small, layer, mid0_scale, bm, bn, bs):
    m, k = x.shape
    n_head, n_mid, n_small = w_head.shape[2], w_mid.shape[2], w_small.shape[2]
    assert m % bm == 0 and n_head % bn == 0 and n_mid % bn == 0 and n_small % bs == 0
    nh, nm, ns = n_head // bn, n_mid // bn, n_small // bs
    n_slots = nh + ns + nm
    assert nm > 1

    def slot(i, j):
        return _in_proj_slot(i, j, n_slots)

    def pa_block(i, j):
        s = slot(i, j)
        hold = jnp.where(i % 2 == 0, nh - 1, nh)
        return jnp.where(s < nh, s, jnp.where(s >= nh + ns, s - ns, hold))

    return pl.pallas_call(
        functools.partial(_in_proj_kernel, nh=nh, ns=ns, nm=nm, mid0_scale=mid0_scale),
        grid=(m // bm, n_slots),
        in_specs=[
            pl.BlockSpec((bm, k), lambda i, j: (i, 0)),
            _resident((1, k), lambda i, j: (0, 0)),
            pl.BlockSpec((None, k, bn), lambda i, j: (layer, 0, jnp.clip(slot(i, j), 0, nh - 1))),
            pl.BlockSpec((None, k, bn), lambda i, j: (layer, 0, jnp.clip(slot(i, j) - nh - ns, 0, nm - 1))),
            pl.BlockSpec((None, k, bs), lambda i, j: (layer, 0, jnp.clip(slot(i, j) - nh, 0, ns - 1))),
        ],
        out_specs=[
            pl.BlockSpec((bm, bn), lambda i, j: (i, pa_block(i, j))),
            pl.BlockSpec((bm, bs), lambda i, j: (i, jnp.clip(slot(i, j) - nh, 0, ns - 1))),
        ],
        out_shape=[
            jax.ShapeDtypeStruct((m, n_head + n_mid), BF16),
            jax.ShapeDtypeStruct((m, n_small), F32),
        ],
        scratch_shapes=[pltpu.VMEM((bm, k), BF16)],
        compiler_params=_params("parallel", "arbitrary"),
        name="in_proj",
    )(x, g.reshape(1, k), w_head, w_mid, w_small)


def _ffn_in_kernel(x_ref, g_ref, wg_ref, wu_ref, o_ref, h_ref):
    def gated(h):
        a = jnp.dot(h, wg_ref[...], preferred_element_type=F32)
        b = jnp.dot(h, wu_ref[...], preferred_element_type=F32)
        return (a * (1.0 / (1.0 + jnp.exp(-a))) * b).astype(o_ref.dtype)

    @pl.when(pl.program_id(1) == 0)
    def _():
        g = g_ref[...]
        for rows in _row_parts(x_ref.shape[0], NORM_PARTS):
            h = _rms(x_ref[rows, :], g).astype(h_ref.dtype)
            h_ref[rows, :] = h
            o_ref[rows, :] = gated(h)

    @pl.when(pl.program_id(1) > 0)
    def _():
        o_ref[...] = gated(h_ref[...])


def ffn_in(x, g, w, layer, d_ff, bm, bn):
    m, k = x.shape
    assert m % bm == 0 and d_ff % bn == 0
    nj = d_ff // bn
    return pl.pallas_call(
        _ffn_in_kernel,
        grid=(m // bm, nj),
        in_specs=[
            pl.BlockSpec((bm, k), lambda i, j: (i, 0)),
            _resident((1, k), lambda i, j: (0, 0)),
            pl.BlockSpec((None, k, bn), lambda i, j: (layer, 0, j)),
            pl.BlockSpec((None, k, bn), lambda i, j: (layer, 0, j + nj)),
        ],
        out_specs=pl.BlockSpec((bm, bn), lambda i, j: (i, j)),
        out_shape=jax.ShapeDtypeStruct((m, d_ff), BF16),
        scratch_shapes=[pltpu.VMEM((bm, k), BF16)],
        compiler_params=_params("parallel", "arbitrary"),
        name="ffn_in",
    )(x, g.reshape(1, k), w, w)


def _mm_norm_res_kernel(*refs, n_lhs, nk):
    lhs = refs[:n_lhs]
    w_ref, g_ref, res_ref, o_ref = refs[n_lhs:n_lhs + 4]
    if nk == 1:
        g = g_ref[...]
        half = o_ref.shape[0] // 2
        for rows in (slice(0, half), slice(half, 2 * half)):
            y = None
            off = 0
            for a in lhs:
                wd = a.shape[1]
                part = jnp.dot(a[rows, :], w_ref[off:off + wd, :], preferred_element_type=F32)
                y = part if y is None else y + part
                off += wd
            o_ref[rows, :] = res_ref[rows, :] + _rms(y, g)
    else:
        k = pl.program_id(1)
        part = jnp.dot(lhs[0][...], w_ref[...], preferred_element_type=F32)

        @pl.when(k == 0)
        def _():
            o_ref[...] = part

        @pl.when(k > 0)
        def _():
            o_ref[...] += part

        @pl.when(k == nk - 1)
        def _():
            g = g_ref[...]

            def finish(c, _):
                rows = pl.ds(pl.multiple_of(c * ROW_CHUNK, ROW_CHUNK), ROW_CHUNK)
                o_ref[rows, :] = res_ref[rows, :] + _rms(o_ref[rows, :], g)
                return 0

            lax.fori_loop(0, o_ref.shape[0] // ROW_CHUNK, finish, 0)


def mm_norm_res(lhs_list, w, layer, g, res, bm, nk=1):
    m, n = res.shape
    kk = w.shape[1]
    assert m % bm == 0 and kk % nk == 0
    bk = kk // nk
    n_lhs = len(lhs_list)
    if nk == 1:
        lhs_specs = [pl.BlockSpec((bm, a.shape[1]), lambda i, k: (i, 0)) for a in lhs_list]
        w_spec = _resident((None, kk, n), lambda i, k: (layer, 0, 0))
    else:
        assert n_lhs == 1
        kblock = lambda i, k: jnp.where(i % 2 == 0, k, nk - 1 - k)
        lhs_specs = [pl.BlockSpec((bm, bk), lambda i, k: (i, kblock(i, k)))]
        w_spec = pl.BlockSpec((None, bk, n), lambda i, k: (layer, kblock(i, k), 0))
    return pl.pallas_call(
        functools.partial(_mm_norm_res_kernel, n_lhs=n_lhs, nk=nk),
        grid=(m // bm, nk),
        in_specs=lhs_specs + [
            w_spec,
            _resident((1, n), lambda i, k: (0, 0)),
            pl.BlockSpec((bm, n), lambda i, k: (i, 0)),
        ],
        out_specs=pl.BlockSpec((bm, n), lambda i, k: (i, 0)),
        out_shape=jax.ShapeDtypeStruct((m, n), F32),
        compiler_params=_params("parallel", "arbitrary"),
        name="mm_norm_res",
    )(*lhs_list, w, g.reshape(1, n), res)


def _gla_kernel(q_ref, k_ref, v_ref, z_ref, r_ref, w2_ref, bg_ref, gh_ref, tri_ref,
                o_ref, st_ref, *, dk, dv):
    tl = q_ref.shape[0]
    heads = q_ref.shape[1] // dk
    n_chunks = tl // GLA_CHUNK

    @pl.when(pl.program_id(2) == 0)
    def _():
        st_ref[...] = jnp.zeros_like(st_ref)

    pre = jnp.dot(z_ref[...].astype(BF16), w2_ref[...], preferred_element_type=F32) + bg_ref[...]
    logg = (jnp.minimum(pre, 0.0) - jnp.log1p(jnp.exp(-jnp.abs(pre)))) * (1.0 / GLA_TAU)
    hi = logg.astype(BF16)
    lo = (logg - hi.astype(F32)).astype(BF16)
    incl = tri_ref[...]
    cs = jnp.dot(incl, jnp.concatenate([hi, lo], axis=1), preferred_element_type=F32)
    bcum_all = cs[:, :heads * dk] + cs[:, heads * dk:]
    gh = gh_ref[...]

    for hh in range(heads):
        ks = slice(hh * dk, (hh + 1) * dk)
        vs = slice(hh * dv, (hh + 1) * dv)
        bcum = bcum_all[:, ks]
        b_last = [bcum[(c + 1) * GLA_CHUNK - 1:(c + 1) * GLA_CHUNK, :] for c in range(n_chunks)]
        suf = jnp.concatenate([jnp.broadcast_to(bl, (GLA_CHUNK, dk)) for bl in b_last], axis=0) - bcum

        qf = q_ref[:, ks].astype(F32) * dk ** -0.5
        kf = k_ref[:, ks].astype(F32)
        q_dec = (qf * jnp.exp(bcum)).astype(BF16)
        k_intra = (kf * jnp.exp(-bcum)).astype(BF16)
        k_state = (kf * jnp.exp(suf)).astype(BF16)
        v = v_ref[:, vs]

        a = lax.dot_general(q_dec, k_intra, NT_DIMS, preferred_element_type=F32)
        a = jnp.where(incl > 0, a, 0.0).astype(BF16)
        o_intra = jnp.dot(a, v, preferred_element_type=F32)

        st = st_ref[hh]
        outs = []
        for c in range(n_chunks):
            rows = slice(c * GLA_CHUNK, (c + 1) * GLA_CHUNK)
            o_inter = lax.dot_general(q_dec[rows], st.astype(BF16), NT_DIMS,
                                      preferred_element_type=F32)
            outs.append(o_intra[rows] + o_inter)
            decay = jnp.exp(b_last[c])
            upd = lax.dot_general(v[rows], k_state[rows], TN_DIMS, preferred_element_type=F32)
            st = decay * st + upd
        st_ref[hh] = st
        o = _rms(jnp.concatenate(outs, axis=0), gh)
        r = r_ref[:, vs]
        o_ref[:, vs] = (o * (r * (1.0 / (1.0 + jnp.exp(-r))))).astype(o_ref.dtype)


def gla(pa, pb, w2p, bgate, ghead, tri, tl, heads_per_step):
    b, l, _ = pa.shape
    dk = LANES
    dv = 2 * dk
    hp = heads_per_step
    assert GLA_HEADS % hp == 0
    groups = GLA_HEADS // hp
    wk, wv = hp * dk, hp * dv
    kb0 = GLA_HEADS * dk // wk
    vb0 = 2 * GLA_HEADS * dk // wv
    zb = GLA_HEADS * dv // LANES
    return pl.pallas_call(
        functools.partial(_gla_kernel, dk=dk, dv=dv),
        grid=(b, groups, l // tl),
        in_specs=[
            pl.BlockSpec((None, tl, wk), lambda bi, h, t: (bi, t, h)),
            pl.BlockSpec((None, tl, wk), lambda bi, h, t: (bi, t, kb0 + h)),
            pl.BlockSpec((None, tl, wv), lambda bi, h, t: (bi, t, vb0 + h)),
            pl.BlockSpec((None, tl, LANES), lambda bi, h, t: (bi, t, zb)),
            pl.BlockSpec((None, tl, wv), lambda bi, h, t: (bi, t, h)),
            pl.BlockSpec((LANES, wk), lambda bi, h, t: (0, h)),
            pl.BlockSpec((1, wk), lambda bi, h, t: (0, h)),
            _resident((1, dv), lambda bi, h, t: (0, 0)),
            _resident((tl, tl), lambda bi, h, t: (0, 0)),
        ],
        out_specs=pl.BlockSpec((None, tl, wv), lambda bi, h, t: (bi, t, h)),
        out_shape=jax.ShapeDtypeStruct((b, l, GLA_HEADS * dv), BF16),
        scratch_shapes=[pltpu.VMEM((hp, dv, dk), F32)],
        compiler_params=_params("parallel", "parallel", "arbitrary"),
        name="gla",
    )(pa, pa, pa, pb, pb, w2p, bgate, ghead, tri)


def _for_each_block_paired(n, body):
    def quad(p, _):
        for u in range(4):
            body(4 * p + u)
        return 0

    lax.fori_loop(0, n >> 2, quad, 0)
    base = (n >> 2) << 2

    @pl.when(n & 2 == 2)
    def _():
        body(base)
        body(base + 1)

    @pl.when(n & 1 == 1)
    def _():
        body(n - 1)


def _dsa_kernel(q_ref, iq_ref, iw_ref, k_ref, v_ref, ik_ref, corr_ref, o_ref,
                key_ref, hi_ref, lo_ref, madd_ref, wt_ref, m_ref, l_ref, acc_ref, vt_ref, s_ref,
                *, topk):
    tq = q_ref.shape[0]
    tk = tq
    dh = LANES
    i = pl.program_id(1)
    nkb = i + 1

    wt_ref[...] = (iw_ref[...] * (IDX_HEADS ** -0.5 * IDX_DH ** -0.5)).T
    qpos = lax.broadcasted_iota(jnp.int32, (tk, tq), 1) + i * tq
    kpos0 = lax.broadcasted_iota(jnp.int32, (tk, tq), 0)

    def score_block(kb):
        ik = ik_ref[pl.ds(pl.multiple_of(kb * tk, tk), tk), :].astype(BF16)
        ik_even, ik_odd = ik[:, :LANES], ik[:, LANES:]
        acc = jnp.zeros((tk, tq), F32)
        for p in range(IDX_HEADS // 2):
            qp = iq_ref[:, p * LANES:(p + 1) * LANES]
            for par, ikx in ((0, ik_even), (1, ik_odd)):
                h = 2 * p + par
                d = lax.dot_general(ikx, qp, NT_DIMS, preferred_element_type=F32)
                acc = acc + jnp.maximum(d, 0.0) * wt_ref[h:h + 1, :]
        sc = jnp.where(kpos0 + kb * tk <= qpos, acc, -jnp.inf)
        bits = lax.bitcast_convert_type(sc, jnp.int32)
        key = bits ^ ((bits >> 31) & 0x7FFFFFFF)
        key_ref[kb] = key
        hi_ref[kb] = (key >> 16).astype(jnp.int16)
        lo_ref[kb] = ((key & 0xFFFF) - HALF16).astype(jnp.int16)

    _for_each_block_paired(nkb, score_block)

    n_acc = 4
    slabs = tk // PACK16
    one16, zero16 = jnp.int16(1), jnp.int16(0)

    def rows16(x):
        return jnp.broadcast_to(x, (PACK16, tq)).astype(jnp.int16)

    def count16(src_ref, cand):
        cand16 = rows16(cand)

        def block(kb, cnts):
            cnts = list(cnts)
            for r in range(slabs):
                hit = src_ref[kb, r * PACK16:(r + 1) * PACK16, :] >= cand16
                cnts[r % n_acc] = cnts[r % n_acc] + jnp.where(hit, one16, zero16)
            return tuple(cnts)

        def pair(p, cnts):
            return block(2 * p + 1, block(2 * p, cnts))

        cnts = lax.fori_loop(0, nkb >> 1, pair, (jnp.zeros((PACK16, tq), jnp.int16),) * n_acc)
        cnts = lax.cond(nkb & 1 == 1, lambda c: block(nkb - 1, c), lambda c: c, cnts)
        cnt = ((cnts[0] + cnts[1]) + (cnts[2] + cnts[3])).astype(jnp.int32)
        return jnp.sum(cnt, axis=0, keepdims=True)

    def bisect16(src_ref, kth):
        def step(s, carry):
            t, above = carry
            cand = t + lax.shift_left(jnp.int32(1), 15 - s)
            total = count16(src_ref, cand)
            ok = total >= kth
            return jnp.where(ok, cand, t), jnp.where(ok, above, total)

        return lax.fori_loop(0, 16, step, (jnp.full((1, tq), -HALF16, jnp.int32),
                                           jnp.zeros((1, tq), jnp.int32)))

    t_hi, above = bisect16(hi_ref, topk)
    t_hi16 = rows16(t_hi)

    def keep_threshold_bucket(kb, _):
        for r in range(slabs):
            rows = slice(r * PACK16, (r + 1) * PACK16)
            lo_ref[kb, rows, :] = jnp.where(hi_ref[kb, rows, :] == t_hi16, lo_ref[kb, rows, :],
                                            jnp.int16(-HALF16))
        return 0

    lax.fori_loop(0, nkb, keep_threshold_bucket, 0)
    t_lo, _ = bisect16(lo_ref, topk - above)
    thr = t_hi * (2 * HALF16) + (t_lo + HALF16)

    def mask_block(kb, _):
        vis = kpos0 + kb * tk <= qpos
        madd_ref[kb] = jnp.where(vis, jnp.where(key_ref[kb] >= thr, 0.0, NEG_BIG), NEG_BIG)
        return 0

    lax.fori_loop(0, nkb, mask_block, 0)

    @pl.when(i == 0)
    def _():
        def transpose_values(kb, _):
            vt_ref[kb] = v_ref[pl.ds(pl.multiple_of(kb * tk, tk), tk), :].T
            return 0

        lax.fori_loop(0, vt_ref.shape[0], transpose_values, 0)

    m_ref[...] = jnp.full(m_ref.shape, NEG_BIG, F32)
    l_ref[...] = jnp.zeros(l_ref.shape, F32)
    acc_ref[...] = jnp.zeros(acc_ref.shape, F32)

    ones_rows = jnp.ones((PACK16, tk), BF16)

    def attend(kb, near, buf=0):
        keys = pl.ds(pl.multiple_of(kb * tk, tk), tk)
        madd = madd_ref[kb]
        for h in range(DSA_HEADS):
            hs = slice(h * dh, (h + 1) * dh)
            s_ref[buf, h] = lax.dot_general(k_ref[keys, hs], q_ref[:, hs], NT_DIMS,
                                            preferred_element_type=F32)
        for h in range(DSA_HEADS):
            hs = slice(h * dh, (h + 1) * dh)
            s = s_ref[buf, h]
            if near is not None:
                s = s + corr_ref[h, near]
            s = s + madd
            m_prev = m_ref[h, 0:1, :]
            m_next = jnp.maximum(m_prev, jnp.max(s, axis=0, keepdims=True))
            alpha = jnp.exp2(m_prev - m_next)
            p = jnp.exp2(s - m_next)
            pv = jnp.dot(jnp.concatenate([vt_ref[kb, hs, :], ones_rows], axis=0), p.astype(BF16),
                         preferred_element_type=F32)
            l_next = alpha * l_ref[h, 0:1, :] + pv[dh:dh + 1, :]
            acc_ref[h] = alpha * acc_ref[h] + pv[:dh, :]
            m_ref[h] = jnp.broadcast_to(m_next, (SUBLANES, tq))
            l_ref[h] = jnp.broadcast_to(l_next, (SUBLANES, tq))

    def far_pair(p, _):
        attend(2 * p, None, 0)
        attend(2 * p + 1, None, 1)
        return 0

    n_far = jnp.maximum(i - 1, 0)
    lax.fori_loop(0, n_far >> 1, far_pair, 0)

    @pl.when(n_far & 1 == 1)
    def _():
        attend(n_far - 1, None)

    @pl.when(i >= 1)
    def _():
        attend(i - 1, 1, 0)
        attend(i, 0, 1)

    @pl.when(i == 0)
    def _():
        attend(i, 0)

    for h in range(DSA_HEADS):
        o_h = acc_ref[h] / l_ref[h, 0:1, :]
        o_ref[:, h * dh:(h + 1) * dh] = o_h.T.astype(o_ref.dtype)


def dsa(pa, pb, corr, tq):
    b, l, _ = pa.shape
    dh = LANES
    hd = DSA_HEADS * dh
    qb = (2 * GLA_HEADS * LANES + GLA_HEADS * 2 * LANES) // hd
    iwb = GLA_HEADS * 2 * LANES // LANES + 1
    ikb = (iwb + 1) * LANES // (2 * LANES)
    topk = min(TOPK_MAX, l // 4)
    assert l % tq == 0 and tq >= topk and tq >= REL_MAX_DIST
    nkb = l // tq
    assert nkb * tq // PACK16 < HALF16
    return pl.pallas_call(
        functools.partial(_dsa_kernel, topk=topk),
        grid=(b, l // tq),
        in_specs=[
            pl.BlockSpec((None, tq, hd), lambda bi, i: (bi, i, qb)),
            pl.BlockSpec((None, tq, IDX_HEADS * IDX_DH), lambda bi, i: (bi, i, qb + 3)),
            pl.BlockSpec((None, tq, LANES), lambda bi, i: (bi, i, iwb)),
            _resident((None, l, hd), lambda bi, i: (bi, 0, qb + 1)),
            _resident((None, l, hd), lambda bi, i: (bi, 0, qb + 2)),
            _resident((None, l, 2 * LANES), lambda bi, i: (bi, 0, ikb)),
            _resident((DSA_HEADS, 2, tq, tq), lambda bi, i: (0, 0, 0, 0)),
        ],
        out_specs=pl.BlockSpec((None, tq, hd), lambda bi, i: (bi, i, 0)),
        out_shape=jax.ShapeDtypeStruct((b, l, hd), BF16),
        scratch_shapes=[
            pltpu.VMEM((nkb, tq, tq), jnp.int32),
            pltpu.VMEM((nkb, tq, tq), jnp.int16),
            pltpu.VMEM((nkb, tq, tq), jnp.int16),
            pltpu.VMEM((nkb, tq, tq), F32),
            pltpu.VMEM((LANES, tq), F32),
            pltpu.VMEM((DSA_HEADS, SUBLANES, tq), F32),
            pltpu.VMEM((DSA_HEADS, SUBLANES, tq), F32),
            pltpu.VMEM((DSA_HEADS, dh, tq), F32),
            pltpu.VMEM((nkb, hd, tq), BF16),
            pltpu.VMEM((2, DSA_HEADS, tq, tq), F32),
        ],
        compiler_params=_params("parallel", "arbitrary"),
        name="dsa",
    )(pa, pa, pb, pa, pa, pb, corr)


def _xattn_kernel(x_ref, gq_ref, wq_ref, kv_ref, wo_ref, go_ref, o_ref, q_ref, a_ref):
    d = x_ref.shape[1]
    dh = d // XA_HEADS
    scale = dh ** -0.5
    halves = _row_parts(x_ref.shape[0], 2)
    gq = gq_ref[...]
    for rows in halves:
        h = _rms(x_ref[rows, :], gq).astype(BF16)
        q_ref[rows, :] = jnp.dot(h, wq_ref[...], preferred_element_type=F32).astype(q_ref.dtype)
    heads = [slice(h * dh, (h + 1) * dh) for h in range(XA_HEADS)]
    scores = [lax.dot_general(q_ref[:, hs], kv_ref[:, hs], NT_DIMS, preferred_element_type=F32)
              for hs in heads]
    for h, hs in enumerate(heads):
        s = scores[h] * scale
        e = jnp.exp(s - jnp.max(s, axis=-1, keepdims=True))
        o = jnp.dot(e.astype(BF16), kv_ref[:, d + h * dh:d + (h + 1) * dh], preferred_element_type=F32)
        a_ref[:, hs] = (o / jnp.sum(e, axis=-1, keepdims=True)).astype(a_ref.dtype)
    go = go_ref[...]
    for rows in halves:
        y = jnp.dot(a_ref[rows, :], wo_ref[...], preferred_element_type=F32)
        o_ref[rows, :] = x_ref[rows, :] + _rms(y, go)


def xattn(x, kv, gq, wq, go, wo, layer, tq):
    b, l, d = x.shape
    nm = kv.shape[1]
    vec = lambda: _resident((1, d), lambda bi, i: (0, 0))
    mat = lambda: _resident((None, d, d), lambda bi, i: (layer, 0, 0))
    return pl.pallas_call(
        _xattn_kernel,
        grid=(b, l // tq),
        in_specs=[
            pl.BlockSpec((None, tq, d), lambda bi, i: (bi, i, 0)),
            vec(), mat(),
            pl.BlockSpec((None, nm, 2 * d), lambda bi, i: (bi, 0, 0)),
            mat(), vec(),
        ],
        out_specs=pl.BlockSpec((None, tq, d), lambda bi, i: (bi, i, 0)),
        out_shape=jax.ShapeDtypeStruct((b, l, d), F32),
        scratch_shapes=[pltpu.VMEM((tq, d), BF16)] * 2,
        compiler_params=_params("parallel", "parallel"),
        name="xattn",
    )(x, gq.reshape(1, d), wq, kv, wo, go.reshape(1, d))


def _rel_bucket(dist):
    n = jnp.maximum(dist, 0)
    max_exact = REL_BUCKETS // 2
    nf = jnp.maximum(n, 1).astype(F32)
    large = max_exact + (jnp.log(nf / max_exact) / math.log(REL_MAX_DIST / max_exact)
                         * (REL_BUCKETS - max_exact)).astype(jnp.int32)
    large = jnp.minimum(large, REL_BUCKETS - 1)
    return jnp.where(n < max_exact, n, large)


def _bias_correction_tiles(rel_table, tq):
    t = jnp.arange(tq)
    d0 = t[:, None] - t[None, :]
    buckets = jnp.stack([_rel_bucket(d0), _rel_bucket(d0 + tq)], axis=0)
    onehot = (buckets[..., None] == jnp.arange(REL_BUCKETS)).astype(F32)
    far = rel_table[_rel_bucket(jnp.int32(2 * tq))]
    return jnp.einsum("ntsk,kh->hnst", onehot, (rel_table - far) * LOG2E,
                      precision=lax.Precision.HIGHEST)


def _chunk_sum_matrices(tl):
    r = jnp.arange(tl)
    same = (r[:, None] // GLA_CHUNK) == (r[None, :] // GLA_CHUNK)
    return (same & (r[None, :] <= r[:, None])).astype(BF16)


def _split_in_proj(w):
    depth, d, _ = w.shape
    half = d // 2
    sizes = [GLA_HEADS * LANES, GLA_HEADS * LANES, half, GLA_RANK, half,
             half, half, half, IDX_HEADS * IDX_DH, IDX_DH, IDX_HEADS]
    offs = [0]
    for s in sizes:
        offs.append(offs[-1] + s)
    assert offs[-1] == w.shape[2]
    seg = lambda n: w[:, :, offs[n]:offs[n + 1]]
    zero = lambda n: jnp.zeros((depth, d, n), w.dtype)
    ik = seg(9)
    w_head = w[:, :, :offs[3]]
    w_mid = w[:, :, offs[5]:offs[9]]
    w_small = jnp.concatenate([seg(4), seg(3), zero(LANES - GLA_RANK), seg(10), zero(LANES - IDX_HEADS),
                               ik, zero(2 * IDX_DH), ik], axis=2)
    return w_head.astype(BF16), w_mid.astype(BF16), w_small.astype(BF16)


def kernel(x, mem, norm_gains, w_in, w_gate_up, b_gate, gla_head_gain, rel_table,
           w_out, w_xq, w_xkv, w_xo, w_ffn_in, w_ffn_out):
    b, l, d = x.shape
    nm = mem.shape[1]
    depth = w_in.shape[0]
    d_ff = w_ffn_out.shape[1]
    m = b * l
    assert d // 2 // GLA_HEADS == 2 * LANES and d // 2 // DSA_HEADS == LANES

    t = TILES
    corr = _bias_correction_tiles(rel_table, t["dsa_q"])
    tri = _chunk_sum_matrices(t["gla_t"])
    w2p = jnp.pad(w_gate_up, ((0, 0), (0, LANES - GLA_RANK), (0, 0))).astype(BF16)

    w_out, w_xq, w_xkv, w_xo, w_ffn_in, w_ffn_out = (
        w.astype(BF16) for w in (w_out, w_xq, w_xkv, w_xo, w_ffn_in, w_ffn_out))
    w_head, w_mid, w_small = _split_in_proj(w_in)

    xf = x.reshape(m, d)
    memf = mem.reshape(b * nm, d)
    for li in range(depth):
        g = norm_gains[li]
        assert t["in_proj"][1] == DSA_HEADS * LANES
        pa, pb = in_proj(xf, g[0], w_head, w_mid, w_small, li, DSA_LOGIT_SCALE, *t["in_proj"])
        pa, pb = pa.reshape(b, l, -1), pb.reshape(b, l, -1)
        o_gla = gla(pa, pb, w2p[li], b_gate[li].reshape(1, -1),
                    gla_head_gain[li].reshape(1, -1), tri, t["gla_t"], heads_per_step=GLA_HEADS)
        o_dsa = dsa(pa, pb, corr, t["dsa_q"])
        xf = mm_norm_res([o_gla.reshape(m, -1), o_dsa.reshape(m, -1)], w_out, li, g[1], xf,
                         bm=t["out_proj_m"])
        kv = norm_matmul(memf, g[6], w_xkv, li, BF16, bm=min(t["mem_kv"][0], b * nm), bn=t["mem_kv"][1])
        xf = xattn(xf.reshape(b, l, d), kv.reshape(b, nm, 2 * d), g[2], w_xq, g[3], w_xo, li,
                   tq=t["xattn_q"]).reshape(m, d)
        act = ffn_in(xf, g[4], w_ffn_in, li, d_ff, *t["ffn_in"])
        xf = mm_norm_res([act], w_ffn_out, li, g[5], xf, bm=t["ffn_out_m"])
    return xf.reshape(b, l, d)
```

```python
import functools
import math

import jax
import jax.numpy as jnp
from jax import lax
from jax.experimental import pallas as pl
from jax.experimental.pallas import tpu as pltpu

F32 = jnp.float32
BF16 = jnp.bfloat16

EPS = 1e-6
GLA_HEADS = 4
GLA_RANK = 16
GLA_TAU = 16.0
GLA_CHUNK = 64
DSA_HEADS = 8
IDX_HEADS = 16
IDX_DH = 64
TOPK_MAX = 256
REL_BUCKETS = 32
REL_MAX_DIST = 128
XA_HEADS = 4

LANES = 128
SUBLANES = 8
VMEM_LIMIT_BYTES = 60000 * 1024

PACK16 = 16
ROW_CHUNK = 128
NORM_PARTS = 4

NEG_BIG = -1e30
HALF16 = 2 ** 15
LOG2E = 1.4426950408889634
DSA_LOGIT_SCALE = LANES ** -0.5 * LOG2E

TILES = dict(
    in_proj=(1024, 1024, 768),
    gla_t=512,
    dsa_q=256,
    out_proj_m=1024,
    mem_kv=(512, 1024),
    xattn_q=512,
    ffn_in=(1024, 512),
    ffn_out_m=512,
)

NT_DIMS = (((1,), (1,)), ((), ()))
TN_DIMS = (((0,), (0,)), ((), ()))


def _params(*sem):
    return pltpu.CompilerParams(dimension_semantics=sem, vmem_limit_bytes=VMEM_LIMIT_BYTES)


def _resident(shape, index_map):
    return pl.BlockSpec(shape, index_map, pipeline_mode=pl.Buffered(1))


def _rms(x, g):
    return x * lax.rsqrt(jnp.mean(x * x, axis=-1, keepdims=True) + EPS) * g


def _row_parts(n_rows, parts):
    size = n_rows // parts
    return [slice(p * size, (p + 1) * size) for p in range(parts)]


def _rms_to(x_ref, g_ref, h_ref):
    g = g_ref[...]

    def body(c, _):
        rows = pl.ds(pl.multiple_of(c * ROW_CHUNK, ROW_CHUNK), ROW_CHUNK)
        h_ref[rows, :] = _rms(x_ref[rows, :], g).astype(h_ref.dtype)
        return 0

    lax.fori_loop(0, x_ref.shape[0] // ROW_CHUNK, body, 0)


def _norm_matmul_kernel(x_ref, g_ref, w_ref, o_ref, h_ref):
    @pl.when(pl.program_id(1) == 0)
    def _():
        _rms_to(x_ref, g_ref, h_ref)

    o_ref[...] = jnp.dot(h_ref[...], w_ref[...], preferred_element_type=F32).astype(o_ref.dtype)


def norm_matmul(x, g, w, layer, out_dtype, bm, bn):
    m, k = x.shape
    n = w.shape[2]
    assert m % bm == 0 and n % bn == 0
    return pl.pallas_call(
        _norm_matmul_kernel,
        grid=(m // bm, n // bn),
        in_specs=[
            pl.BlockSpec((bm, k), lambda i, j: (i, 0)),
            _resident((1, k), lambda i, j: (0, 0)),
            pl.BlockSpec((None, k, bn), lambda i, j: (layer, 0, j)),
        ],
        out_specs=pl.BlockSpec((bm, bn), lambda i, j: (i, j)),
        out_shape=jax.ShapeDtypeStruct((m, n), out_dtype),
        scratch_shapes=[pltpu.VMEM((bm, k), BF16)],
        compiler_params=_params("parallel", "arbitrary"),
        name="norm_matmul",
    )(x, g.reshape(1, k), w)


def _in_proj_slot(i, j, n_slots):
    return jnp.where(i % 2 == 0, j, n_slots - 1 - j)


def _in_proj_kernel(x_ref, g_ref, wh_ref, wm_ref, ws_ref, oa_ref, ob_ref, h_ref, *, nh, ns, nm,
                    mid0_scale):
    s = _in_proj_slot(pl.program_id(0), pl.program_id(1), nh + ns + nm)

    first = pl.program_id(1) == 0

    def project(w_ref, o_ref, scale=None):
        y = jnp.dot(h_ref[...], w_ref[...], preferred_element_type=F32)
        o_ref[...] = (y if scale is None else y * scale).astype(o_ref.dtype)

    def project_first(w_ref, o_ref):
        g = g_ref[...]
        for rows in _row_parts(x_ref.shape[0], NORM_PARTS):
            h = _rms(x_ref[rows, :], g).astype(h_ref.dtype)
            h_ref[rows, :] = h
            o_ref[rows, :] = jnp.dot(h, w_ref[...], preferred_element_type=F32).astype(o_ref.dtype)

    pl.when(first & (s < nh))(lambda: project_first(wh_ref, oa_ref))
    pl.when(first & (s >= nh + ns))(lambda: project_first(wm_ref, oa_ref))
    later = jnp.logical_not(first)
    pl.when(later & (s < nh))(lambda: project(wh_ref, oa_ref))
    pl.when((s >= nh) & (s < nh + ns))(lambda: project(ws_ref, ob_ref))
    pl.when(later & (s == nh + ns))(lambda: project(wm_ref, oa_ref, mid0_scale))
    pl.when(later & (s > nh + ns))(lambda: project(wm_ref, oa_ref))


def in_proj(x, g, w_head, w_mid, w_small, layer, mid0_scale, bm, bn, bs):
    m, k = x.shape
    n_head, n_mid, n_small = w_head.shape[2], w_mid.shape[2], w_small.shape[2]
    assert m % bm == 0 and n_head % bn == 0 and n_mid % bn == 0 and n_small % bs == 0
    nh, nm, ns = n_head // bn, n_mid // bn, n_small // bs
    n_slots = nh + ns + nm
    assert nm > 1

    def slot(i, j):
        return _in_proj_slot(i, j, n_slots)

    def pa_block(i, j):
        s = slot(i, j)
        hold = jnp.where(i % 2 == 0, nh - 1, nh)
        return jnp.where(s < nh, s, jnp.where(s >= nh + ns, s - ns, hold))

    return pl.pallas_call(
        functools.partial(_in_proj_kernel, nh=nh, ns=ns, nm=nm, mid0_scale=mid0_scale),
        grid=(m // bm, n_slots),
        in_specs=[
            pl.BlockSpec((bm, k), lambda i, j: (i, 0)),
            _resident((1, k), lambda i, j: (0, 0)),
            pl.BlockSpec((None, k, bn), lambda i, j: (layer, 0, jnp.clip(slot(i, j), 0, nh - 1))),
            pl.BlockSpec((None, k, bn), lambda i, j: (layer, 0, jnp.clip(slot(i, j) - nh - ns, 0, nm - 1))),
            pl.BlockSpec((None, k, bs), lambda i, j: (layer, 0, jnp.clip(slot(i, j) - nh, 0, ns - 1))),
        ],
        out_specs=[
            pl.BlockSpec((bm, bn), lambda i, j: (i, pa_block(i, j))),
            pl.BlockSpec((bm, bs), lambda i, j: (i, jnp.clip(slot(i, j) - nh, 0, ns - 1))),
        ],
        out_shape=[
            jax.ShapeDtypeStruct((m, n_head + n_mid), BF16),
            jax.ShapeDtypeStruct((m, n_small), F32),
        ],
        scratch_shapes=[pltpu.VMEM((bm, k), BF16)],
        compiler_params=_params("parallel", "arbitrary"),
        name="in_proj",
    )(x, g.reshape(1, k), w_head, w_mid, w_small)


def _ffn_in_kernel(x_ref, g_ref, wg_ref, wu_ref, o_ref, h_ref):
    def gated(h):
        a = jnp.dot(h, wg_ref[...], preferred_element_type=F32)
        b = jnp.dot(h, wu_ref[...], preferred_element_type=F32)
        return (a * (1.0 / (1.0 + jnp.exp(-a))) * b).astype(o_ref.dtype)

    @pl.when(pl.program_id(1) == 0)
    def _():
        g = g_ref[...]
        for rows in _row_parts(x_ref.shape[0], NORM_PARTS):
            h = _rms(x_ref[rows, :], g).astype(h_ref.dtype)
            h_ref[rows, :] = h
            o_ref[rows, :] = gated(h)

    @pl.when(pl.program_id(1) > 0)
    def _():
        o_ref[...] = gated(h_ref[...])


def ffn_in(x, g, w, layer, d_ff, bm, bn):
    m, k = x.shape
    assert m % bm == 0 and d_ff % bn == 0
    nj = d_ff // bn
    return pl.pallas_call(
        _ffn_in_kernel,
        grid=(m // bm, nj),
        in_specs=[
            pl.BlockSpec((bm, k), lambda i, j: (i, 0)),
            _resident((1, k), lambda i, j: (0, 0)),
            pl.BlockSpec((None, k, bn), lambda i, j: (layer, 0, j)),
            pl.BlockSpec((None, k, bn), lambda i, j: (layer, 0, j + nj)),
        ],
        out_specs=pl.BlockSpec((bm, bn), lambda i, j: (i, j)),
        out_shape=jax.ShapeDtypeStruct((m, d_ff), BF16),
        scratch_shapes=[pltpu.VMEM((bm, k), BF16)],
        compiler_params=_params("parallel", "arbitrary"),
        name="ffn_in",
    )(x, g.reshape(1, k), w, w)


def _mm_norm_res_kernel(*refs, n_lhs, nk):
    lhs = refs[:n_lhs]
    w_ref, g_ref, res_ref, o_ref = refs[n_lhs:n_lhs + 4]
    if nk == 1:
        g = g_ref[...]
        half = o_ref.shape[0] // 2
        for rows in (slice(0, half), slice(half, 2 * half)):
            y = None
            off = 0
            for a in lhs:
                wd = a.shape[1]
                part = jnp.dot(a[rows, :], w_ref[off:off + wd, :], preferred_element_type=F32)
                y = part if y is None else y + part
                off += wd
            o_ref[rows, :] = res_ref[rows, :] + _rms(y, g)
    else:
        k = pl.program_id(1)
        part = jnp.dot(lhs[0][...], w_ref[...], preferred_element_type=F32)

        @pl.when(k == 0)
        def _():
            o_ref[...] = part

        @pl.when(k > 0)
        def _():
            o_ref[...] += part

        @pl.when(k == nk - 1)
        def _():
            g = g_ref[...]

            def finish(c, _):
                rows = pl.ds(pl.multiple_of(c * ROW_CHUNK, ROW_CHUNK), ROW_CHUNK)
                o_ref[rows, :] = res_ref[rows, :] + _rms(o_ref[rows, :], g)
                return 0

            lax.fori_loop(0, o_ref.shape[0] // ROW_CHUNK, finish, 0)


def mm_norm_res(lhs_list, w, layer, g, res, bm, nk=1):
    m, n = res.shape
    kk = w.shape[1]
    assert m % bm == 0 and kk % nk == 0
    bk = kk // nk
    n_lhs = len(lhs_list)
    if nk == 1:
        lhs_specs = [pl.BlockSpec((bm, a.shape[1]), lambda i, k: (i, 0)) for a in lhs_list]
        w_spec = _resident((None, kk, n), lambda i, k: (layer, 0, 0))
    else:
        assert n_lhs == 1
        kblock = lambda i, k: jnp.where(i % 2 == 0, k, nk - 1 - k)
        lhs_specs = [pl.BlockSpec((bm, bk), lambda i, k: (i, kblock(i, k)))]
        w_spec = pl.BlockSpec((None, bk, n), lambda i, k: (layer, kblock(i, k), 0))
    return pl.pallas_call(
        functools.partial(_mm_norm_res_kernel, n_lhs=n_lhs, nk=nk),
        grid=(m // bm, nk),
        in_specs=lhs_specs + [
            w_spec,
            _resident((1, n), lambda i, k: (0, 0)),
            pl.BlockSpec((bm, n), lambda i, k: (i, 0)),
        ],
        out_specs=pl.BlockSpec((bm, n), lambda i, k: (i, 0)),
        out_shape=jax.ShapeDtypeStruct((m, n), F32),
        compiler_params=_params("parallel", "arbitrary"),
        name="mm_norm_res",
    )(*lhs_list, w, g.reshape(1, n), res)


def _gla_kernel(q_ref, k_ref, v_ref, z_ref, r_ref, w2_ref, bg_ref, gh_ref, tri_ref,
                o_ref, st_ref, *, dk, dv):
    tl = q_ref.shape[0]
    heads = q_ref.shape[1] // dk
    n_chunks = tl // GLA_CHUNK

    @pl.when(pl.program_id(2) == 0)
    def _():
        st_ref[...] = jnp.zeros_like(st_ref)

    pre = jnp.dot(z_ref[...].astype(BF16), w2_ref[...], preferred_element_type=F32) + bg_ref[...]
    logg = (jnp.minimum(pre, 0.0) - jnp.log1p(jnp.exp(-jnp.abs(pre)))) * (1.0 / GLA_TAU)
    hi = logg.astype(BF16)
    lo = (logg - hi.astype(F32)).astype(BF16)
    incl = tri_ref[...]
    cs = jnp.dot(incl, jnp.concatenate([hi, lo], axis=1), preferred_element_type=F32)
    bcum_all = cs[:, :heads * dk] + cs[:, heads * dk:]
    gh = gh_ref[...]

    for hh in range(heads):
        ks = slice(hh * dk, (hh + 1) * dk)
        vs = slice(hh * dv, (hh + 1) * dv)
        bcum = bcum_all[:, ks]
        b_last = [bcum[(c + 1) * GLA_CHUNK - 1:(c + 1) * GLA_CHUNK, :] for c in range(n_chunks)]
        suf = jnp.concatenate([jnp.broadcast_to(bl, (GLA_CHUNK, dk)) for bl in b_last], axis=0) - bcum

        qf = q_ref[:, ks].astype(F32) * dk ** -0.5
        kf = k_ref[:, ks].astype(F32)
        q_dec = (qf * jnp.exp(bcum)).astype(BF16)
        k_intra = (kf * jnp.exp(-bcum)).astype(BF16)
        k_state = (kf * jnp.exp(suf)).astype(BF16)
        v = v_ref[:, vs]

        a = lax.dot_general(q_dec, k_intra, NT_DIMS, preferred_element_type=F32)
        a = jnp.where(incl > 0, a, 0.0).astype(BF16)
        o_intra = jnp.dot(a, v, preferred_element_type=F32)

        st = st_ref[hh]
        outs = []
        for c in range(n_chunks):
            rows = slice(c * GLA_CHUNK, (c + 1) * GLA_CHUNK)
            o_inter = lax.dot_general(q_dec[rows], st.astype(BF16), NT_DIMS,
                                      preferred_element_type=F32)
            outs.append(o_intra[rows] + o_inter)
            decay = jnp.exp(b_last[c])
            upd = lax.dot_general(v[rows], k_state[rows], TN_DIMS, preferred_element_type=F32)
            st = decay * st + upd
        st_ref[hh] = st
        o = _rms(jnp.concatenate(outs, axis=0), gh)
        r = r_ref[:, vs]
        o_ref[:, vs] = (o * (r * (1.0 / (1.0 + jnp.exp(-r))))).astype(o_ref.dtype)


def gla(pa, pb, w2p, bgate, ghead, tri, tl, heads_per_step):
    b, l, _ = pa.shape
    dk = LANES
    dv = 2 * dk
    hp = heads_per_step
    assert GLA_HEADS % hp == 0
    groups = GLA_HEADS // hp
    wk, wv = hp * dk, hp * dv
    kb0 = GLA_HEADS * dk // wk
    vb0 = 2 * GLA_HEADS * dk // wv
    zb = GLA_HEADS * dv // LANES
    return pl.pallas_call(
        functools.partial(_gla_kernel, dk=dk, dv=dv),
        grid=(b, groups, l // tl),
        in_specs=[
            pl.BlockSpec((None, tl, wk), lambda bi, h, t: (bi, t, h)),
            pl.BlockSpec((None, tl, wk), lambda bi, h, t: (bi, t, kb0 + h)),
            pl.BlockSpec((None, tl, wv), lambda bi, h, t: (bi, t, vb0 + h)),
            pl.BlockSpec((None, tl, LANES), lambda bi, h, t: (bi, t, zb)),
            pl.BlockSpec((None, tl, wv), lambda bi, h, t: (bi, t, h)),
            pl.BlockSpec((LANES, wk), lambda bi, h, t: (0, h)),
            pl.BlockSpec((1, wk), lambda bi, h, t: (0, h)),
            _resident((1, dv), lambda bi, h, t: (0, 0)),
            _resident((tl, tl), lambda bi, h, t: (0, 0)),
        ],
        out_specs=pl.BlockSpec((None, tl, wv), lambda bi, h, t: (bi, t, h)),
        out_shape=jax.ShapeDtypeStruct((b, l, GLA_HEADS * dv), BF16),
        scratch_shapes=[pltpu.VMEM((hp, dv, dk), F32)],
        compiler_params=_params("parallel", "parallel", "arbitrary"),
        name="gla",
    )(pa, pa, pa, pb, pb, w2p, bgate, ghead, tri)


def _for_each_block_paired(n, body):
    def quad(p, _):
        for u in range(4):
            body(4 * p + u)
        return 0

    lax.fori_loop(0, n >> 2, quad, 0)
    base = (n >> 2) << 2

    @pl.when(n & 2 == 2)
    def _():
        body(base)
        body(base + 1)

    @pl.when(n & 1 == 1)
    def _():
        body(n - 1)


def _dsa_kernel(q_ref, iq_ref, iw_ref, k_ref, v_ref, ik_ref, corr_ref, o_ref,
                key_ref, hi_ref, lo_ref, madd_ref, wt_ref, m_ref, l_ref, acc_ref, vt_ref, s_ref,
                *, topk):
    tq = q_ref.shape[0]
    tk = tq
    dh = LANES
    i = pl.program_id(1)
    nkb = i + 1

    wt_ref[...] = (iw_ref[...] * (IDX_HEADS ** -0.5 * IDX_DH ** -0.5)).T
    qpos = lax.broadcasted_iota(jnp.int32, (tk, tq), 1) + i * tq
    kpos0 = lax.broadcasted_iota(jnp.int32, (tk, tq), 0)

    def score_block(kb):
        ik = ik_ref[pl.ds(pl.multiple_of(kb * tk, tk), tk), :].astype(BF16)
        ik_even, ik_odd = ik[:, :LANES], ik[:, LANES:]
        acc = jnp.zeros((tk, tq), F32)
        for p in range(IDX_HEADS // 2):
            qp = iq_ref[:, p * LANES:(p + 1) * LANES]
            for par, ikx in ((0, ik_even), (1, ik_odd)):
                h = 2 * p + par
                d = lax.dot_general(ikx, qp, NT_DIMS, preferred_element_type=F32)
                acc = acc + jnp.maximum(d, 0.0) * wt_ref[h:h + 1, :]
        sc = jnp.where(kpos0 + kb * tk <= qpos, acc, -jnp.inf)
        bits = lax.bitcast_convert_type(sc, jnp.int32)
        key = bits ^ ((bits >> 31) & 0x7FFFFFFF)
        key_ref[kb] = key
        hi_ref[kb] = (key >> 16).astype(jnp.int16)
        lo_ref[kb] = ((key & 0xFFFF) - HALF16).astype(jnp.int16)

    _for_each_block_paired(nkb, score_block)

    n_acc = 4
    slabs = tk // PACK16
    one16, zero16 = jnp.int16(1), jnp.int16(0)

    def rows16(x):
        return jnp.broadcast_to(x, (PACK16, tq)).astype(jnp.int16)

    def count16(src_ref, cand):
        cand16 = rows16(cand)

        def block(kb, cnts):
            cnts = list(cnts)
            for r in range(slabs):
                hit = src_ref[kb, r * PACK16:(r + 1) * PACK16, :] >= cand16
                cnts[r % n_acc] = cnts[r % n_acc] + jnp.where(hit, one16, zero16)
            return tuple(cnts)

        def pair(p, cnts):
            return block(2 * p + 1, block(2 * p, cnts))

        cnts = lax.fori_loop(0, nkb >> 1, pair, (jnp.zeros((PACK16, tq), jnp.int16),) * n_acc)
        cnts = lax.cond(nkb & 1 == 1, lambda c: block(nkb - 1, c), lambda c: c, cnts)
        cnt = ((cnts[0] + cnts[1]) + (cnts[2] + cnts[3])).astype(jnp.int32)
        return jnp.sum(cnt, axis=0, keepdims=True)

    def bisect16(src_ref, kth):
        def step(s, carry):
            t, above = carry
            cand = t + lax.shift_left(jnp.int32(1), 15 - s)
            total = count16(src_ref, cand)
            ok = total >= kth
            return jnp.where(ok, cand, t), jnp.where(ok, above, total)

        return lax.fori_loop(0, 16, step, (jnp.full((1, tq), -HALF16, jnp.int32),
                                           jnp.zeros((1, tq), jnp.int32)))

    t_hi, above = bisect16(hi_ref, topk)
    t_hi16 = rows16(t_hi)

    def keep_threshold_bucket(kb, _):
        for r in range(slabs):
            rows = slice(r * PACK16, (r + 1) * PACK16)
            lo_ref[kb, rows, :] = jnp.where(hi_ref[kb, rows, :] == t_hi16, lo_ref[kb, rows, :],
                                            jnp.int16(-HALF16))
        return 0

    lax.fori_loop(0, nkb, keep_threshold_bucket, 0)
    t_lo, _ = bisect16(lo_ref, topk - above)
    thr = t_hi * (2 * HALF16) + (t_lo + HALF16)

    def mask_block(kb, _):
        vis = kpos0 + kb * tk <= qpos
        madd_ref[kb] = jnp.where(vis, jnp.where(key_ref[kb] >= thr, 0.0, NEG_BIG), NEG_BIG)
        return 0

    lax.fori_loop(0, nkb, mask_block, 0)

    @pl.when(i == 0)
    def _():
        def transpose_values(kb, _):
            vt_ref[kb] = v_ref[pl.ds(pl.multiple_of(kb * tk, tk), tk), :].T
            return 0

        lax.fori_loop(0, vt_ref.shape[0], transpose_values, 0)

    m_ref[...] = jnp.full(m_ref.shape, NEG_BIG, F32)
    l_ref[...] = jnp.zeros(l_ref.shape, F32)
    acc_ref[...] = jnp.zeros(acc_ref.shape, F32)

    ones_rows = jnp.ones((PACK16, tk), BF16)

    def attend(kb, near, buf=0):
        keys = pl.ds(pl.multiple_of(kb * tk, tk), tk)
        madd = madd_ref[kb]
        for h in range(DSA_HEADS):
            hs = slice(h * dh, (h + 1) * dh)
            s_ref[buf, h] = lax.dot_general(k_ref[keys, hs], q_ref[:, hs], NT_DIMS,
                                            preferred_element_type=F32)
        for h in range(DSA_HEADS):
            hs = slice(h * dh, (h + 1) * dh)
            s = s_ref[buf, h]
            if near is not None:
                s = s + corr_ref[h, near]
            s = s + madd
            m_prev = m_ref[h, 0:1, :]
            m_next = jnp.maximum(m_prev, jnp.max(s, axis=0, keepdims=True))
            alpha = jnp.exp2(m_prev - m_next)
            p = jnp.exp2(s - m_next)
            pv = jnp.dot(jnp.concatenate([vt_ref[kb, hs, :], ones_rows], axis=0), p.astype(BF16),
                         preferred_element_type=F32)
            l_next = alpha * l_ref[h, 0:1, :] + pv[dh:dh + 1, :]
            acc_ref[h] = alpha * acc_ref[h] + pv[:dh, :]
            m_ref[h] = jnp.broadcast_to(m_next, (SUBLANES, tq))
            l_ref[h] = jnp.broadcast_to(l_next, (SUBLANES, tq))

    def attend_two(kb0):
        blocks = (kb0, kb0 + 1)
        for buf, kb in enumerate(blocks):
            keys = pl.ds(pl.multiple_of(kb * tk, tk), tk)
            for h in range(DSA_HEADS):
                hs = slice(h * dh, (h + 1) * dh)
                s_ref[buf, h] = lax.dot_general(k_ref[keys, hs], q_ref[:, hs], NT_DIMS,
                                                preferred_element_type=F32)
        madds = [madd_ref[kb] for kb in blocks]
        for h in range(DSA_HEADS):
            hs = slice(h * dh, (h + 1) * dh)
            ss = [s_ref[buf, h] + madds[buf] for buf in range(2)]
            m_prev = m_ref[h, 0:1, :]
            m_next = jnp.maximum(m_prev, jnp.maximum(jnp.max(ss[0], axis=0, keepdims=True),
                                                     jnp.max(ss[1], axis=0, keepdims=True)))
            alpha = jnp.exp2(m_prev - m_next)
            p = jnp.concatenate([jnp.exp2(sb - m_next).astype(BF16) for sb in ss], axis=0)
            values = jnp.concatenate(
                [jnp.concatenate([vt_ref[kb, hs, :], ones_rows], axis=0) for kb in blocks], axis=1)
            pv = jnp.dot(values, p, preferred_element_type=F32)
            l_next = alpha * l_ref[h, 0:1, :] + pv[dh:dh + 1, :]
            acc_ref[h] = alpha * acc_ref[h] + pv[:dh, :]
            m_ref[h] = jnp.broadcast_to(m_next, (SUBLANES, tq))
            l_ref[h] = jnp.broadcast_to(l_next, (SUBLANES, tq))

    def far_pair(p, _):
        attend_two(2 * p)
        return 0

    n_far = jnp.maximum(i - 1, 0)
    lax.fori_loop(0, n_far >> 1, far_pair, 0)

    @pl.when(n_far & 1 == 1)
    def _():
        attend(n_far - 1, None)

    @pl.when(i >= 1)
    def _():
        attend(i - 1, 1, 0)
        attend(i, 0, 1)

    @pl.when(i == 0)
    def _():
        attend(i, 0)

    for h in range(DSA_HEADS):
        o_h = acc_ref[h] / l_ref[h, 0:1, :]
        o_ref[:, h * dh:(h + 1) * dh] = o_h.T.astype(o_ref.dtype)


def dsa(pa, pb, corr, tq):
    b, l, _ = pa.shape
    dh = LANES
    hd = DSA_HEADS * dh
    qb = (2 * GLA_HEADS * LANES + GLA_HEADS * 2 * LANES) // hd
    iwb = GLA_HEADS * 2 * LANES // LANES + 1
    ikb = (iwb + 1) * LANES // (2 * LANES)
    topk = min(TOPK_MAX, l // 4)
    assert l % tq == 0 and tq >= topk and tq >= REL_MAX_DIST
    nkb = l // tq
    assert nkb * tq // PACK16 < HALF16
    return pl.pallas_call(
        functools.partial(_dsa_kernel, topk=topk),
        grid=(b, l // tq),
        in_specs=[
            pl.BlockSpec((None, tq, hd), lambda bi, i: (bi, i, qb)),
            pl.BlockSpec((None, tq, IDX_HEADS * IDX_DH), lambda bi, i: (bi, i, qb + 3)),
            pl.BlockSpec((None, tq, LANES), lambda bi, i: (bi, i, iwb)),
            _resident((None, l, hd), lambda bi, i: (bi, 0, qb + 1)),
            _resident((None, l, hd), lambda bi, i: (bi, 0, qb + 2)),
            _resident((None, l, 2 * LANES), lambda bi, i: (bi, 0, ikb)),
            _resident((DSA_HEADS, 2, tq, tq), lambda bi, i: (0, 0, 0, 0)),
        ],
        out_specs=pl.BlockSpec((None, tq, hd), lambda bi, i: (bi, i, 0)),
        out_shape=jax.ShapeDtypeStruct((b, l, hd), BF16),
        scratch_shapes=[
            pltpu.VMEM((nkb, tq, tq), jnp.int32),
            pltpu.VMEM((nkb, tq, tq), jnp.int16),
            pltpu.VMEM((nkb, tq, tq), jnp.int16),
            pltpu.VMEM((nkb, tq, tq), F32),
            pltpu.VMEM((LANES, tq), F32),
            pltpu.VMEM((DSA_HEADS, SUBLANES, tq), F32),
            pltpu.VMEM((DSA_HEADS, SUBLANES, tq), F32),
            pltpu.VMEM((DSA_HEADS, dh, tq), F32),
            pltpu.VMEM((nkb, hd, tq), BF16),
            pltpu.VMEM((2, DSA_HEADS, tq, tq), F32),
        ],
        compiler_params=_params("parallel", "arbitrary"),
        name="dsa",
    )(pa, pa, pb, pa, pa, pb, corr)


def _xattn_kernel(x_ref, gq_ref, wq_ref, kv_ref, wo_ref, go_ref, o_ref, q_ref, a_ref):
    d = x_ref.shape[1]
    dh = d // XA_HEADS
    scale = dh ** -0.5
    halves = _row_parts(x_ref.shape[0], 2)
    gq = gq_ref[...]
    for rows in halves:
        h = _rms(x_ref[rows, :], gq).astype(BF16)
        q_ref[rows, :] = jnp.dot(h, wq_ref[...], preferred_element_type=F32).astype(q_ref.dtype)
    heads = [slice(h * dh, (h + 1) * dh) for h in range(XA_HEADS)]
    scores = [lax.dot_general(q_ref[:, hs], kv_ref[:, hs], NT_DIMS, preferred_element_type=F32)
              for hs in heads]
    for h, hs in enumerate(heads):
        s = scores[h] * scale
        e = jnp.exp(s - jnp.max(s, axis=-1, keepdims=True))
        o = jnp.dot(e.astype(BF16), kv_ref[:, d + h * dh:d + (h + 1) * dh], preferred_element_type=F32)
        a_ref[:, hs] = (o / jnp.sum(e, axis=-1, keepdims=True)).astype(a_ref.dtype)
    go = go_ref[...]
    for rows in halves:
        y = jnp.dot(a_ref[rows, :], wo_ref[...], preferred_element_type=F32)
        o_ref[rows, :] = x_ref[rows, :] + _rms(y, go)


def xattn(x, kv, gq, wq, go, wo, layer, tq):
    b, l, d = x.shape
    nm = kv.shape[1]
    vec = lambda: _resident((1, d), lambda bi, i: (0, 0))
    mat = lambda: _resident((None, d, d), lambda bi, i: (layer, 0, 0))
    return pl.pallas_call(
        _xattn_kernel,
        grid=(b, l // tq),
        in_specs=[
            pl.BlockSpec((None, tq, d), lambda bi, i: (bi, i, 0)),
            vec(), mat(),
            pl.BlockSpec((None, nm, 2 * d), lambda bi, i: (bi, 0, 0)),
            mat(), vec(),
        ],
        out_specs=pl.BlockSpec((None, tq, d), lambda bi, i: (bi, i, 0)),
        out_shape=jax.ShapeDtypeStruct((b, l, d), F32),
        scratch_shapes=[pltpu.VMEM((tq, d), BF16)] * 2,
        compiler_params=_params("parallel", "parallel"),
        name="xattn",
    )(x, gq.reshape(1, d), wq, kv, wo, go.reshape(1, d))


def _rel_bucket(dist):
    n = jnp.maximum(dist, 0)
    max_exact = REL_BUCKETS // 2
    nf = jnp.maximum(n, 1).astype(F32)
    large = max_exact + (jnp.log(nf / max_exact) / math.log(REL_MAX_DIST / max_exact)
                         * (REL_BUCKETS - max_exact)).astype(jnp.int32)
    large = jnp.minimum(large, REL_BUCKETS - 1)
    return jnp.where(n < max_exact, n, large)


def _bias_correction_tiles(rel_table, tq):
    t = jnp.arange(tq)
    d0 = t[:, None] - t[None, :]
    buckets = jnp.stack([_rel_bucket(d0), _rel_bucket(d0 + tq)], axis=0)
    onehot = (buckets[..., None] == jnp.arange(REL_BUCKETS)).astype(F32)
    far = rel_table[_rel_bucket(jnp.int32(2 * tq))]
    return jnp.einsum("ntsk,kh->hnst", onehot, (rel_table - far) * LOG2E,
                      precision=lax.Precision.HIGHEST)


def _chunk_sum_matrices(tl):
    r = jnp.arange(tl)
    same = (r[:, None] // GLA_CHUNK) == (r[None, :] // GLA_CHUNK)
    return (same & (r[None, :] <= r[:, None])).astype(BF16)


def _split_in_proj(w):
    depth, d, _ = w.shape
    half = d // 2
    sizes = [GLA_HEADS * LANES, GLA_HEADS * LANES, half, GLA_RANK, half,
             half, half, half, IDX_HEADS * IDX_DH, IDX_DH, IDX_HEADS]
    offs = [0]
    for s in sizes:
        offs.append(offs[-1] + s)
    assert offs[-1] == w.shape[2]
    seg = lambda n: w[:, :, offs[n]:offs[n + 1]]
    zero = lambda n: jnp.zeros((depth, d, n), w.dtype)
    ik = seg(9)
    w_head = w[:, :, :offs[3]]
    w_mid = w[:, :, offs[5]:offs[9]]
    w_small = jnp.concatenate([seg(4), seg(3), zero(LANES - GLA_RANK), seg(10), zero(LANES - IDX_HEADS),
                               ik, zero(2 * IDX_DH), ik], axis=2)
    return w_head.astype(BF16), w_mid.astype(BF16), w_small.astype(BF16)


def kernel(x, mem, norm_gains, w_in, w_gate_up, b_gate, gla_head_gain, rel_table,
           w_out, w_xq, w_xkv, w_xo, w_ffn_in, w_ffn_out):
    b, l, d = x.shape
    nm = mem.shape[1]
    depth = w_in.shape[0]
    d_ff = w_ffn_out.shape[1]
    m = b * l
    assert d // 2 // GLA_HEADS == 2 * LANES and d // 2 // DSA_HEADS == LANES

    t = TILES
    corr = _bias_correction_tiles(rel_table, t["dsa_q"])
    tri = _chunk_sum_matrices(t["gla_t"])
    w2p = jnp.pad(w_gate_up, ((0, 0), (0, LANES - GLA_RANK), (0, 0))).astype(BF16)

    w_out, w_xq, w_xkv, w_xo, w_ffn_in, w_ffn_out = (
        w.astype(BF16) for w in (w_out, w_xq, w_xkv, w_xo, w_ffn_in, w_ffn_out))
    w_head, w_mid, w_small = _split_in_proj(w_in)

    xf = x.reshape(m, d)
    memf = mem.reshape(b * nm, d)
    for li in range(depth):
        g = norm_gains[li]
        assert t["in_proj"][1] == DSA_HEADS * LANES
        pa, pb = in_proj(xf, g[0], w_head, w_mid, w_small, li, DSA_LOGIT_SCALE, *t["in_proj"])
        pa, pb = pa.reshape(b, l, -1), pb.reshape(b, l, -1)
        o_gla = gla(pa, pb, w2p[li], b_gate[li].reshape(1, -1),
                    gla_head_gain[li].reshape(1, -1), tri, t["gla_t"], heads_per_step=GLA_HEADS)
        o_dsa = dsa(pa, pb, corr, t["dsa_q"])
        xf = mm_norm_res([o_gla.reshape(m, -1), o_dsa.reshape(m, -1)], w_out, li, g[1], xf,
                         bm=t["out_proj_m"])
        kv = norm_matmul(memf, g[6], w_xkv, li, BF16, bm=min(t["mem_kv"][0], b * nm), bn=t["mem_kv"][1])
        xf = xattn(xf.reshape(b, l, d), kv.reshape(b, nm, 2 * d), g[2], w_xq, g[3], w_xo, li,
                   tq=t["xattn_q"]).reshape(m, d)
        act = ffn_in(xf, g[4], w_ffn_in, li, d_ff, *t["ffn_in"])
        xf = mm_norm_res([act], w_ffn_out, li, g[5], xf, bm=t["ffn_out_m"])
    return xf.reshape(b, l, d)
```
